```python
import math
import jax, jax.numpy as jnp
from jax import lax
import numpy as np

D_MODEL = 1024
BATCH = 2
SEQ = 8192
DEPTH = 1
DEC_BATCH = 128
DEC_SEQ = 8
PAST_LEN = 2048
PAGE_SIZE = 128

D_MIX = D_MODEL
D_ATT = D_MIX // 2
D_CONV = D_MIX - D_ATT
N_HEADS = 4
HEAD_DIM = D_ATT // (2 * N_HEADS)
V_DIM = 2 * HEAD_DIM
CONV_W = 31
Q_BLOCK = 128
EPS = 1e-5
SPLITS = (D_ATT, 2 * D_ATT, 3 * D_ATT, 4 * D_ATT, 4 * D_ATT + D_CONV, 4 * D_ATT + 2 * D_CONV)
D_IN = 4 * D_ATT + 3 * D_CONV

kernel_name = 'hymba_diffattn_conformer_step'


def rmsnorm(x, g):
    xf = x.astype(jnp.float32)
    y = xf * lax.rsqrt(jnp.mean(xf * xf, axis=-1, keepdims=True) + EPS)
    return y.astype(x.dtype) * g


def layernorm(x, g, b):
    xf = x.astype(jnp.float32)
    mu = jnp.mean(xf, axis=-1, keepdims=True)
    var = jnp.mean(jnp.square(xf - mu), axis=-1, keepdims=True)
    return ((xf - mu) * lax.rsqrt(var + EPS)).astype(x.dtype) * g + b


def in_project(h, w_in):
    n, t = h.shape[0], h.shape[1]
    z = h @ w_in
    q, k, v, g_att, a, b, g_conv = jnp.split(z, SPLITS, axis=-1)
    q = q.reshape(n, t, N_HEADS, 2, HEAD_DIM)
    k = k.reshape(n, t, N_HEADS, V_DIM)
    v = v.reshape(n, t, N_HEADS, V_DIM)
    u = a * jax.nn.sigmoid(b)
    return q, k, v, g_att, u, g_conv


def diff_attend(q, k, v, q_pos, k_pos, lam):
    k = k.reshape(k.shape[:3] + (2, HEAD_DIM))
    s = jnp.einsum('nqhmd,nkhmd->nhmqk', q, k).astype(jnp.float32) * (HEAD_DIM ** -0.5)
    mask = k_pos[None, :] <= q_pos[:, None]
    s = jnp.where(mask, s, -jnp.inf)
    p = jax.nn.softmax(s, axis=-1)
    a = p[:, :, 0] - lam * p[:, :, 1]
    return jnp.einsum('nhqk,nkhe->nqhe', a.astype(v.dtype), v)


def attn_post(o, subln_g, lam_init):
    n, t = o.shape[0], o.shape[1]
    o = rmsnorm(o, subln_g) * (1.0 - lam_init)
    return o.reshape(n, t, D_ATT)


def conv_branch(u, buf, dw_w, dw_b, ln_g, ln_b, w_pw2, b_pw2):
    u_ext = jnp.concatenate([buf, u], axis=1)
    c = lax.conv_general_dilated(u_ext, dw_w[:, None, :], window_strides=(1,), padding='VALID',
                                 dimension_numbers=('NWC', 'WIO', 'NWC'),
                                 feature_group_count=D_CONV) + dw_b
    c = jax.nn.silu(layernorm(c, ln_g, ln_b))
    return c @ w_pw2 + b_pw2, u_ext[:, -(CONV_W - 1):]


def mix_out(att, conv, g_att, g_conv, w_out):
    m = jnp.concatenate([att * jax.nn.silu(g_att), conv * jax.nn.silu(g_conv)], axis=-1)
    return m @ w_out


def setup_inputs(seed: int = 0) -> dict:
    key = jax.random.key(seed)
    ks = jax.random.split(key, 24)
    n_pages = PAST_LEN // PAGE_SIZE
    n_used = DEC_BATCH * n_pages
    n_pool = n_used + max(1, n_used // 4)
    f32 = jnp.float32
    nrm = lambda k, s, sc: jax.random.normal(k, s, f32) * sc
    page_table = jax.random.permutation(ks[5], n_pool)[:n_used].reshape(DEC_BATCH, n_pages).astype(jnp.int32)
    return {
        'x_prompt': nrm(ks[0], (BATCH, SEQ, D_MODEL), 1.0),
        'x_sample': nrm(ks[1], (DEC_BATCH, DEC_SEQ, D_MODEL), 1.0),
        'cache_k': nrm(ks[2], (DEPTH, n_pool, PAGE_SIZE, N_HEADS, V_DIM), 1.0),
        'cache_v': nrm(ks[3], (DEPTH, n_pool, PAGE_SIZE, N_HEADS, V_DIM), 1.0),
        'state_conv': nrm(ks[4], (DEPTH, DEC_BATCH, CONV_W - 1, D_CONV), 0.5),
        'page_table': page_table,
        'norm_g': 1.0 + nrm(ks[6], (DEPTH, D_MODEL), 0.02),
        'w_in': nrm(ks[7], (DEPTH, D_MODEL, D_IN), D_MODEL ** -0.5),
        'lambda_q1': nrm(ks[8], (DEPTH, HEAD_DIM), 0.1),
        'lambda_k1': nrm(ks[9], (DEPTH, HEAD_DIM), 0.1),
        'lambda_q2': nrm(ks[10], (DEPTH, HEAD_DIM), 0.1),
        'lambda_k2': nrm(ks[11], (DEPTH, HEAD_DIM), 0.1),
        'subln_g': 1.0 + nrm(ks[12], (DEPTH, V_DIM), 0.02),
        'dw_w': nrm(ks[13], (DEPTH, CONV_W, D_CONV), CONV_W ** -0.5),
        'dw_b': nrm(ks[14], (DEPTH, D_CONV), 0.02),
        'conv_ln_g': 1.0 + nrm(ks[15], (DEPTH, D_CONV), 0.02),
        'conv_ln_b': nrm(ks[16], (DEPTH, D_CONV), 0.02),
        'w_pw2': nrm(ks[17], (DEPTH, D_CONV, D_CONV), D_CONV ** -0.5),
        'b_pw2': nrm(ks[18], (DEPTH, D_CONV), 0.02),
        'w_out': nrm(ks[19], (DEPTH, D_MIX, D_MODEL), D_MIX ** -0.5),
        'final_norm_g': 1.0 + nrm(ks[20], (D_MODEL,), 0.02),
    }


def reference(x_prompt, x_sample, cache_k, cache_v, state_conv, page_table, norm_g, w_in,
              lambda_q1, lambda_k1, lambda_q2, lambda_k2, subln_g, dw_w, dw_b, conv_ln_g,
              conv_ln_b, w_pw2, b_pw2, w_out, final_norm_g):
    yp, ys = x_prompt, x_sample
    kp_l, vp_l, cp_l, ks_l, vs_l, cs_l = [], [], [], [], [], []
    n_blocks = SEQ // Q_BLOCK
    for l in range(DEPTH):
        lam_init = 0.8 - 0.6 * math.exp(-0.3 * l)
        f = lambda t: t.astype(jnp.float32)
        lam = (jnp.exp(jnp.sum(f(lambda_q1[l]) * f(lambda_k1[l])))
               - jnp.exp(jnp.sum(f(lambda_q2[l]) * f(lambda_k2[l]))) + lam_init)
        conv_p = (dw_w[l], dw_b[l], conv_ln_g[l], conv_ln_b[l], w_pw2[l], b_pw2[l])

        h = rmsnorm(yp, norm_g[l])
        q, k, v, g_att, u, g_conv = in_project(h, w_in[l])
        k_pos = jnp.arange(SEQ)

        def q_block(i, q=q, k=k, v=v, k_pos=k_pos, lam=lam):
            qb = lax.dynamic_slice_in_dim(q, i * Q_BLOCK, Q_BLOCK, axis=1)
            return diff_attend(qb, k, v, i * Q_BLOCK + jnp.arange(Q_BLOCK), k_pos, lam)

        o = lax.map(q_block, jnp.arange(n_blocks))
        o = jnp.moveaxis(o, 0, 1).reshape(BATCH, SEQ, N_HEADS, V_DIM)
        att = attn_post(o, subln_g[l], lam_init)
        zero_buf = jnp.zeros((BATCH, CONV_W - 1, D_CONV), u.dtype)
        conv, buf_p = conv_branch(u, zero_buf, *conv_p)
        yp = yp + mix_out(att, conv, g_att, g_conv, w_out[l])
        kp_l.append(k); vp_l.append(v); cp_l.append(buf_p)

        h = rmsnorm(ys, norm_g[l])
        q, k, v, g_att, u, g_conv = in_project(h, w_in[l])
        k_past = cache_k[l][page_table].reshape(DEC_BATCH, PAST_LEN, N_HEADS, V_DIM)
        v_past = cache_v[l][page_table].reshape(DEC_BATCH, PAST_LEN, N_HEADS, V_DIM)
        keys = jnp.concatenate([k_past, k], axis=1)
        vals = jnp.concatenate([v_past, v], axis=1)
        o = diff_attend(q, keys, vals, PAST_LEN + jnp.arange(DEC_SEQ),
                        jnp.arange(PAST_LEN + DEC_SEQ), lam)
        att = attn_post(o, subln_g[l], lam_init)
        conv, buf_s = conv_branch(u, state_conv[l], *conv_p)
        ys = ys + mix_out(att, conv, g_att, g_conv, w_out[l])
        ks_l.append(k); vs_l.append(v); cs_l.append(buf_s)

    y_prompt = rmsnorm(yp, final_norm_g)
    y_sample = rmsnorm(ys, final_norm_g)
    return (y_prompt, y_sample, jnp.stack(kp_l), jnp.stack(vp_l), jnp.stack(cp_l),
            jnp.stack(ks_l), jnp.stack(vs_l), jnp.stack(cs_l))
```

```python
import functools
import math

import jax
import jax.numpy as jnp
from jax import lax
from jax.experimental import pallas as pl
from jax.experimental.pallas import tpu as pltpu

N_HEADS = 4
HEAD_DIM = 64
V_DIM = 2 * HEAD_DIM
CONV_W = 31
EPS = 1e-5
LAM_INIT = 0.8 - 0.6 * math.exp(-0.3 * 0)

LANES = 128
SUBLANES = 8
VMEM_LIMIT_BYTES = 56 * 1024 * 1024

CONV_HALO = 32
NEG_INF = float("-inf")


def _silu(x):
    return x * jax.nn.sigmoid(x)


def _dot(a, b):
    return jnp.dot(a, b, preferred_element_type=jnp.float32)


def _dot_nt(a, b):
    return lax.dot_general(a, b, (((1,), (1,)), ((), ())), preferred_element_type=jnp.float32)


def _in_proj_kernel(x_ref, g_ref, w_ref, q_ref, k_ref, v_ref, sga_ref, u_ref, sgc_ref, *bf16_refs):
    x = x_ref[...]
    h = x * lax.rsqrt(jnp.mean(x * x, axis=-1, keepdims=True) + EPS) * g_ref[...]
    hb = h.astype(jnp.bfloat16)
    c = q_ref.shape[-1]

    def col(i):
        return _dot(hb, w_ref[:, i * c:(i + 1) * c])

    q_ref[...] = col(0) * (HEAD_DIM ** -0.5)
    k = col(1)
    v = col(2)
    k_ref[...] = k
    v_ref[...] = v
    if bf16_refs:
        kb_ref, vb_ref = bf16_refs
        kb_ref[...] = k.astype(jnp.bfloat16)
        vb_ref[...] = v.astype(jnp.bfloat16)
    sga_ref[...] = _silu(col(3))
    u_ref[...] = col(4) * jax.nn.sigmoid(col(5))
    sgc_ref[...] = _silu(col(6))


def _in_proj(x2d, norm_g, w_in_bf16, *, block_rows, emit_bf16_kv):
    rows, d_model = x2d.shape
    c = w_in_bf16.shape[1] // 7
    row_spec = pl.BlockSpec((block_rows, c), lambda i: (i, 0))
    n_f32 = 6
    out_shape = [jax.ShapeDtypeStruct((rows, c), jnp.float32)] * n_f32
    out_specs = [row_spec] * n_f32
    if emit_bf16_kv:
        out_shape += [jax.ShapeDtypeStruct((rows, c), jnp.bfloat16)] * 2
        out_specs += [row_spec] * 2
    return pl.pallas_call(
        _in_proj_kernel,
        grid=(rows // block_rows,),
        in_specs=[
            pl.BlockSpec((block_rows, d_model), lambda i: (i, 0)),
            pl.BlockSpec((1, d_model), lambda i: (0, 0)),
            pl.BlockSpec(w_in_bf16.shape, lambda i: (0, 0)),
        ],
        out_specs=out_specs,
        out_shape=out_shape,
        compiler_params=pltpu.CompilerParams(
            dimension_semantics=("arbitrary",), vmem_limit_bytes=VMEM_LIMIT_BYTES),
        name="in_proj",
    )(x2d, norm_g, w_in_bf16)


def _lam(lq1_ref, lk1_ref, lq2_ref, lk2_ref):
    s1 = jnp.sum(lq1_ref[...] * lk1_ref[...], axis=-1, keepdims=True)
    s2 = jnp.sum(lq2_ref[...] * lk2_ref[...], axis=-1, keepdims=True)
    return jnp.exp(s1) - jnp.exp(s2) + LAM_INIT


def _split_maps(q):
    lane = lax.broadcasted_iota(jnp.int32, q.shape, 1)
    zero = jnp.zeros_like(q)
    return jnp.where(lane < HEAD_DIM, q, zero), jnp.where(lane >= HEAD_DIM, q, zero)


def _head_norm(o, subln_g):
    y = o * lax.rsqrt(jnp.mean(o * o, axis=-1, keepdims=True) + EPS)
    return y * subln_g * (1.0 - LAM_INIT)


def _prompt_attn_kernel(q_ref, k_ref, v_ref, lq1_ref, lk1_ref, lq2_ref, lk2_ref, g_ref, o_ref,
                        m_ref, l_ref, acc_ref, *, block_q, block_k):
    qi = pl.program_id(2)
    q1, q2 = _split_maps(q_ref[...].astype(jnp.bfloat16))
    qs = (q1, q2)

    m_ref[...] = jnp.full_like(m_ref, NEG_INF)
    l_ref[...] = jnp.zeros_like(l_ref)
    acc_ref[...] = jnp.zeros_like(acc_ref)

    def step(kc, masked):
        start = pl.multiple_of(kc * block_k, block_k)
        k = k_ref[pl.ds(start, block_k), :]
        v = v_ref[pl.ds(start, block_k), :]
        for mp in range(2):
            s = _dot_nt(qs[mp], k)
            if masked:
                row = lax.broadcasted_iota(jnp.int32, s.shape, 0)
                colk = lax.broadcasted_iota(jnp.int32, s.shape, 1)
                s = jnp.where(colk <= row, s, NEG_INF)
            m_prev = m_ref[mp]
            m_new = jnp.maximum(m_prev, jnp.max(s, axis=-1, keepdims=True))
            alpha = jnp.exp(m_prev - m_new)
            p = jnp.exp(s - m_new)
            l_ref[mp] = alpha * l_ref[mp] + jnp.sum(p, axis=-1, keepdims=True)
            acc_ref[mp] = alpha * acc_ref[mp] + _dot(p.astype(jnp.bfloat16), v)
            m_ref[mp] = m_new

    def full_step(kc, carry):
        step(kc, masked=False)
        return carry

    lax.fori_loop(0, qi, full_step, 0)
    step(qi, masked=True)

    lam = _lam(lq1_ref, lk1_ref, lq2_ref, lk2_ref)
    o = acc_ref[0] / l_ref[0] - lam * (acc_ref[1] / l_ref[1])
    o_ref[...] = _head_norm(o, g_ref[...])


def _prompt_attn(q, kb, vb, lams, subln_g, *, batch, seq, block):
    rows, d_att = q.shape
    nq = seq // block
    vec = lambda n: pl.BlockSpec((1, n), lambda b, h, i: (0, 0))
    kernel = functools.partial(_prompt_attn_kernel, block_q=block, block_k=block)
    return pl.pallas_call(
        kernel,
        grid=(batch, N_HEADS, nq),
        in_specs=[
            pl.BlockSpec((block, V_DIM), lambda b, h, i: (b * nq + i, h)),
            pl.BlockSpec((seq, V_DIM), lambda b, h, i: (b, h)),
            pl.BlockSpec((seq, V_DIM), lambda b, h, i: (b, h)),
            vec(HEAD_DIM), vec(HEAD_DIM), vec(HEAD_DIM), vec(HEAD_DIM), vec(V_DIM),
        ],
        out_specs=pl.BlockSpec((block, V_DIM), lambda b, h, i: (b * nq + i, h)),
        out_shape=jax.ShapeDtypeStruct((rows, d_att), jnp.float32),
        scratch_shapes=[
            pltpu.VMEM((2, block, 1), jnp.float32),
            pltpu.VMEM((2, block, 1), jnp.float32),
            pltpu.VMEM((2, block, V_DIM), jnp.float32),
        ],
        compiler_params=pltpu.CompilerParams(
            dimension_semantics=("arbitrary", "arbitrary", "arbitrary"),
            vmem_limit_bytes=VMEM_LIMIT_BYTES),
        name="prompt_attn",
    )(q, kb, vb, *lams, subln_g)


def _sample_attn_kernel(pt_ref, q_ref, kn_ref, vn_ref, lq1_ref, lk1_ref, lq2_ref, lk2_ref, g_ref,
                        ck_hbm, cv_hbm, o_ref, kbuf, vbuf, sem, *, n_pages, page):
    s_idx = pl.program_id(0)
    n_seq = pl.num_programs(0)
    slot = s_idx % 2

    def page_copies(seq, slot_):
        copies = []
        for p in range(n_pages):
            pid = pt_ref[seq * n_pages + p]
            dst = pl.ds(p * page, page)
            copies.append(pltpu.make_async_copy(ck_hbm.at[pid], kbuf.at[slot_, dst], sem.at[0, slot_]))
            copies.append(pltpu.make_async_copy(cv_hbm.at[pid], vbuf.at[slot_, dst], sem.at[1, slot_]))
        return copies

    @pl.when(s_idx == 0)
    def _():
        for cp in page_copies(s_idx, slot):
            cp.start()

    @pl.when(s_idx + 1 < n_seq)
    def _():
        for cp in page_copies(s_idx + 1, 1 - slot):
            cp.start()

    for cp in page_copies(s_idx, slot):
        cp.wait()

    lam = _lam(lq1_ref, lk1_ref, lq2_ref, lk2_ref)
    t = q_ref.shape[0]
    row = lax.broadcasted_iota(jnp.int32, (t, t), 0)
    colk = lax.broadcasted_iota(jnp.int32, (t, t), 1)
    causal = colk <= row
    for h in range(N_HEADS):
        hs = slice(h * V_DIM, (h + 1) * V_DIM)
        qh = q_ref[:, hs].astype(jnp.bfloat16)
        k_past = kbuf[slot, :, hs].astype(jnp.bfloat16)
        v_past = vbuf[slot, :, hs].astype(jnp.bfloat16)
        k_new = kn_ref[:, hs].astype(jnp.bfloat16)
        v_new = vn_ref[:, hs].astype(jnp.bfloat16)
        probs = []
        for qm in _split_maps(qh):
            s_past = _dot_nt(qm, k_past)
            s_new = jnp.where(causal, _dot_nt(qm, k_new), NEG_INF)
            m = jnp.maximum(jnp.max(s_past, axis=-1, keepdims=True),
                            jnp.max(s_new, axis=-1, keepdims=True))
            e_past = jnp.exp(s_past - m)
            e_new = jnp.exp(s_new - m)
            denom = (jnp.sum(e_past, axis=-1, keepdims=True)
                     + jnp.sum(e_new, axis=-1, keepdims=True))
            probs.append((e_past / denom, e_new / denom))
        a_past = probs[0][0] - lam * probs[1][0]
        a_new = probs[0][1] - lam * probs[1][1]
        o = _dot(a_past.astype(jnp.bfloat16), v_past) + _dot(a_new.astype(jnp.bfloat16), v_new)
        o_ref[:, hs] = _head_norm(o, g_ref[...])


def _sample_attn(page_table, q, k_new, v_new, lams, subln_g, cache_k, cache_v, *, n_seq, n_new):
    n_pages = page_table.shape[1]
    n_pool, page, d_att = cache_k.shape
    past = n_pages * page
    row_spec = pl.BlockSpec((n_new, d_att), lambda s, pt: (s, 0))
    vec = lambda n: pl.BlockSpec((1, n), lambda s, pt: (0, 0))
    kernel = functools.partial(_sample_attn_kernel, n_pages=n_pages, page=page)
    return pl.pallas_call(
        kernel,
        grid_spec=pltpu.PrefetchScalarGridSpec(
            num_scalar_prefetch=1,
            grid=(n_seq,),
            in_specs=[
                row_spec, row_spec, row_spec,
                vec(HEAD_DIM), vec(HEAD_DIM), vec(HEAD_DIM), vec(HEAD_DIM), vec(V_DIM),
                pl.BlockSpec(memory_space=pl.ANY),
                pl.BlockSpec(memory_space=pl.ANY),
            ],
            out_specs=row_spec,
            scratch_shapes=[
                pltpu.VMEM((2, past, d_att), jnp.float32),
                pltpu.VMEM((2, past, d_att), jnp.float32),
                pltpu.SemaphoreType.DMA((2, 2)),
            ],
        ),
        out_shape=jax.ShapeDtypeStruct(q.shape, jnp.float32),
        compiler_params=pltpu.CompilerParams(
            dimension_semantics=("arbitrary",), vmem_limit_bytes=VMEM_LIMIT_BYTES),
        name="sample_attn",
    )(page_table.reshape(-1), q, k_new, v_new, *lams, subln_g, cache_k, cache_v)


def _mix_tail(c, x, att, sga, sgc, ln_g_ref, ln_b_ref, wp_ref, bp_ref, wo_att_ref, wo_conv_ref,
              fg_ref):
    mu = jnp.mean(c, axis=-1, keepdims=True)
    d = c - mu
    var = jnp.mean(d * d, axis=-1, keepdims=True)
    ln = d * lax.rsqrt(var + EPS) * ln_g_ref[...] + ln_b_ref[...]
    conv = _dot(_silu(ln).astype(jnp.bfloat16), wp_ref[...]) + bp_ref[...]
    mixed = (_dot((att * sga).astype(jnp.bfloat16), wo_att_ref[...])
             + _dot((conv * sgc).astype(jnp.bfloat16), wo_conv_ref[...]))
    y = x + mixed
    return y * lax.rsqrt(jnp.mean(y * y, axis=-1, keepdims=True) + EPS) * fg_ref[...]


def _mix_prompt_kernel(x_ref, att_ref, sga_ref, u_ref, halo_ref, sgc_ref, dw_ref, db_ref,
                       ln_g_ref, ln_b_ref, wp_ref, bp_ref, wo_att_ref, wo_conv_ref, fg_ref,
                       y_ref, ext_ref, *, tiles_per_seq):
    rows = u_ref.shape[0]
    first = pl.program_id(0) % tiles_per_seq == 0

    @pl.when(first)
    def _():
        ext_ref[0:CONV_HALO, :] = jnp.zeros((CONV_HALO, ext_ref.shape[1]), jnp.float32)

    @pl.when(jnp.logical_not(first))
    def _():
        ext_ref[0:CONV_HALO, :] = halo_ref[...]

    ext_ref[CONV_HALO:, :] = u_ref[...]
    base = CONV_HALO - (CONV_W - 1)
    c = jnp.broadcast_to(db_ref[...], (rows, ext_ref.shape[1]))
    for w in range(CONV_W):
        c = c + ext_ref[base + w:base + w + rows, :] * dw_ref[w:w + 1, :]
    y_ref[...] = _mix_tail(c, x_ref[...], att_ref[...], sga_ref[...], sgc_ref[...], ln_g_ref,
                           ln_b_ref, wp_ref, bp_ref, wo_att_ref, wo_conv_ref, fg_ref)


def _weight_specs(d_conv, d_model, d_att):
    full = lambda r, c: pl.BlockSpec((r, c), lambda i: (0, 0))
    return [
        full(CONV_W, d_conv), full(1, d_conv), full(1, d_conv), full(1, d_conv),
        full(d_conv, d_conv), full(1, d_conv),
        pl.BlockSpec((d_att, d_model), lambda i: (0, 0)),
        pl.BlockSpec((d_conv, d_model), lambda i: (d_att // d_conv, 0)),
        full(1, d_model),
    ]


def _mix_prompt(x2d, att, sga, u, sgc, weights, *, seq, block_rows):
    rows, d_model = x2d.shape
    d_att = att.shape[1]
    d_conv = u.shape[1]
    tiles_per_seq = seq // block_rows
    halo_blocks = block_rows // CONV_HALO
    row = lambda c: pl.BlockSpec((block_rows, c), lambda i: (i, 0))
    kernel = functools.partial(_mix_prompt_kernel, tiles_per_seq=tiles_per_seq)
    w_out = weights[6]
    return pl.pallas_call(
        kernel,
        grid=(rows // block_rows,),
        in_specs=[
            row(d_model), row(d_att), row(d_att), row(d_conv),
            pl.BlockSpec((CONV_HALO, d_conv), lambda i: (jnp.maximum(i * halo_blocks - 1, 0), 0)),
            row(d_conv),
        ] + _weight_specs(d_conv, d_model, d_att),
        out_specs=row(d_model),
        out_shape=jax.ShapeDtypeStruct((rows, d_model), jnp.float32),
        scratch_shapes=[pltpu.VMEM((CONV_HALO + block_rows, d_conv), jnp.float32)],
        compiler_params=pltpu.CompilerParams(
            dimension_semantics=("arbitrary",), vmem_limit_bytes=VMEM_LIMIT_BYTES),
        name="mix_prompt",
    )(x2d, att, sga, u, u, sgc, *weights[:6], w_out, w_out, weights[7])


def _mix_sample_kernel(x_ref, att_ref, sga_ref, u_ref, st_ref, sgc_ref, dw_ref, db_ref,
                       ln_g_ref, ln_b_ref, wp_ref, bp_ref, wo_att_ref, wo_conv_ref, fg_ref,
                       y_ref, st_out_ref, ext_ref, *, n_new):
    g = st_ref.shape[0]
    d_conv = st_ref.shape[2]
    hist = CONV_W - 1
    ext_ref[:, 0:hist, :] = st_ref[...]
    ext_ref[:, hist:hist + n_new, :] = u_ref[...].reshape(g, n_new, d_conv)
    c = jnp.broadcast_to(db_ref[...].reshape(1, 1, d_conv), (g, n_new, d_conv))
    for w in range(CONV_W):
        c = c + ext_ref[:, w:w + n_new, :] * dw_ref[w:w + 1, :].reshape(1, 1, d_conv)
    st_out_ref[...] = ext_ref[:, n_new:n_new + hist, :]
    y_ref[...] = _mix_tail(c.reshape(g * n_new, d_conv), x_ref[...], att_ref[...], sga_ref[...],
                           sgc_ref[...], ln_g_ref, ln_b_ref, wp_ref, bp_ref, wo_att_ref,
                           wo_conv_ref, fg_ref)


def _mix_sample(x2d, att, sga, u, state, sgc, weights, *, n_new, seqs_per_block):
    rows, d_model = x2d.shape
    d_att = att.shape[1]
    n_seq, hist, d_conv = state.shape
    block_rows = seqs_per_block * n_new
    row = lambda c: pl.BlockSpec((block_rows, c), lambda i: (i, 0))
    st_spec = pl.BlockSpec((seqs_per_block, hist, d_conv), lambda i: (i, 0, 0))
    ext_rows = -(-(hist + n_new) // SUBLANES) * SUBLANES
    kernel = functools.partial(_mix_sample_kernel, n_new=n_new)
    w_out = weights[6]
    return pl.pallas_call(
        kernel,
        grid=(n_seq // seqs_per_block,),
        in_specs=[row(d_model), row(d_att), row(d_att), row(d_conv), st_spec, row(d_conv)]
        + _weight_specs(d_conv, d_model, d_att),
        out_specs=[row(d_model), st_spec],
        out_shape=[jax.ShapeDtypeStruct((rows, d_model), jnp.float32),
                   jax.ShapeDtypeStruct(state.shape, jnp.float32)],
        scratch_shapes=[pltpu.VMEM((seqs_per_block, ext_rows, d_conv), jnp.float32)],
        compiler_params=pltpu.CompilerParams(
            dimension_semantics=("arbitrary",), vmem_limit_bytes=VMEM_LIMIT_BYTES),
        name="mix_sample",
    )(x2d, att, sga, u, state, sgc, *weights[:6], w_out, w_out, weights[7])


def kernel(x_prompt, x_sample, cache_k, cache_v, state_conv, page_table, norm_g, w_in, lambda_q1,
           lambda_k1, lambda_q2, lambda_k2, subln_g, dw_w, dw_b, conv_ln_g, conv_ln_b, w_pw2,
           b_pw2, w_out, final_norm_g):
    batch, seq, d_model = x_prompt.shape
    n_seq, n_new, _ = x_sample.shape
    depth, n_pool, page, n_heads, v_dim = cache_k.shape
    assert depth == 1 and n_heads == N_HEADS and v_dim == V_DIM
    d_att = N_HEADS * V_DIM
    d_conv = dw_w.shape[2]
    hist = CONV_W - 1
    bf16 = jnp.bfloat16

    norm_g2 = norm_g[0].reshape(1, d_model)
    w_in_b = w_in[0].astype(bf16)
    lams = [t[0].reshape(1, HEAD_DIM) for t in (lambda_q1, lambda_k1, lambda_q2, lambda_k2)]
    subln = subln_g[0].reshape(1, V_DIM)
    mix_w = [dw_w[0], dw_b[0].reshape(1, d_conv), conv_ln_g[0].reshape(1, d_conv),
             conv_ln_b[0].reshape(1, d_conv), w_pw2[0].astype(bf16), b_pw2[0].reshape(1, d_conv),
             w_out[0].astype(bf16), final_norm_g.reshape(1, d_model)]

    xp = x_prompt.reshape(batch * seq, d_model)
    q, k, v, sga, u, sgc, kb, vb = _in_proj(xp, norm_g2, w_in_b, block_rows=512, emit_bf16_kv=True)
    att = _prompt_attn(q, kb, vb, lams, subln, batch=batch, seq=seq, block=512)
    yp = _mix_prompt(xp, att, sga, u, sgc, mix_w, seq=seq, block_rows=256)
    kv_shape = (1, batch, seq, N_HEADS, V_DIM)
    conv_p = u.reshape(batch, seq, d_conv)[:, seq - hist:, :][None]

    xs = x_sample.reshape(n_seq * n_new, d_model)
    qs, ks, vs, sgas, us, sgcs = _in_proj(xs, norm_g2, w_in_b, block_rows=512, emit_bf16_kv=False)
    att_s = _sample_attn(page_table, qs, ks, vs, lams, subln,
                         cache_k[0].reshape(n_pool, page, d_att),
                         cache_v[0].reshape(n_pool, page, d_att), n_seq=n_seq, n_new=n_new)
    ys, conv_s = _mix_sample(xs, att_s, sgas, us, state_conv[0], sgcs, mix_w, n_new=n_new,
                             seqs_per_block=32)
    kvs_shape = (1, n_seq, n_new, N_HEADS, V_DIM)

    return (yp.reshape(batch, seq, d_model), ys.reshape(n_seq, n_new, d_model),
            k.reshape(kv_shape), v.reshape(kv_shape), conv_p,
            ks.reshape(kvs_shape), vs.reshape(kvs_shape), conv_s[None])
```

```python
import functools
import math

import jax
import jax.numpy as jnp
from jax import lax
from jax.experimental import pallas as pl
from jax.experimental.pallas import tpu as pltpu

N_HEADS = 4
HEAD_DIM = 64
V_DIM = 2 * HEAD_DIM
CONV_W = 31
EPS = 1e-5
LAM_INIT = 0.8 - 0.6 * math.exp(-0.3 * 0)

Q_SCALE = HEAD_DIM ** -0.5 * math.log2(math.e)

SUBLANES = 8
VMEM_LIMIT_BYTES = 56 * 1024 * 1024

CONV_HALO = 32
NEG_INF = float("-inf")


def _silu(x):
    return x * jax.nn.sigmoid(x)


def _dot(a, b):
    return jnp.dot(a, b, preferred_element_type=jnp.float32)


def _dot_nt(a, b):
    return lax.dot_general(a, b, (((1,), (1,)), ((), ())), preferred_element_type=jnp.float32)


def _in_proj_kernel(x_ref, g_ref, w_ref, q_ref, k_ref, v_ref, sga_ref, u_ref, sgc_ref, *attn_refs):
    x = x_ref[...]
    h = x * lax.rsqrt(jnp.mean(x * x, axis=-1, keepdims=True) + EPS) * g_ref[...]
    hb = h.astype(jnp.bfloat16)
    c = q_ref.shape[-1]

    def col(i):
        return _dot(hb, w_ref[:, i * c:(i + 1) * c])

    q_ref[...] = (col(0) * Q_SCALE).astype(q_ref.dtype)
    k = col(1)
    v = col(2)
    for h_idx in range(N_HEADS):
        hs = slice(h_idx * V_DIM, (h_idx + 1) * V_DIM)
        k_ref[:, h_idx, :] = k[:, hs]
        v_ref[:, h_idx, :] = v[:, hs]
    if attn_refs:
        kb_ref, vt_ref = attn_refs
        kb_ref[...] = k.astype(jnp.bfloat16)
        vt_ref[...] = v.T.astype(jnp.bfloat16)
    sga_ref[...] = _silu(col(3))
    u_ref[...] = col(4) * jax.nn.sigmoid(col(5))
    sgc_ref[...] = _silu(col(6))


def _in_proj(x2d, norm_g, w_in_bf16, *, block_rows, for_prompt):
    rows, d_model = x2d.shape
    c = w_in_bf16.shape[1] // 7
    row_spec = pl.BlockSpec((block_rows, c), lambda i: (i, 0))
    kv_spec = pl.BlockSpec((block_rows, N_HEADS, V_DIM), lambda i: (i, 0, 0))
    f32_rows = jax.ShapeDtypeStruct((rows, c), jnp.float32)
    kv_rows = jax.ShapeDtypeStruct((rows, N_HEADS, V_DIM), jnp.float32)
    q_dtype = jnp.bfloat16 if for_prompt else jnp.float32
    out_shape = [jax.ShapeDtypeStruct((rows, c), q_dtype), kv_rows, kv_rows, f32_rows, f32_rows,
                 f32_rows]
    out_specs = [row_spec, kv_spec, kv_spec, row_spec, row_spec, row_spec]
    if for_prompt:
        out_shape += [jax.ShapeDtypeStruct((rows, c), jnp.bfloat16),
                      jax.ShapeDtypeStruct((c, rows), jnp.bfloat16)]
        out_specs += [row_spec, pl.BlockSpec((c, block_rows), lambda i: (0, i))]
    return pl.pallas_call(
        _in_proj_kernel,
        grid=(rows // block_rows,),
        in_specs=[
            pl.BlockSpec((block_rows, d_model), lambda i: (i, 0)),
            pl.BlockSpec((1, d_model), lambda i: (0, 0)),
            pl.BlockSpec(w_in_bf16.shape, lambda i: (0, 0)),
        ],
        out_specs=out_specs,
        out_shape=out_shape,
        compiler_params=pltpu.CompilerParams(
            dimension_semantics=("arbitrary",), vmem_limit_bytes=VMEM_LIMIT_BYTES),
        name="in_proj",
    )(x2d, norm_g, w_in_bf16)


def _lam(lq1_ref, lk1_ref, lq2_ref, lk2_ref):
    s1 = jnp.sum(lq1_ref[...] * lk1_ref[...], axis=-1, keepdims=True)
    s2 = jnp.sum(lq2_ref[...] * lk2_ref[...], axis=-1, keepdims=True)
    return jnp.exp(s1) - jnp.exp(s2) + LAM_INIT


def _split_maps(q):
    lane = lax.broadcasted_iota(jnp.int32, q.shape, 1)
    zero = jnp.zeros_like(q)
    return jnp.where(lane < HEAD_DIM, q, zero), jnp.where(lane >= HEAD_DIM, q, zero)


def _prompt_attn_kernel(q_ref, k_ref, vt_ref, lq1_ref, lk1_ref, lq2_ref, lk2_ref, g_ref, o_ref,
                        m_ref, l_ref, acc_ref, *, block):
    qi = pl.program_id(2)
    qs = _split_maps(q_ref[...])

    m_ref[...] = jnp.full_like(m_ref, NEG_INF)
    l_ref[...] = jnp.zeros_like(l_ref)
    acc_ref[...] = jnp.zeros_like(acc_ref)

    def step(kc, masked):
        start = pl.multiple_of(kc * block, block)
        k = k_ref[pl.ds(start, block), :]
        vt = vt_ref[:, pl.ds(start, block)]
        for mp in range(2):
            s = _dot_nt(k, qs[mp])
            if masked:
                key = lax.broadcasted_iota(jnp.int32, s.shape, 0)
                qry = lax.broadcasted_iota(jnp.int32, s.shape, 1)
                s = jnp.where(key <= qry, s, NEG_INF)
            m_prev = m_ref[mp]
            m_new = jnp.maximum(m_prev, jnp.max(s, axis=0, keepdims=True))
            alpha = jnp.exp2(m_prev - m_new)
            p = jnp.exp2(s - m_new)
            l_ref[mp] = alpha * l_ref[mp] + jnp.sum(p, axis=0, keepdims=True)
            acc_ref[mp] = alpha * acc_ref[mp] + _dot(vt, p.astype(jnp.bfloat16))
            m_ref[mp] = m_new

    def full_step(kc, carry):
        step(kc, masked=False)
        return carry

    lax.fori_loop(0, qi, full_step, 0)
    step(qi, masked=True)

    lam = _lam(lq1_ref, lk1_ref, lq2_ref, lk2_ref)
    o = acc_ref[0] / l_ref[0] - lam * (acc_ref[1] / l_ref[1])
    y = o * lax.rsqrt(jnp.mean(o * o, axis=0, keepdims=True) + EPS)
    o_ref[...] = (y * g_ref[...] * (1.0 - LAM_INIT)).T


def _prompt_attn(q, kb, vt, lams, subln_g, *, batch, seq, block):
    rows, d_att = q.shape
    nq = seq // block
    vec = lambda n: pl.BlockSpec((1, n), lambda b, h, i: (0, 0))
    kernel = functools.partial(_prompt_attn_kernel, block=block)
    return pl.pallas_call(
        kernel,
        grid=(batch, N_HEADS, nq),
        in_specs=[
            pl.BlockSpec((block, V_DIM), lambda b, h, i: (b * nq + i, h)),
            pl.BlockSpec((seq, V_DIM), lambda b, h, i: (b, h)),
            pl.BlockSpec((V_DIM, seq), lambda b, h, i: (h, b)),
            vec(HEAD_DIM), vec(HEAD_DIM), vec(HEAD_DIM), vec(HEAD_DIM),
            pl.BlockSpec((V_DIM, 1), lambda b, h, i: (0, 0)),
        ],
        out_specs=pl.BlockSpec((block, V_DIM), lambda b, h, i: (b * nq + i, h)),
        out_shape=jax.ShapeDtypeStruct((rows, d_att), jnp.float32),
        scratch_shapes=[
            pltpu.VMEM((2, 1, block), jnp.float32),
            pltpu.VMEM((2, 1, block), jnp.float32),
            pltpu.VMEM((2, V_DIM, block), jnp.float32),
        ],
        compiler_params=pltpu.CompilerParams(
            dimension_semantics=("arbitrary", "arbitrary", "arbitrary"),
            vmem_limit_bytes=VMEM_LIMIT_BYTES),
        name="prompt_attn",
    )(q, kb, vt, *lams, subln_g.reshape(V_DIM, 1))


def _sample_attn_kernel(pt_ref, q_ref, kn_ref, vn_ref, lq1_ref, lk1_ref, lq2_ref, lk2_ref, g_ref,
                        ck_hbm, cv_hbm, o_ref, kbuf, vbuf, sem, *, n_pages, page):
    s_idx = pl.program_id(0)
    n_seq = pl.num_programs(0)
    slot = s_idx % 2

    def page_copies(seq, slot_):
        copies = []
        for p in range(n_pages):
            pid = pt_ref[seq * n_pages + p]
            dst = pl.ds(p * page, page)
            for h in range(N_HEADS):
                copies.append(pltpu.make_async_copy(
                    ck_hbm.at[0, pid, :, h, :], kbuf.at[slot_, h, dst], sem.at[0, slot_]))
                copies.append(pltpu.make_async_copy(
                    cv_hbm.at[0, pid, :, h, :], vbuf.at[slot_, h, dst], sem.at[1, slot_]))
        return copies

    @pl.when(s_idx == 0)
    def _():
        for cp in page_copies(s_idx, slot):
            cp.start()

    @pl.when(s_idx + 1 < n_seq)
    def _():
        for cp in page_copies(s_idx + 1, 1 - slot):
            cp.start()

    for cp in page_copies(s_idx, slot):
        cp.wait()

    lam = _lam(lq1_ref, lk1_ref, lq2_ref, lk2_ref)
    t = q_ref.shape[0]
    row = lax.broadcasted_iota(jnp.int32, (t, t), 0)
    colk = lax.broadcasted_iota(jnp.int32, (t, t), 1)
    causal = colk <= row
    for h in range(N_HEADS):
        hs = slice(h * V_DIM, (h + 1) * V_DIM)
        qh = q_ref[:, hs].astype(jnp.bfloat16)
        k_past = kbuf[slot, h].astype(jnp.bfloat16)
        v_past = vbuf[slot, h].astype(jnp.bfloat16)
        k_new = kn_ref[:, h, :].astype(jnp.bfloat16)
        v_new = vn_ref[:, h, :].astype(jnp.bfloat16)
        probs = []
        for qm in _split_maps(qh):
            s_past = _dot_nt(qm, k_past)
            s_new = jnp.where(causal, _dot_nt(qm, k_new), NEG_INF)
            m = jnp.maximum(jnp.max(s_past, axis=-1, keepdims=True),
                            jnp.max(s_new, axis=-1, keepdims=True))
            e_past = jnp.exp2(s_past - m)
            e_new = jnp.exp2(s_new - m)
            denom = (jnp.sum(e_past, axis=-1, keepdims=True)
                     + jnp.sum(e_new, axis=-1, keepdims=True))
            probs.append((e_past / denom, e_new / denom))
        a_past = probs[0][0] - lam * probs[1][0]
        a_new = probs[0][1] - lam * probs[1][1]
        o = _dot(a_past.astype(jnp.bfloat16), v_past) + _dot(a_new.astype(jnp.bfloat16), v_new)
        y = o * lax.rsqrt(jnp.mean(o * o, axis=-1, keepdims=True) + EPS)
        o_ref[:, hs] = y * g_ref[...] * (1.0 - LAM_INIT)


def _sample_attn(page_table, q, k_new, v_new, lams, subln_g, cache_k, cache_v, *, n_new):
    n_seq, n_pages = page_table.shape
    _, _, page, n_heads, v_dim = cache_k.shape
    past = n_pages * page
    d_att = q.shape[1]
    row_spec = pl.BlockSpec((n_new, d_att), lambda s, pt: (s, 0))
    kv_spec = pl.BlockSpec((n_new, n_heads, v_dim), lambda s, pt: (s, 0, 0))
    vec = lambda n: pl.BlockSpec((1, n), lambda s, pt: (0, 0))
    kernel = functools.partial(_sample_attn_kernel, n_pages=n_pages, page=page)
    return pl.pallas_call(
        kernel,
        grid_spec=pltpu.PrefetchScalarGridSpec(
            num_scalar_prefetch=1,
            grid=(n_seq,),
            in_specs=[
                row_spec, kv_spec, kv_spec,
                vec(HEAD_DIM), vec(HEAD_DIM), vec(HEAD_DIM), vec(HEAD_DIM), vec(V_DIM),
                pl.BlockSpec(memory_space=pl.ANY),
                pl.BlockSpec(memory_space=pl.ANY),
            ],
            out_specs=row_spec,
            scratch_shapes=[
                pltpu.VMEM((2, n_heads, past, v_dim), jnp.float32),
                pltpu.VMEM((2, n_heads, past, v_dim), jnp.float32),
                pltpu.SemaphoreType.DMA((2, 2)),
            ],
        ),
        out_shape=jax.ShapeDtypeStruct(q.shape, jnp.float32),
        compiler_params=pltpu.CompilerParams(
            dimension_semantics=("arbitrary",), vmem_limit_bytes=VMEM_LIMIT_BYTES),
        name="sample_attn",
    )(page_table.reshape(-1), q, k_new, v_new, *lams, subln_g, cache_k, cache_v)


def _mix_tail(c, x, att, sga, sgc, ln_g_ref, ln_b_ref, wp_ref, bp_ref, wo_att_ref, wo_conv_ref,
              fg_ref):
    mu = jnp.mean(c, axis=-1, keepdims=True)
    d = c - mu
    var = jnp.mean(d * d, axis=-1, keepdims=True)
    ln = d * lax.rsqrt(var + EPS) * ln_g_ref[...] + ln_b_ref[...]
    conv = _dot(_silu(ln).astype(jnp.bfloat16), wp_ref[...]) + bp_ref[...]
    mixed = (_dot((att * sga).astype(jnp.bfloat16), wo_att_ref[...])
             + _dot((conv * sgc).astype(jnp.bfloat16), wo_conv_ref[...]))
    y = x + mixed
    return y * lax.rsqrt(jnp.mean(y * y, axis=-1, keepdims=True) + EPS) * fg_ref[...]


def _mix_prompt_kernel(x_ref, att_ref, sga_ref, u_ref, halo_ref, sgc_ref, dw_ref, db_ref,
                       ln_g_ref, ln_b_ref, wp_ref, bp_ref, wo_att_ref, wo_conv_ref, fg_ref,
                       y_ref, ext_ref, *, tiles_per_seq):
    rows = u_ref.shape[0]
    first = pl.program_id(0) % tiles_per_seq == 0

    @pl.when(first)
    def _():
        ext_ref[0:CONV_HALO, :] = jnp.zeros((CONV_HALO, ext_ref.shape[1]), jnp.float32)

    @pl.when(jnp.logical_not(first))
    def _():
        ext_ref[0:CONV_HALO, :] = halo_ref[...]

    ext_ref[CONV_HALO:, :] = u_ref[...]
    base = CONV_HALO - (CONV_W - 1)
    c = jnp.broadcast_to(db_ref[...], (rows, ext_ref.shape[1]))
    for w in range(CONV_W):
        c = c + ext_ref[base + w:base + w + rows, :] * dw_ref[w:w + 1, :]
    y_ref[...] = _mix_tail(c, x_ref[...], att_ref[...], sga_ref[...], sgc_ref[...], ln_g_ref,
                           ln_b_ref, wp_ref, bp_ref, wo_att_ref, wo_conv_ref, fg_ref)


def _weight_specs(d_conv, d_model, d_att):
    full = lambda r, c: pl.BlockSpec((r, c), lambda i: (0, 0))
    return [
        full(CONV_W, d_conv), full(1, d_conv), full(1, d_conv), full(1, d_conv),
        full(d_conv, d_conv), full(1, d_conv),
        pl.BlockSpec((d_att, d_model), lambda i: (0, 0)),
        pl.BlockSpec((d_conv, d_model), lambda i: (d_att // d_conv, 0)),
        full(1, d_model),
    ]


def _mix_prompt(x2d, att, sga, u, sgc, weights, *, seq, block_rows):
    rows, d_model = x2d.shape
    d_att = att.shape[1]
    d_conv = u.shape[1]
    tiles_per_seq = seq // block_rows
    halo_blocks = block_rows // CONV_HALO
    row = lambda c: pl.BlockSpec((block_rows, c), lambda i: (i, 0))
    kernel = functools.partial(_mix_prompt_kernel, tiles_per_seq=tiles_per_seq)
    w_out = weights[6]
    return pl.pallas_call(
        kernel,
        grid=(rows // block_rows,),
        in_specs=[
            row(d_model), row(d_att), row(d_att), row(d_conv),
            pl.BlockSpec((CONV_HALO, d_conv), lambda i: (jnp.maximum(i * halo_blocks - 1, 0), 0)),
            row(d_conv),
        ] + _weight_specs(d_conv, d_model, d_att),
        out_specs=row(d_model),
        out_shape=jax.ShapeDtypeStruct((rows, d_model), jnp.float32),
        scratch_shapes=[pltpu.VMEM((CONV_HALO + block_rows, d_conv), jnp.float32)],
        compiler_params=pltpu.CompilerParams(
            dimension_semantics=("arbitrary",), vmem_limit_bytes=VMEM_LIMIT_BYTES),
        name="mix_prompt",
    )(x2d, att, sga, u, u, sgc, *weights[:6], w_out, w_out, weights[7])


def _mix_sample_kernel(x_ref, att_ref, sga_ref, u_ref, st_ref, sgc_ref, dw_ref, db_ref,
                       ln_g_ref, ln_b_ref, wp_ref, bp_ref, wo_att_ref, wo_conv_ref, fg_ref,
                       y_ref, st_out_ref, ext_ref, *, n_new):
    g = st_ref.shape[0]
    d_conv = st_ref.shape[2]
    hist = CONV_W - 1
    ext_ref[:, 0:hist, :] = st_ref[...]
    ext_ref[:, hist:hist + n_new, :] = u_ref[...].reshape(g, n_new, d_conv)
    c = jnp.broadcast_to(db_ref[...].reshape(1, 1, d_conv), (g, n_new, d_conv))
    for w in range(CONV_W):
        c = c + ext_ref[:, w:w + n_new, :] * dw_ref[w:w + 1, :].reshape(1, 1, d_conv)
    st_out_ref[...] = ext_ref[:, n_new:n_new + hist, :]
    y_ref[...] = _mix_tail(c.reshape(g * n_new, d_conv), x_ref[...], att_ref[...], sga_ref[...],
                           sgc_ref[...], ln_g_ref, ln_b_ref, wp_ref, bp_ref, wo_att_ref,
                           wo_conv_ref, fg_ref)


def _mix_sample(x2d, att, sga, u, state, sgc, weights, *, n_new, seqs_per_block):
    rows, d_model = x2d.shape
    d_att = att.shape[1]
    n_seq, hist, d_conv = state.shape
    block_rows = seqs_per_block * n_new
    row = lambda c: pl.BlockSpec((block_rows, c), lambda i: (i, 0))
    st_spec = pl.BlockSpec((seqs_per_block, hist, d_conv), lambda i: (i, 0, 0))
    ext_rows = -(-(hist + n_new) // SUBLANES) * SUBLANES
    kernel = functools.partial(_mix_sample_kernel, n_new=n_new)
    w_out = weights[6]
    return pl.pallas_call(
        kernel,
        grid=(n_seq // seqs_per_block,),
        in_specs=[row(d_model), row(d_att), row(d_att), row(d_conv), st_spec, row(d_conv)]
        + _weight_specs(d_conv, d_model, d_att),
        out_specs=[row(d_model), st_spec],
        out_shape=[jax.ShapeDtypeStruct((rows, d_model), jnp.float32),
                   jax.ShapeDtypeStruct(state.shape, jnp.float32)],
        scratch_shapes=[pltpu.VMEM((seqs_per_block, ext_rows, d_conv), jnp.float32)],
        compiler_params=pltpu.CompilerParams(
            dimension_semantics=("arbitrary",), vmem_limit_bytes=VMEM_LIMIT_BYTES),
        name="mix_sample",
    )(x2d, att, sga, u, state, sgc, *weights[:6], w_out, w_out, weights[7])


def kernel(x_prompt, x_sample, cache_k, cache_v, state_conv, page_table, norm_g, w_in, lambda_q1,
           lambda_k1, lambda_q2, lambda_k2, subln_g, dw_w, dw_b, conv_ln_g, conv_ln_b, w_pw2,
           b_pw2, w_out, final_norm_g):
    batch, seq, d_model = x_prompt.shape
    n_seq, n_new, _ = x_sample.shape
    depth, _, _, n_heads, v_dim = cache_k.shape
    assert depth == 1 and n_heads == N_HEADS and v_dim == V_DIM
    d_conv = dw_w.shape[2]
    hist = CONV_W - 1
    bf16 = jnp.bfloat16

    norm_g2 = norm_g[0].reshape(1, d_model)
    w_in_b = w_in[0].astype(bf16)
    lams = [t[0].reshape(1, HEAD_DIM) for t in (lambda_q1, lambda_k1, lambda_q2, lambda_k2)]
    subln = subln_g[0].reshape(1, V_DIM)
    mix_w = [dw_w[0], dw_b[0].reshape(1, d_conv), conv_ln_g[0].reshape(1, d_conv),
             conv_ln_b[0].reshape(1, d_conv), w_pw2[0].astype(bf16), b_pw2[0].reshape(1, d_conv),
             w_out[0].astype(bf16), final_norm_g.reshape(1, d_model)]

    xp = x_prompt.reshape(batch * seq, d_model)
    q, k, v, sga, u, sgc, kb, vt = _in_proj(xp, norm_g2, w_in_b, block_rows=512, for_prompt=True)
    att = _prompt_attn(q, kb, vt, lams, subln, batch=batch, seq=seq, block=512)
    yp = _mix_prompt(xp, att, sga, u, sgc, mix_w, seq=seq, block_rows=256)
    kv_shape = (1, batch, seq, N_HEADS, V_DIM)
    conv_p = u.reshape(batch, seq, d_conv)[:, seq - hist:, :][None]

    xs = x_sample.reshape(n_seq * n_new, d_model)
    qs, ks, vs, sgas, us, sgcs = _in_proj(xs, norm_g2, w_in_b, block_rows=512, for_prompt=False)
    att_s = _sample_attn(page_table, qs, ks, vs, lams, subln, cache_k, cache_v, n_new=n_new)
    ys, conv_s = _mix_sample(xs, att_s, sgas, us, state_conv[0], sgcs, mix_w, n_new=n_new,
                             seqs_per_block=32)
    kvs_shape = (1, n_seq, n_new, N_HEADS, V_DIM)

    return (yp.reshape(batch, seq, d_model), ys.reshape(n_seq, n_new, d_model),
            k.reshape(kv_shape), v.reshape(kv_shape), conv_p,
            ks.reshape(kvs_shape), vs.reshape(kvs_shape), conv_s[None])
```

```python
import functools
import math

import jax
import jax.numpy as jnp
from jax import lax
from jax.experimental import pallas as pl
from jax.experimental.pallas import tpu as pltpu

N_HEADS = 4
HEAD_DIM = 64
V_DIM = 2 * HEAD_DIM
CONV_W = 31
EPS = 1e-5
LAM_INIT = 0.8 - 0.6 * math.exp(-0.3 * 0)

Q_SCALE = HEAD_DIM ** -0.5 * math.log2(math.e)

SUBLANES = 8
VMEM_LIMIT_BYTES = 56 * 1024 * 1024

KEY_SUB = 256
CONV_HALO = 32
NEG_INF = float("-inf")


def _silu(x):
    return x * jax.nn.sigmoid(x)


def _dot(a, b):
    return jnp.dot(a, b, preferred_element_type=jnp.float32)


def _dot_nt(a, b):
    return lax.dot_general(a, b, (((1,), (1,)), ((), ())), preferred_element_type=jnp.float32)


def _in_proj_kernel(x_ref, g_ref, w_ref, q_ref, k_ref, v_ref, sga_ref, u_ref, sgc_ref, *attn_refs):
    x = x_ref[...]
    h = x * lax.rsqrt(jnp.mean(x * x, axis=-1, keepdims=True) + EPS) * g_ref[...]
    hb = h.astype(jnp.bfloat16)
    c = q_ref.shape[-1]

    def col(i):
        return _dot(hb, w_ref[:, i * c:(i + 1) * c])

    q_ref[...] = (col(0) * Q_SCALE).astype(q_ref.dtype)
    k = col(1)
    v = col(2)
    for h_idx in range(N_HEADS):
        hs = slice(h_idx * V_DIM, (h_idx + 1) * V_DIM)
        k_ref[:, h_idx, :] = k[:, hs]
        v_ref[:, h_idx, :] = v[:, hs]
    if attn_refs:
        kb_ref, vt_ref = attn_refs
        kb_ref[...] = k.astype(jnp.bfloat16)
        vt_ref[...] = v.T.astype(jnp.bfloat16)
    sga_ref[...] = _silu(col(3))
    u_ref[...] = col(4) * jax.nn.sigmoid(col(5))
    sgc_ref[...] = _silu(col(6))


def _in_proj(x2d, norm_g, w_in_bf16, *, block_rows, for_prompt):
    rows, d_model = x2d.shape
    c = w_in_bf16.shape[1] // 7
    row_spec = pl.BlockSpec((block_rows, c), lambda i: (i, 0))
    kv_spec = pl.BlockSpec((block_rows, N_HEADS, V_DIM), lambda i: (i, 0, 0))
    f32_rows = jax.ShapeDtypeStruct((rows, c), jnp.float32)
    kv_rows = jax.ShapeDtypeStruct((rows, N_HEADS, V_DIM), jnp.float32)
    q_dtype = jnp.bfloat16 if for_prompt else jnp.float32
    out_shape = [jax.ShapeDtypeStruct((rows, c), q_dtype), kv_rows, kv_rows, f32_rows, f32_rows,
                 f32_rows]
    out_specs = [row_spec, kv_spec, kv_spec, row_spec, row_spec, row_spec]
    if for_prompt:
        out_shape += [jax.ShapeDtypeStruct((rows, c), jnp.bfloat16),
                      jax.ShapeDtypeStruct((c, rows), jnp.bfloat16)]
        out_specs += [row_spec, pl.BlockSpec((c, block_rows), lambda i: (0, i))]
    return pl.pallas_call(
        _in_proj_kernel,
        grid=(rows // block_rows,),
        in_specs=[
            pl.BlockSpec((block_rows, d_model), lambda i: (i, 0)),
            pl.BlockSpec((1, d_model), lambda i: (0, 0)),
            pl.BlockSpec(w_in_bf16.shape, lambda i: (0, 0)),
        ],
        out_specs=out_specs,
        out_shape=out_shape,
        compiler_params=pltpu.CompilerParams(
            dimension_semantics=("arbitrary",), vmem_limit_bytes=VMEM_LIMIT_BYTES),
        name="in_proj",
    )(x2d, norm_g, w_in_bf16)


def _lam(lq1_ref, lk1_ref, lq2_ref, lk2_ref):
    s1 = jnp.sum(lq1_ref[...] * lk1_ref[...], axis=-1, keepdims=True)
    s2 = jnp.sum(lq2_ref[...] * lk2_ref[...], axis=-1, keepdims=True)
    return jnp.exp(s1) - jnp.exp(s2) + LAM_INIT


def _split_maps(q):
    lane = lax.broadcasted_iota(jnp.int32, q.shape, 1)
    zero = jnp.zeros_like(q)
    return jnp.where(lane < HEAD_DIM, q, zero), jnp.where(lane >= HEAD_DIM, q, zero)


def _prompt_attn_kernel(q_ref, k_ref, vt_ref, lq1_ref, lk1_ref, lq2_ref, lk2_ref, g_ref, o_ref,
                        m_ref, l_ref, acc_ref, s_ref, *, block):
    qi = pl.program_id(2)
    qs = _split_maps(q_ref[...])

    m_ref[...] = jnp.full_like(m_ref, NEG_INF)
    l_ref[...] = jnp.zeros_like(l_ref)
    acc_ref[...] = jnp.zeros_like(acc_ref)

    n_sub = block // KEY_SUB

    def fold(x, op):
        return op(x.reshape(KEY_SUB // SUBLANES, SUBLANES, block), axis=0)

    def sub(c):
        return slice(c * KEY_SUB, (c + 1) * KEY_SUB)

    def scores(kc, mp):
        start = pl.multiple_of(kc * block, block)
        for c in range(n_sub):
            rows = pl.ds(start + c * KEY_SUB, KEY_SUB)
            s_ref[mp, sub(c), :] = _dot_nt(k_ref[rows, :], qs[mp])

    def mask_diagonal(kc, mp):
        for c in range(n_sub):
            s = s_ref[mp, sub(c), :]
            key = lax.broadcasted_iota(jnp.int32, s.shape, 0) + c * KEY_SUB
            qry = lax.broadcasted_iota(jnp.int32, s.shape, 1)
            s_ref[mp, sub(c), :] = jnp.where(key <= qry, s, NEG_INF)

    def accumulate(kc, mp):
        start = pl.multiple_of(kc * block, block)
        mx = None
        for c in range(n_sub):
            part = fold(s_ref[mp, sub(c), :], jnp.max)
            mx = part if mx is None else jnp.maximum(mx, part)
        m_prev = m_ref[mp]
        m_new = jnp.maximum(m_prev, jnp.max(mx, axis=0, keepdims=True))
        alpha = jnp.exp2(m_prev - m_new)
        lsum = None
        pv = None
        for c in range(n_sub):
            p = jnp.exp2(s_ref[mp, sub(c), :] - m_new)
            part = fold(p, jnp.sum)
            lsum = part if lsum is None else lsum + part
            cols = pl.ds(start + c * KEY_SUB, KEY_SUB)
            d = _dot(vt_ref[:, cols], p.astype(jnp.bfloat16))
            pv = d if pv is None else pv + d
        l_ref[mp] = alpha * l_ref[mp] + jnp.sum(lsum, axis=0, keepdims=True)
        acc_ref[mp] = alpha * acc_ref[mp] + pv
        m_ref[mp] = m_new

    scores(0, 0)

    def pipelined(kc, carry):
        scores(kc, 1)
        accumulate(kc, 0)
        scores(kc + 1, 0)
        accumulate(kc, 1)
        return carry

    lax.fori_loop(0, qi, pipelined, 0)
    scores(qi, 1)
    for mp in range(2):
        mask_diagonal(qi, mp)
        accumulate(qi, mp)

    lam = _lam(lq1_ref, lk1_ref, lq2_ref, lk2_ref)
    o = acc_ref[0] / l_ref[0] - lam * (acc_ref[1] / l_ref[1])
    y = o * lax.rsqrt(jnp.mean(o * o, axis=0, keepdims=True) + EPS)
    o_ref[...] = (y * g_ref[...] * (1.0 - LAM_INIT)).T


def _prompt_attn(q, kb, vt, lams, subln_g, *, batch, seq, block):
    rows, d_att = q.shape
    nq = seq // block
    vec = lambda n: pl.BlockSpec((1, n), lambda b, h, i: (0, 0))
    kernel = functools.partial(_prompt_attn_kernel, block=block)
    return pl.pallas_call(
        kernel,
        grid=(batch, N_HEADS, nq),
        in_specs=[
            pl.BlockSpec((block, V_DIM), lambda b, h, i: (b * nq + i, h)),
            pl.BlockSpec((seq, V_DIM), lambda b, h, i: (b, h)),
            pl.BlockSpec((V_DIM, seq), lambda b, h, i: (h, b)),
            vec(HEAD_DIM), vec(HEAD_DIM), vec(HEAD_DIM), vec(HEAD_DIM),
            pl.BlockSpec((V_DIM, 1), lambda b, h, i: (0, 0)),
        ],
        out_specs=pl.BlockSpec((block, V_DIM), lambda b, h, i: (b * nq + i, h)),
        out_shape=jax.ShapeDtypeStruct((rows, d_att), jnp.float32),
        scratch_shapes=[
            pltpu.VMEM((2, 1, block), jnp.float32),
            pltpu.VMEM((2, 1, block), jnp.float32),
            pltpu.VMEM((2, V_DIM, block), jnp.float32),
            pltpu.VMEM((2, block, block), jnp.float32),
        ],
        compiler_params=pltpu.CompilerParams(
            dimension_semantics=("arbitrary", "arbitrary", "arbitrary"),
            vmem_limit_bytes=VMEM_LIMIT_BYTES),
        name="prompt_attn",
    )(q, kb, vt, *lams, subln_g.reshape(V_DIM, 1))


def _sample_attn_kernel(pt_ref, q_ref, kn_ref, vn_ref, lq1_ref, lk1_ref, lq2_ref, lk2_ref, g_ref,
                        ck_hbm, cv_hbm, o_ref, kbuf, vbuf, sem, *, n_pages, page):
    s_idx = pl.program_id(0)
    n_seq = pl.num_programs(0)
    slot = s_idx % 2

    def page_copies(seq, slot_):
        copies = []
        for p in range(n_pages):
            pid = pt_ref[seq * n_pages + p]
            dst = pl.ds(p * page, page)
            for h in range(N_HEADS):
                copies.append(pltpu.make_async_copy(
                    ck_hbm.at[0, pid, :, h, :], kbuf.at[slot_, h, dst], sem.at[0, slot_]))
                copies.append(pltpu.make_async_copy(
                    cv_hbm.at[0, pid, :, h, :], vbuf.at[slot_, h, dst], sem.at[1, slot_]))
        return copies

    @pl.when(s_idx == 0)
    def _():
        for cp in page_copies(s_idx, slot):
            cp.start()

    @pl.when(s_idx + 1 < n_seq)
    def _():
        for cp in page_copies(s_idx + 1, 1 - slot):
            cp.start()

    for cp in page_copies(s_idx, slot):
        cp.wait()

    lam = _lam(lq1_ref, lk1_ref, lq2_ref, lk2_ref)
    t = q_ref.shape[0]
    row = lax.broadcasted_iota(jnp.int32, (t, t), 0)
    colk = lax.broadcasted_iota(jnp.int32, (t, t), 1)
    causal = colk <= row
    for h in range(N_HEADS):
        hs = slice(h * V_DIM, (h + 1) * V_DIM)
        qh = q_ref[:, hs].astype(jnp.bfloat16)
        k_past = kbuf[slot, h].astype(jnp.bfloat16)
        v_past = vbuf[slot, h].astype(jnp.bfloat16)
        k_new = kn_ref[:, h, :].astype(jnp.bfloat16)
        v_new = vn_ref[:, h, :].astype(jnp.bfloat16)
        probs = []
        for qm in _split_maps(qh):
            s_past = _dot_nt(qm, k_past)
            s_new = jnp.where(causal, _dot_nt(qm, k_new), NEG_INF)
            m = jnp.maximum(jnp.max(s_past, axis=-1, keepdims=True),
                            jnp.max(s_new, axis=-1, keepdims=True))
            e_past = jnp.exp2(s_past - m)
            e_new = jnp.exp2(s_new - m)
            denom = (jnp.sum(e_past, axis=-1, keepdims=True)
                     + jnp.sum(e_new, axis=-1, keepdims=True))
            probs.append((e_past / denom, e_new / denom))
        a_past = probs[0][0] - lam * probs[1][0]
        a_new = probs[0][1] - lam * probs[1][1]
        o = _dot(a_past.astype(jnp.bfloat16), v_past) + _dot(a_new.astype(jnp.bfloat16), v_new)
        y = o * lax.rsqrt(jnp.mean(o * o, axis=-1, keepdims=True) + EPS)
        o_ref[:, hs] = y * g_ref[...] * (1.0 - LAM_INIT)


def _sample_attn(page_table, q, k_new, v_new, lams, subln_g, cache_k, cache_v, *, n_new):
    n_seq, n_pages = page_table.shape
    _, _, page, n_heads, v_dim = cache_k.shape
    past = n_pages * page
    d_att = q.shape[1]
    row_spec = pl.BlockSpec((n_new, d_att), lambda s, pt: (s, 0))
    kv_spec = pl.BlockSpec((n_new, n_heads, v_dim), lambda s, pt: (s, 0, 0))
    vec = lambda n: pl.BlockSpec((1, n), lambda s, pt: (0, 0))
    kernel = functools.partial(_sample_attn_kernel, n_pages=n_pages, page=page)
    return pl.pallas_call(
        kernel,
        grid_spec=pltpu.PrefetchScalarGridSpec(
            num_scalar_prefetch=1,
            grid=(n_seq,),
            in_specs=[
                row_spec, kv_spec, kv_spec,
                vec(HEAD_DIM), vec(HEAD_DIM), vec(HEAD_DIM), vec(HEAD_DIM), vec(V_DIM),
                pl.BlockSpec(memory_space=pl.ANY),
                pl.BlockSpec(memory_space=pl.ANY),
            ],
            out_specs=row_spec,
            scratch_shapes=[
                pltpu.VMEM((2, n_heads, past, v_dim), jnp.float32),
                pltpu.VMEM((2, n_heads, past, v_dim), jnp.float32),
                pltpu.SemaphoreType.DMA((2, 2)),
            ],
        ),
        out_shape=jax.ShapeDtypeStruct(q.shape, jnp.float32),
        compiler_params=pltpu.CompilerParams(
            dimension_semantics=("arbitrary",), vmem_limit_bytes=VMEM_LIMIT_BYTES),
        name="sample_attn",
    )(page_table.reshape(-1), q, k_new, v_new, *lams, subln_g, cache_k, cache_v)


def _mix_tail(c, x, att, sga, sgc, ln_g_ref, ln_b_ref, wp_ref, bp_ref, wo_att_ref, wo_conv_ref,
              fg_ref):
    mu = jnp.mean(c, axis=-1, keepdims=True)
    d = c - mu
    var = jnp.mean(d * d, axis=-1, keepdims=True)
    ln = d * lax.rsqrt(var + EPS) * ln_g_ref[...] + ln_b_ref[...]
    conv = _dot(_silu(ln).astype(jnp.bfloat16), wp_ref[...]) + bp_ref[...]
    mixed = (_dot((att * sga).astype(jnp.bfloat16), wo_att_ref[...])
             + _dot((conv * sgc).astype(jnp.bfloat16), wo_conv_ref[...]))
    y = x + mixed
    return y * lax.rsqrt(jnp.mean(y * y, axis=-1, keepdims=True) + EPS) * fg_ref[...]


def _mix_prompt_kernel(x_ref, att_ref, sga_ref, u_ref, halo_ref, sgc_ref, dw_ref, db_ref,
                       ln_g_ref, ln_b_ref, wp_ref, bp_ref, wo_att_ref, wo_conv_ref, fg_ref,
                       y_ref, ext_ref, *, tiles_per_seq):
    rows = u_ref.shape[0]
    first = pl.program_id(0) % tiles_per_seq == 0

    @pl.when(first)
    def _():
        ext_ref[0:CONV_HALO, :] = jnp.zeros((CONV_HALO, ext_ref.shape[1]), jnp.float32)

    @pl.when(jnp.logical_not(first))
    def _():
        ext_ref[0:CONV_HALO, :] = halo_ref[...]

    ext_ref[CONV_HALO:, :] = u_ref[...]
    base = CONV_HALO - (CONV_W - 1)
    c = jnp.broadcast_to(db_ref[...], (rows, ext_ref.shape[1]))
    for w in range(CONV_W):
        c = c + ext_ref[base + w:base + w + rows, :] * dw_ref[w:w + 1, :]
    y_ref[...] = _mix_tail(c, x_ref[...], att_ref[...], sga_ref[...], sgc_ref[...], ln_g_ref,
                           ln_b_ref, wp_ref, bp_ref, wo_att_ref, wo_conv_ref, fg_ref)


def _weight_specs(d_conv, d_model, d_att):
    full = lambda r, c: pl.BlockSpec((r, c), lambda i: (0, 0))
    return [
        full(CONV_W, d_conv), full(1, d_conv), full(1, d_conv), full(1, d_conv),
        full(d_conv, d_conv), full(1, d_conv),
        pl.BlockSpec((d_att, d_model), lambda i: (0, 0)),
        pl.BlockSpec((d_conv, d_model), lambda i: (d_att // d_conv, 0)),
        full(1, d_model),
    ]


def _mix_prompt(x2d, att, sga, u, sgc, weights, *, seq, block_rows):
    rows, d_model = x2d.shape
    d_att = att.shape[1]
    d_conv = u.shape[1]
    tiles_per_seq = seq // block_rows
    halo_blocks = block_rows // CONV_HALO
    row = lambda c: pl.BlockSpec((block_rows, c), lambda i: (i, 0))
    kernel = functools.partial(_mix_prompt_kernel, tiles_per_seq=tiles_per_seq)
    w_out = weights[6]
    return pl.pallas_call(
        kernel,
        grid=(rows // block_rows,),
        in_specs=[
            row(d_model), row(d_att), row(d_att), row(d_conv),
            pl.BlockSpec((CONV_HALO, d_conv), lambda i: (jnp.maximum(i * halo_blocks - 1, 0), 0)),
            row(d_conv),
        ] + _weight_specs(d_conv, d_model, d_att),
        out_specs=row(d_model),
        out_shape=jax.ShapeDtypeStruct((rows, d_model), jnp.float32),
        scratch_shapes=[pltpu.VMEM((CONV_HALO + block_rows, d_conv), jnp.float32)],
        compiler_params=pltpu.CompilerParams(
            dimension_semantics=("arbitrary",), vmem_limit_bytes=VMEM_LIMIT_BYTES),
        name="mix_prompt",
    )(x2d, att, sga, u, u, sgc, *weights[:6], w_out, w_out, weights[7])


def _mix_sample_kernel(x_ref, att_ref, sga_ref, u_ref, st_ref, sgc_ref, dw_ref, db_ref,
                       ln_g_ref, ln_b_ref, wp_ref, bp_ref, wo_att_ref, wo_conv_ref, fg_ref,
                       y_ref, st_out_ref, ext_ref, *, n_new):
    g = st_ref.shape[0]
    d_conv = st_ref.shape[2]
    hist = CONV_W - 1
    ext_ref[:, 0:hist, :] = st_ref[...]
    ext_ref[:, hist:hist + n_new, :] = u_ref[...].reshape(g, n_new, d_conv)
    c = jnp.broadcast_to(db_ref[...].reshape(1, 1, d_conv), (g, n_new, d_conv))
    for w in range(CONV_W):
        c = c + ext_ref[:, w:w + n_new, :] * dw_ref[w:w + 1, :].reshape(1, 1, d_conv)
    st_out_ref[...] = ext_ref[:, n_new:n_new + hist, :]
    y_ref[...] = _mix_tail(c.reshape(g * n_new, d_conv), x_ref[...], att_ref[...], sga_ref[...],
                           sgc_ref[...], ln_g_ref, ln_b_ref, wp_ref, bp_ref, wo_att_ref,
                           wo_conv_ref, fg_ref)


def _mix_sample(x2d, att, sga, u, state, sgc, weights, *, n_new, seqs_per_block):
    rows, d_model = x2d.shape
    d_att = att.shape[1]
    n_seq, hist, d_conv = state.shape
    block_rows = seqs_per_block * n_new
    row = lambda c: pl.BlockSpec((block_rows, c), lambda i: (i, 0))
    st_spec = pl.BlockSpec((seqs_per_block, hist, d_conv), lambda i: (i, 0, 0))
    ext_rows = -(-(hist + n_new) // SUBLANES) * SUBLANES
    kernel = functools.partial(_mix_sample_kernel, n_new=n_new)
    w_out = weights[6]
    return pl.pallas_call(
        kernel,
        grid=(n_seq // seqs_per_block,),
        in_specs=[row(d_model), row(d_att), row(d_att), row(d_conv), st_spec, row(d_conv)]
        + _weight_specs(d_conv, d_model, d_att),
        out_specs=[row(d_model), st_spec],
        out_shape=[jax.ShapeDtypeStruct((rows, d_model), jnp.float32),
                   jax.ShapeDtypeStruct(state.shape, jnp.float32)],
        scratch_shapes=[pltpu.VMEM((seqs_per_block, ext_rows, d_conv), jnp.float32)],
        compiler_params=pltpu.CompilerParams(
            dimension_semantics=("arbitrary",), vmem_limit_bytes=VMEM_LIMIT_BYTES),
        name="mix_sample",
    )(x2d, att, sga, u, state, sgc, *weights[:6], w_out, w_out, weights[7])


def kernel(x_prompt, x_sample, cache_k, cache_v, state_conv, page_table, norm_g, w_in, lambda_q1,
           lambda_k1, lambda_q2, lambda_k2, subln_g, dw_w, dw_b, conv_ln_g, conv_ln_b, w_pw2,
           b_pw2, w_out, final_norm_g):
    batch, seq, d_model = x_prompt.shape
    n_seq, n_new, _ = x_sample.shape
    depth, _, _, n_heads, v_dim = cache_k.shape
    assert depth == 1 and n_heads == N_HEADS and v_dim == V_DIM
    d_conv = dw_w.shape[2]
    hist = CONV_W - 1
    bf16 = jnp.bfloat16

    norm_g2 = norm_g[0].reshape(1, d_model)
    w_in_b = w_in[0].astype(bf16)
    lams = [t[0].reshape(1, HEAD_DIM) for t in (lambda_q1, lambda_k1, lambda_q2, lambda_k2)]
    subln = subln_g[0].reshape(1, V_DIM)
    mix_w = [dw_w[0], dw_b[0].reshape(1, d_conv), conv_ln_g[0].reshape(1, d_conv),
             conv_ln_b[0].reshape(1, d_conv), w_pw2[0].astype(bf16), b_pw2[0].reshape(1, d_conv),
             w_out[0].astype(bf16), final_norm_g.reshape(1, d_model)]

    xp = x_prompt.reshape(batch * seq, d_model)
    q, k, v, sga, u, sgc, kb, vt = _in_proj(xp, norm_g2, w_in_b, block_rows=512, for_prompt=True)
    att = _prompt_attn(q, kb, vt, lams, subln, batch=batch, seq=seq, block=512)
    yp = _mix_prompt(xp, att, sga, u, sgc, mix_w, seq=seq, block_rows=256)
    kv_shape = (1, batch, seq, N_HEADS, V_DIM)
    conv_p = u.reshape(batch, seq, d_conv)[:, seq - hist:, :][None]

    xs = x_sample.reshape(n_seq * n_new, d_model)
    qs, ks, vs, sgas, us, sgcs = _in_proj(xs, norm_g2, w_in_b, block_rows=512, for_prompt=False)
    att_s = _sample_attn(page_table, qs, ks, vs, lams, subln, cache_k, cache_v, n_new=n_new)
    ys, conv_s = _mix_sample(xs, att_s, sgas, us, state_conv[0], sgcs, mix_w, n_new=n_new,
                             seqs_per_block=32)
    kvs_shape = (1, n_seq, n_new, N_HEADS, V_DIM)

    return (yp.reshape(batch, seq, d_model), ys.reshape(n_seq, n_new, d_model),
            k.reshape(kv_shape), v.reshape(kv_shape), conv_p,
            ks.reshape(kvs_shape), vs.reshape(kvs_shape), conv_s[None])
```

```python
import functools
import math

import jax
import jax.numpy as jnp
from jax import lax
from jax.experimental import pallas as pl
from jax.experimental.pallas import tpu as pltpu

N_HEADS = 4
HEAD_DIM = 64
V_DIM = 2 * HEAD_DIM
CONV_W = 31
EPS = 1e-5
LAM_INIT = 0.8 - 0.6 * math.exp(-0.3 * 0)

Q_SCALE = HEAD_DIM ** -0.5 * math.log2(math.e)

SUBLANES = 8
VMEM_LIMIT_BYTES = 56 * 1024 * 1024

KEY_SUB = 256
CONV_HALO = 32
CONV_CHUNK = 64
NEG_INF = float("-inf")


def _silu(x):
    return x * jax.nn.sigmoid(x)


def _dot(a, b):
    return jnp.dot(a, b, preferred_element_type=jnp.float32)


def _dot_nt(a, b):
    return lax.dot_general(a, b, (((1,), (1,)), ((), ())), preferred_element_type=jnp.float32)


def _normed_input(x_ref, g_ref):
    x = x_ref[...]
    h = x * lax.rsqrt(jnp.mean(x * x, axis=-1, keepdims=True) + EPS) * g_ref[...]
    return h.astype(jnp.bfloat16)


def _conv_post(c, ln_g_ref, ln_b_ref, wp_ref, bp_ref):
    mu = jnp.mean(c, axis=-1, keepdims=True)
    d = c - mu
    var = jnp.mean(d * d, axis=-1, keepdims=True)
    ln = d * lax.rsqrt(var + EPS) * ln_g_ref[...] + ln_b_ref[...]
    return _dot(_silu(ln).astype(jnp.bfloat16), wp_ref[...]) + bp_ref[...]


def _in_proj_sample_kernel(x_ref, g_ref, w_ref, q_ref, k_ref, v_ref, sga_ref, u_ref, sgc_ref):
    hb = _normed_input(x_ref, g_ref)
    c = q_ref.shape[-1]

    def col(i):
        return _dot(hb, w_ref[:, i * c:(i + 1) * c])

    q_ref[...] = col(0) * Q_SCALE
    k = col(1)
    v = col(2)
    for h in range(N_HEADS):
        hs = slice(h * V_DIM, (h + 1) * V_DIM)
        k_ref[:, h, :] = k[:, hs]
        v_ref[:, h, :] = v[:, hs]
    sga_ref[...] = _silu(col(3))
    u_ref[...] = col(4) * jax.nn.sigmoid(col(5))
    sgc_ref[...] = _silu(col(6))


def _in_proj_sample(x2d, norm_g, w_in_bf16, *, block_rows):
    rows, d_model = x2d.shape
    c = w_in_bf16.shape[1] // 7
    row_spec = pl.BlockSpec((block_rows, c), lambda i: (i, 0))
    kv_spec = pl.BlockSpec((block_rows, N_HEADS, V_DIM), lambda i: (i, 0, 0))
    f32_rows = jax.ShapeDtypeStruct((rows, c), jnp.float32)
    kv_rows = jax.ShapeDtypeStruct((rows, N_HEADS, V_DIM), jnp.float32)
    return pl.pallas_call(
        _in_proj_sample_kernel,
        grid=(rows // block_rows,),
        in_specs=[
            pl.BlockSpec((block_rows, d_model), lambda i: (i, 0)),
            pl.BlockSpec((1, d_model), lambda i: (0, 0)),
            pl.BlockSpec(w_in_bf16.shape, lambda i: (0, 0)),
        ],
        out_specs=[row_spec, kv_spec, kv_spec, row_spec, row_spec, row_spec],
        out_shape=[f32_rows, kv_rows, kv_rows, f32_rows, f32_rows, f32_rows],
        compiler_params=pltpu.CompilerParams(
            dimension_semantics=("arbitrary",), vmem_limit_bytes=VMEM_LIMIT_BYTES),
        name="in_proj_sample",
    )(x2d, norm_g, w_in_bf16)


def _in_proj_prompt_kernel(x_ref, g_ref, w_ref, dw_ref, db_ref, ln_g_ref, ln_b_ref, wp_ref, bp_ref,
                           q_ref, kb_ref, vt_ref, sga_ref, cg_ref, cstate_ref, k_hbm, v_hbm,
                           hb_ref, ext_ref, shift_ref, sgc_ref, act_ref, kst_ref, vst_ref, sem, *,
                           tiles_per_seq):
    i = pl.program_id(0)
    rows = x_ref.shape[0]
    hb_ref[...] = _normed_input(x_ref, g_ref)
    c = q_ref.shape[-1]

    def col(j):
        return _dot(hb_ref[...], w_ref[:, j * c:(j + 1) * c])

    def kv_copies(step):
        dst_rows = pl.ds(step * rows, rows)
        copies = []
        for h in range(N_HEADS):
            copies.append(pltpu.make_async_copy(kst_ref.at[h], k_hbm.at[dst_rows, h, :], sem.at[0]))
            copies.append(pltpu.make_async_copy(vst_ref.at[h], v_hbm.at[dst_rows, h, :], sem.at[1]))
        return copies

    first = i % tiles_per_seq == 0

    @pl.when(first)
    def _():
        ext_ref[0:CONV_HALO, :] = jnp.zeros((CONV_HALO, c), jnp.float32)

    @pl.when(jnp.logical_not(first))
    def _():
        ext_ref[0:CONV_HALO, :] = ext_ref[rows:rows + CONV_HALO, :]

    @pl.when(i > 0)
    def _():
        for cp in kv_copies(i - 1):
            cp.wait()

    sgc_ref[...] = _silu(col(6))
    ext_ref[CONV_HALO:, :] = col(4) * jax.nn.sigmoid(col(5))
    hist = CONV_W - 1
    cstate_ref[0] = ext_ref[CONV_HALO + rows - hist:CONV_HALO + rows, :]
    for o in range(1, SUBLANES):
        shift_ref[o - 1] = ext_ref[o:o + shift_ref.shape[1], :]

    def conv_chunk(ch):
        r0 = ch * CONV_CHUNK
        acc = jnp.broadcast_to(db_ref[...], (CONV_CHUNK, c))
        for w in range(CONV_W):
            start = CONV_HALO - hist + w
            o = start % SUBLANES
            a = r0 + start - o
            src = ext_ref if o == 0 else shift_ref.at[o - 1]
            acc = acc + src[a:a + CONV_CHUNK, :] * dw_ref[w:w + 1, :]
        mu = jnp.mean(acc, axis=-1, keepdims=True)
        d = acc - mu
        var = jnp.mean(d * d, axis=-1, keepdims=True)
        ln = d * lax.rsqrt(var + EPS) * ln_g_ref[...] + ln_b_ref[...]
        act_ref[r0:r0 + CONV_CHUNK, :] = _silu(ln).astype(jnp.bfloat16)

    def half_col(j, half):
        lo = half * (c // 2)
        return _dot(hb_ref[...], w_ref[:, j * c + lo:j * c + lo + c // 2]), slice(lo, lo + c // 2)

    def q_piece(half):
        z, cols = half_col(0, half)
        q_ref[:, cols] = (z * Q_SCALE).astype(jnp.bfloat16)

    def kv_piece(j, flat_ref, stage_ref, half):
        z, cols = half_col(j, half)
        heads_per_half = N_HEADS // 2
        for hh in range(heads_per_half):
            stage_ref[half * heads_per_half + hh] = z[:, hh * V_DIM:(hh + 1) * V_DIM]
        return z, cols

    def k_piece(half):
        z, cols = kv_piece(1, kb_ref, kst_ref, half)
        kb_ref[:, cols] = z.astype(jnp.bfloat16)

    def v_piece(half):
        z, cols = kv_piece(2, vt_ref, vst_ref, half)
        vt_ref[cols, :] = z.T.astype(jnp.bfloat16)

    def gate_piece(half):
        z, cols = half_col(3, half)
        sga_ref[:, cols] = _silu(z).astype(jnp.bfloat16)

    for ch in range(rows // CONV_CHUNK):
        conv_chunk(ch)
    for piece in (q_piece, k_piece, v_piece, gate_piece):
        for half in range(2):
            piece(half)
    for cp in kv_copies(i):
        cp.start()

    conv = _dot(act_ref[...], wp_ref[...]) + bp_ref[...]
    cg_ref[...] = (conv * sgc_ref[...]).astype(jnp.bfloat16)

    @pl.when(i == pl.num_programs(0) - 1)
    def _():
        for cp in kv_copies(i):
            cp.wait()


def _in_proj_prompt(x2d, norm_g, w_in_bf16, conv_w, *, batch, seq, block_rows):
    rows, d_model = x2d.shape
    c = w_in_bf16.shape[1] // 7
    tiles_per_seq = seq // block_rows
    hist = CONV_W - 1
    full = lambda a: pl.BlockSpec(a.shape, lambda i: (0,) * a.ndim)
    row_spec = pl.BlockSpec((block_rows, c), lambda i: (i, 0))
    bf16_rows = jax.ShapeDtypeStruct((rows, c), jnp.bfloat16)
    kv_rows = jax.ShapeDtypeStruct((rows, N_HEADS, V_DIM), jnp.float32)
    any_spec = pl.BlockSpec(memory_space=pl.ANY)
    kernel = functools.partial(_in_proj_prompt_kernel, tiles_per_seq=tiles_per_seq)
    return pl.pallas_call(
        kernel,
        grid=(rows // block_rows,),
        in_specs=[pl.BlockSpec((block_rows, d_model), lambda i: (i, 0)), full(norm_g),
                  full(w_in_bf16)] + [full(a) for a in conv_w],
        out_specs=[row_spec, row_spec, pl.BlockSpec((c, block_rows), lambda i: (0, i)), row_spec,
                   row_spec,
                   pl.BlockSpec((1, hist, c), lambda i: (i // tiles_per_seq, 0, 0)),
                   any_spec, any_spec],
        out_shape=[bf16_rows, bf16_rows, jax.ShapeDtypeStruct((c, rows), jnp.bfloat16), bf16_rows,
                   bf16_rows, jax.ShapeDtypeStruct((batch, hist, c), jnp.float32),
                   kv_rows, kv_rows],
        scratch_shapes=[
            pltpu.VMEM((block_rows, d_model), jnp.bfloat16),
            pltpu.VMEM((CONV_HALO + block_rows, c), jnp.float32),
            pltpu.VMEM((SUBLANES - 1, CONV_HALO + block_rows - SUBLANES, c), jnp.float32),
            pltpu.VMEM((block_rows, c), jnp.float32),
            pltpu.VMEM((block_rows, c), jnp.bfloat16),
            pltpu.VMEM((N_HEADS, block_rows, V_DIM), jnp.float32),
            pltpu.VMEM((N_HEADS, block_rows, V_DIM), jnp.float32),
            pltpu.SemaphoreType.DMA((2,)),
        ],
        compiler_params=pltpu.CompilerParams(
            dimension_semantics=("arbitrary",), vmem_limit_bytes=VMEM_LIMIT_BYTES),
        name="in_proj_prompt",
    )(x2d, norm_g, w_in_bf16, *conv_w)


def _lam(lq1_ref, lk1_ref, lq2_ref, lk2_ref):
    s1 = jnp.sum(lq1_ref[...] * lk1_ref[...], axis=-1, keepdims=True)
    s2 = jnp.sum(lq2_ref[...] * lk2_ref[...], axis=-1, keepdims=True)
    return jnp.exp(s1) - jnp.exp(s2) + LAM_INIT


def _split_maps(q):
    lane = lax.broadcasted_iota(jnp.int32, q.shape, 1)
    zero = jnp.zeros_like(q)
    return jnp.where(lane < HEAD_DIM, q, zero), jnp.where(lane >= HEAD_DIM, q, zero)


def _prompt_attn_kernel(q_ref, k_ref, vt_ref, sga_ref, lq1_ref, lk1_ref, lq2_ref, lk2_ref, g_ref,
                        o_ref, m_ref, l_ref, acc_ref, s_ref, *, block):
    qi = pl.program_id(2)
    qs = _split_maps(q_ref[...])

    m_ref[...] = jnp.full_like(m_ref, NEG_INF)
    l_ref[...] = jnp.zeros_like(l_ref)
    acc_ref[...] = jnp.zeros_like(acc_ref)

    n_sub = block // KEY_SUB

    def fold(x, op):
        return op(x.reshape(KEY_SUB // SUBLANES, SUBLANES, block), axis=0)

    def sub(c):
        return slice(c * KEY_SUB, (c + 1) * KEY_SUB)

    def scores(kc, mp):
        start = pl.multiple_of(kc * block, block)
        for c in range(n_sub):
            rows = pl.ds(start + c * KEY_SUB, KEY_SUB)
            s_ref[mp, sub(c), :] = _dot_nt(k_ref[rows, :], qs[mp])

    def mask_diagonal(mp):
        for c in range(n_sub):
            s = s_ref[mp, sub(c), :]
            key = lax.broadcasted_iota(jnp.int32, s.shape, 0) + c * KEY_SUB
            qry = lax.broadcasted_iota(jnp.int32, s.shape, 1)
            s_ref[mp, sub(c), :] = jnp.where(key <= qry, s, NEG_INF)

    def accumulate(kc, mp):
        start = pl.multiple_of(kc * block, block)
        mx = None
        for c in range(n_sub):
            part = fold(s_ref[mp, sub(c), :], jnp.max)
            mx = part if mx is None else jnp.maximum(mx, part)
        m_prev = m_ref[mp]
        m_new = jnp.maximum(m_prev, jnp.max(mx, axis=0, keepdims=True))
        alpha = jnp.exp2(m_prev - m_new)
        lsum = None
        pv = None
        for c in range(n_sub):
            p = jnp.exp2(s_ref[mp, sub(c), :] - m_new)
            part = fold(p, jnp.sum)
            lsum = part if lsum is None else lsum + part
            cols = pl.ds(start + c * KEY_SUB, KEY_SUB)
            d = _dot(vt_ref[:, cols], p.astype(jnp.bfloat16))
            pv = d if pv is None else pv + d
        l_ref[mp] = alpha * l_ref[mp] + jnp.sum(lsum, axis=0, keepdims=True)
        acc_ref[mp] = alpha * acc_ref[mp] + pv
        m_ref[mp] = m_new

    scores(0, 0)

    def pipelined(kc, carry):
        scores(kc, 1)
        accumulate(kc, 0)
        scores(kc + 1, 0)
        accumulate(kc, 1)
        return carry

    lax.fori_loop(0, qi, pipelined, 0)
    scores(qi, 1)
    for mp in range(2):
        mask_diagonal(mp)
        accumulate(qi, mp)

    lam = _lam(lq1_ref, lk1_ref, lq2_ref, lk2_ref)
    o = acc_ref[0] / l_ref[0] - lam * (acc_ref[1] / l_ref[1])
    y = o * lax.rsqrt(jnp.mean(o * o, axis=0, keepdims=True) + EPS)
    att = (y * g_ref[...] * (1.0 - LAM_INIT)).T
    o_ref[...] = (att * sga_ref[...].astype(jnp.float32)).astype(o_ref.dtype)


def _prompt_attn(q, kb, vt, sga, lams, subln_g, *, batch, seq, block):
    rows, d_att = q.shape
    nq = seq // block
    vec = lambda n: pl.BlockSpec((1, n), lambda b, h, i: (0, 0))
    q_spec = pl.BlockSpec((block, V_DIM), lambda b, h, i: (b * nq + i, h))
    kernel = functools.partial(_prompt_attn_kernel, block=block)
    return pl.pallas_call(
        kernel,
        grid=(batch, N_HEADS, nq),
        in_specs=[
            q_spec,
            pl.BlockSpec((seq, V_DIM), lambda b, h, i: (b, h)),
            pl.BlockSpec((V_DIM, seq), lambda b, h, i: (h, b)),
            q_spec,
            vec(HEAD_DIM), vec(HEAD_DIM), vec(HEAD_DIM), vec(HEAD_DIM),
            pl.BlockSpec((V_DIM, 1), lambda b, h, i: (0, 0)),
        ],
        out_specs=q_spec,
        out_shape=jax.ShapeDtypeStruct((rows, d_att), jnp.bfloat16),
        scratch_shapes=[
            pltpu.VMEM((2, 1, block), jnp.float32),
            pltpu.VMEM((2, 1, block), jnp.float32),
            pltpu.VMEM((2, V_DIM, block), jnp.float32),
            pltpu.VMEM((2, block, block), jnp.float32),
        ],
        compiler_params=pltpu.CompilerParams(
            dimension_semantics=("arbitrary", "arbitrary", "arbitrary"),
            vmem_limit_bytes=VMEM_LIMIT_BYTES),
        name="prompt_attn",
    )(q, kb, vt, sga, *lams, subln_g.reshape(V_DIM, 1))


def _sample_attn_kernel(pt_ref, q_ref, kn_ref, vn_ref, lq1_ref, lk1_ref, lq2_ref, lk2_ref, g_ref,
                        ck_hbm, cv_hbm, o_ref, kbuf, vbuf, sem, *, n_pages, page):
    s_idx = pl.program_id(0)
    n_seq = pl.num_programs(0)
    slot = s_idx % 2

    def page_copies(seq, slot_):
        copies = []
        for p in range(n_pages):
            pid = pt_ref[seq * n_pages + p]
            dst = pl.ds(p * page, page)
            for h in range(N_HEADS):
                copies.append(pltpu.make_async_copy(
                    ck_hbm.at[0, pid, :, h, :], kbuf.at[slot_, h, dst], sem.at[0, slot_]))
                copies.append(pltpu.make_async_copy(
                    cv_hbm.at[0, pid, :, h, :], vbuf.at[slot_, h, dst], sem.at[1, slot_]))
        return copies

    @pl.when(s_idx == 0)
    def _():
        for cp in page_copies(s_idx, slot):
            cp.start()

    @pl.when(s_idx + 1 < n_seq)
    def _():
        for cp in page_copies(s_idx + 1, 1 - slot):
            cp.start()

    for cp in page_copies(s_idx, slot):
        cp.wait()

    lam = _lam(lq1_ref, lk1_ref, lq2_ref, lk2_ref)
    t = q_ref.shape[0]
    row = lax.broadcasted_iota(jnp.int32, (t, t), 0)
    colk = lax.broadcasted_iota(jnp.int32, (t, t), 1)
    causal = colk <= row
    for h in range(N_HEADS):
        hs = slice(h * V_DIM, (h + 1) * V_DIM)
        qh = q_ref[:, hs].astype(jnp.bfloat16)
        k_past = kbuf[slot, h].astype(jnp.bfloat16)
        v_past = vbuf[slot, h].astype(jnp.bfloat16)
        k_new = kn_ref[:, h, :].astype(jnp.bfloat16)
        v_new = vn_ref[:, h, :].astype(jnp.bfloat16)
        probs = []
        for qm in _split_maps(qh):
            s_past = _dot_nt(qm, k_past)
            s_new = jnp.where(causal, _dot_nt(qm, k_new), NEG_INF)
            m = jnp.maximum(jnp.max(s_past, axis=-1, keepdims=True),
                            jnp.max(s_new, axis=-1, keepdims=True))
            e_past = jnp.exp2(s_past - m)
            e_new = jnp.exp2(s_new - m)
            denom = (jnp.sum(e_past, axis=-1, keepdims=True)
                     + jnp.sum(e_new, axis=-1, keepdims=True))
            probs.append((e_past / denom, e_new / denom))
        a_past = probs[0][0] - lam * probs[1][0]
        a_new = probs[0][1] - lam * probs[1][1]
        o = _dot(a_past.astype(jnp.bfloat16), v_past) + _dot(a_new.astype(jnp.bfloat16), v_new)
        y = o * lax.rsqrt(jnp.mean(o * o, axis=-1, keepdims=True) + EPS)
        o_ref[:, hs] = y * g_ref[...] * (1.0 - LAM_INIT)


def _sample_attn(page_table, q, k_new, v_new, lams, subln_g, cache_k, cache_v, *, n_new):
    n_seq, n_pages = page_table.shape
    _, _, page, n_heads, v_dim = cache_k.shape
    past = n_pages * page
    d_att = q.shape[1]
    row_spec = pl.BlockSpec((n_new, d_att), lambda s, pt: (s, 0))
    kv_spec = pl.BlockSpec((n_new, n_heads, v_dim), lambda s, pt: (s, 0, 0))
    vec = lambda n: pl.BlockSpec((1, n), lambda s, pt: (0, 0))
    kernel = functools.partial(_sample_attn_kernel, n_pages=n_pages, page=page)
    return pl.pallas_call(
        kernel,
        grid_spec=pltpu.PrefetchScalarGridSpec(
            num_scalar_prefetch=1,
            grid=(n_seq,),
            in_specs=[
                row_spec, kv_spec, kv_spec,
                vec(HEAD_DIM), vec(HEAD_DIM), vec(HEAD_DIM), vec(HEAD_DIM), vec(V_DIM),
                pl.BlockSpec(memory_space=pl.ANY),
                pl.BlockSpec(memory_space=pl.ANY),
            ],
            out_specs=row_spec,
            scratch_shapes=[
                pltpu.VMEM((2, n_heads, past, v_dim), jnp.float32),
                pltpu.VMEM((2, n_heads, past, v_dim), jnp.float32),
                pltpu.SemaphoreType.DMA((2, 2)),
            ],
        ),
        out_shape=jax.ShapeDtypeStruct(q.shape, jnp.float32),
        compiler_params=pltpu.CompilerParams(
            dimension_semantics=("arbitrary",), vmem_limit_bytes=VMEM_LIMIT_BYTES),
        name="sample_attn",
    )(page_table.reshape(-1), q, k_new, v_new, *lams, subln_g, cache_k, cache_v)


def _out_tail(x, attg, convg, wo_att_ref, wo_conv_ref, fg_ref):
    y = x + _dot(attg, wo_att_ref[...]) + _dot(convg, wo_conv_ref[...])
    return y * lax.rsqrt(jnp.mean(y * y, axis=-1, keepdims=True) + EPS) * fg_ref[...]


def _out_prompt_kernel(x_ref, attg_ref, cg_ref, wo_att_ref, wo_conv_ref, fg_ref, y_ref):
    y_ref[...] = _out_tail(x_ref[...], attg_ref[...], cg_ref[...], wo_att_ref, wo_conv_ref, fg_ref)


def _out_prompt(x2d, attg, convg, w_out_bf16, final_g, *, block_rows):
    rows, d_model = x2d.shape
    d_att = attg.shape[1]
    d_conv = convg.shape[1]
    row = lambda c: pl.BlockSpec((block_rows, c), lambda i: (i, 0))
    return pl.pallas_call(
        _out_prompt_kernel,
        grid=(rows // block_rows,),
        in_specs=[row(d_model), row(d_att), row(d_conv),
                  pl.BlockSpec((d_att, d_model), lambda i: (0, 0)),
                  pl.BlockSpec((d_conv, d_model), lambda i: (d_att // d_conv, 0)),
                  pl.BlockSpec((1, d_model), lambda i: (0, 0))],
        out_specs=row(d_model),
        out_shape=jax.ShapeDtypeStruct((rows, d_model), jnp.float32),
        compiler_params=pltpu.CompilerParams(
            dimension_semantics=("arbitrary",), vmem_limit_bytes=VMEM_LIMIT_BYTES),
        name="out_prompt",
    )(x2d, attg, convg, w_out_bf16, w_out_bf16, final_g)


def _weight_specs(d_conv, d_model, d_att):
    full = lambda r, c: pl.BlockSpec((r, c), lambda i: (0, 0))
    return [
        full(CONV_W, d_conv), full(1, d_conv), full(1, d_conv), full(1, d_conv),
        full(d_conv, d_conv), full(1, d_conv),
        pl.BlockSpec((d_att, d_model), lambda i: (0, 0)),
        pl.BlockSpec((d_conv, d_model), lambda i: (d_att // d_conv, 0)),
        full(1, d_model),
    ]


def _mix_sample_kernel(x_ref, att_ref, sga_ref, u_ref, st_ref, sgc_ref, dw_ref, db_ref,
                       ln_g_ref, ln_b_ref, wp_ref, bp_ref, wo_att_ref, wo_conv_ref, fg_ref,
                       y_ref, st_out_ref, ext_ref, *, n_new):
    g = st_ref.shape[0]
    d_conv = st_ref.shape[2]
    hist = CONV_W - 1
    ext_ref[:, 0:hist, :] = st_ref[...]
    ext_ref[:, hist:hist + n_new, :] = u_ref[...].reshape(g, n_new, d_conv)
    c = jnp.broadcast_to(db_ref[...].reshape(1, 1, d_conv), (g, n_new, d_conv))
    for w in range(CONV_W):
        c = c + ext_ref[:, w:w + n_new, :] * dw_ref[w:w + 1, :].reshape(1, 1, d_conv)
    st_out_ref[...] = ext_ref[:, n_new:n_new + hist, :]
    conv = _conv_post(c.reshape(g * n_new, d_conv), ln_g_ref, ln_b_ref, wp_ref, bp_ref)
    y_ref[...] = _out_tail(x_ref[...], (att_ref[...] * sga_ref[...]).astype(jnp.bfloat16),
                           (conv * sgc_ref[...]).astype(jnp.bfloat16), wo_att_ref, wo_conv_ref,
                           fg_ref)


def _mix_sample(x2d, att, sga, u, state, sgc, weights, *, n_new, seqs_per_block):
    rows, d_model = x2d.shape
    d_att = att.shape[1]
    n_seq, hist, d_conv = state.shape
    block_rows = seqs_per_block * n_new
    row = lambda c: pl.BlockSpec((block_rows, c), lambda i: (i, 0))
    st_spec = pl.BlockSpec((seqs_per_block, hist, d_conv), lambda i: (i, 0, 0))
    ext_rows = -(-(hist + n_new) // SUBLANES) * SUBLANES
    kernel = functools.partial(_mix_sample_kernel, n_new=n_new)
    w_out = weights[6]
    return pl.pallas_call(
        kernel,
        grid=(n_seq // seqs_per_block,),
        in_specs=[row(d_model), row(d_att), row(d_att), row(d_conv), st_spec, row(d_conv)]
        + _weight_specs(d_conv, d_model, d_att),
        out_specs=[row(d_model), st_spec],
        out_shape=[jax.ShapeDtypeStruct((rows, d_model), jnp.float32),
                   jax.ShapeDtypeStruct(state.shape, jnp.float32)],
        scratch_shapes=[pltpu.VMEM((seqs_per_block, ext_rows, d_conv), jnp.float32)],
        compiler_params=pltpu.CompilerParams(
            dimension_semantics=("arbitrary",), vmem_limit_bytes=VMEM_LIMIT_BYTES),
        name="mix_sample",
    )(x2d, att, sga, u, state, sgc, *weights[:6], w_out, w_out, weights[7])


def kernel(x_prompt, x_sample, cache_k, cache_v, state_conv, page_table, norm_g, w_in, lambda_q1,
           lambda_k1, lambda_q2, lambda_k2, subln_g, dw_w, dw_b, conv_ln_g, conv_ln_b, w_pw2,
           b_pw2, w_out, final_norm_g):
    batch, seq, d_model = x_prompt.shape
    n_seq, n_new, _ = x_sample.shape
    depth, _, _, n_heads, v_dim = cache_k.shape
    assert depth == 1 and n_heads == N_HEADS and v_dim == V_DIM
    d_conv = dw_w.shape[2]
    bf16 = jnp.bfloat16

    norm_g2 = norm_g[0].reshape(1, d_model)
    w_in_b = w_in[0].astype(bf16)
    lams = [t[0].reshape(1, HEAD_DIM) for t in (lambda_q1, lambda_k1, lambda_q2, lambda_k2)]
    subln = subln_g[0].reshape(1, V_DIM)
    mix_w = [dw_w[0], dw_b[0].reshape(1, d_conv), conv_ln_g[0].reshape(1, d_conv),
             conv_ln_b[0].reshape(1, d_conv), w_pw2[0].astype(bf16), b_pw2[0].reshape(1, d_conv),
             w_out[0].astype(bf16), final_norm_g.reshape(1, d_model)]

    xp = x_prompt.reshape(batch * seq, d_model)
    q, kb, vt, sga, convg, conv_p, k, v = _in_proj_prompt(
        xp, norm_g2, w_in_b, mix_w[:6], batch=batch, seq=seq, block_rows=512)
    attg = _prompt_attn(q, kb, vt, sga, lams, subln, batch=batch, seq=seq, block=512)
    yp = _out_prompt(xp, attg, convg, mix_w[6], mix_w[7], block_rows=512)
    kv_shape = (1, batch, seq, N_HEADS, V_DIM)

    xs = x_sample.reshape(n_seq * n_new, d_model)
    qs, ks, vs, sgas, us, sgcs = _in_proj_sample(xs, norm_g2, w_in_b, block_rows=512)
    att_s = _sample_attn(page_table, qs, ks, vs, lams, subln, cache_k, cache_v, n_new=n_new)
    ys, conv_s = _mix_sample(xs, att_s, sgas, us, state_conv[0], sgcs, mix_w, n_new=n_new,
                             seqs_per_block=32)
    kvs_shape = (1, n_seq, n_new, N_HEADS, V_DIM)

    return (yp.reshape(batch, seq, d_model), ys.reshape(n_seq, n_new, d_model),
            k.reshape(kv_shape), v.reshape(kv_shape), conv_p[None],
            ks.reshape(kvs_shape), vs.reshape(kvs_shape), conv_s[None])
```

```python
import functools
import math

import jax
import jax.numpy as jnp
from jax import lax
from jax.experimental import pallas as pl
from jax.experimental.pallas import tpu as pltpu

N_HEADS = 4
HEAD_DIM = 64
V_DIM = 2 * HEAD_DIM
CONV_W = 31
EPS = 1e-5
LAM_INIT = 0.8 - 0.6 * math.exp(-0.3 * 0)

Q_SCALE = HEAD_DIM ** -0.5 * math.log2(math.e)

SUBLANES = 8
VMEM_LIMIT_BYTES = 56 * 1024 * 1024

KEY_SUB = 256
ATTN_UNROLL = 4
CONV_HALO = 32
CONV_CHUNK = 64
NEG_INF = float("-inf")


def _silu(x):
    return x * jax.nn.sigmoid(x)


def _dot(a, b):
    return jnp.dot(a, b, preferred_element_type=jnp.float32)


def _dot_nt(a, b):
    return lax.dot_general(a, b, (((1,), (1,)), ((), ())), preferred_element_type=jnp.float32)


def _normed_input(x_ref, g_ref):
    x = x_ref[...]
    h = x * lax.rsqrt(jnp.mean(x * x, axis=-1, keepdims=True) + EPS) * g_ref[...]
    return h.astype(jnp.bfloat16)


def _conv_post(c, ln_g_ref, ln_b_ref, wp_ref, bp_ref):
    mu = jnp.mean(c, axis=-1, keepdims=True)
    d = c - mu
    var = jnp.mean(d * d, axis=-1, keepdims=True)
    ln = d * lax.rsqrt(var + EPS) * ln_g_ref[...] + ln_b_ref[...]
    return _dot(_silu(ln).astype(jnp.bfloat16), wp_ref[...]) + bp_ref[...]


def _in_proj_sample_kernel(x_ref, g_ref, w_ref, q_ref, k_ref, v_ref, sga_ref, u_ref, sgc_ref):
    hb = _normed_input(x_ref, g_ref)
    c = q_ref.shape[-1]

    def col(i):
        return _dot(hb, w_ref[:, i * c:(i + 1) * c])

    q_ref[...] = col(0) * Q_SCALE
    k = col(1)
    v = col(2)
    for h in range(N_HEADS):
        hs = slice(h * V_DIM, (h + 1) * V_DIM)
        k_ref[:, h, :] = k[:, hs]
        v_ref[:, h, :] = v[:, hs]
    sga_ref[...] = _silu(col(3))
    u_ref[...] = col(4) * jax.nn.sigmoid(col(5))
    sgc_ref[...] = _silu(col(6))


def _in_proj_sample(x2d, norm_g, w_in_bf16, *, block_rows):
    rows, d_model = x2d.shape
    c = w_in_bf16.shape[1] // 7
    row_spec = pl.BlockSpec((block_rows, c), lambda i: (i, 0))
    kv_spec = pl.BlockSpec((block_rows, N_HEADS, V_DIM), lambda i: (i, 0, 0))
    f32_rows = jax.ShapeDtypeStruct((rows, c), jnp.float32)
    kv_rows = jax.ShapeDtypeStruct((rows, N_HEADS, V_DIM), jnp.float32)
    return pl.pallas_call(
        _in_proj_sample_kernel,
        grid=(rows // block_rows,),
        in_specs=[
            pl.BlockSpec((block_rows, d_model), lambda i: (i, 0)),
            pl.BlockSpec((1, d_model), lambda i: (0, 0)),
            pl.BlockSpec(w_in_bf16.shape, lambda i: (0, 0)),
        ],
        out_specs=[row_spec, kv_spec, kv_spec, row_spec, row_spec, row_spec],
        out_shape=[f32_rows, kv_rows, kv_rows, f32_rows, f32_rows, f32_rows],
        compiler_params=pltpu.CompilerParams(
            dimension_semantics=("arbitrary",), vmem_limit_bytes=VMEM_LIMIT_BYTES),
        name="in_proj_sample",
    )(x2d, norm_g, w_in_bf16)


def _in_proj_prompt_kernel(x_ref, g_ref, w_ref, dw_ref, db_ref, ln_g_ref, ln_b_ref, wp_ref, bp_ref,
                           q_ref, kb_ref, vt_ref, sga_ref, cg_ref, cstate_ref, k_hbm, v_hbm,
                           hb_ref, ext_ref, shift_ref, sgc_ref, act_ref, kst_ref, vst_ref, sem, *,
                           tiles_per_seq):
    i = pl.program_id(0)
    rows = x_ref.shape[0]
    hb_ref[...] = _normed_input(x_ref, g_ref)
    c = q_ref.shape[-1]

    def col(j):
        return _dot(hb_ref[...], w_ref[:, j * c:(j + 1) * c])

    def kv_copies(step):
        dst_rows = pl.ds(step * rows, rows)
        copies = []
        for h in range(N_HEADS):
            copies.append(pltpu.make_async_copy(kst_ref.at[h], k_hbm.at[dst_rows, h, :], sem.at[0]))
            copies.append(pltpu.make_async_copy(vst_ref.at[h], v_hbm.at[dst_rows, h, :], sem.at[1]))
        return copies

    first = i % tiles_per_seq == 0

    @pl.when(first)
    def _():
        ext_ref[0:CONV_HALO, :] = jnp.zeros((CONV_HALO, c), jnp.float32)

    @pl.when(jnp.logical_not(first))
    def _():
        ext_ref[0:CONV_HALO, :] = ext_ref[rows:rows + CONV_HALO, :]

    @pl.when(i > 0)
    def _():
        for cp in kv_copies(i - 1):
            cp.wait()

    sgc_ref[...] = _silu(col(6))
    ext_ref[CONV_HALO:, :] = col(4) * jax.nn.sigmoid(col(5))
    hist = CONV_W - 1
    cstate_ref[0] = ext_ref[CONV_HALO + rows - hist:CONV_HALO + rows, :]
    for o in range(1, SUBLANES):
        shift_ref[o - 1] = ext_ref[o:o + shift_ref.shape[1], :]

    def conv_chunk(ch):
        r0 = ch * CONV_CHUNK
        acc = jnp.broadcast_to(db_ref[...], (CONV_CHUNK, c))
        for w in range(CONV_W):
            start = CONV_HALO - hist + w
            o = start % SUBLANES
            a = r0 + start - o
            src = ext_ref if o == 0 else shift_ref.at[o - 1]
            acc = acc + src[a:a + CONV_CHUNK, :] * dw_ref[w:w + 1, :]
        mu = jnp.mean(acc, axis=-1, keepdims=True)
        d = acc - mu
        var = jnp.mean(d * d, axis=-1, keepdims=True)
        ln = d * lax.rsqrt(var + EPS) * ln_g_ref[...] + ln_b_ref[...]
        act_ref[r0:r0 + CONV_CHUNK, :] = _silu(ln).astype(jnp.bfloat16)

    def half_col(j, half):
        lo = half * (c // 2)
        return _dot(hb_ref[...], w_ref[:, j * c + lo:j * c + lo + c // 2]), slice(lo, lo + c // 2)

    def q_piece(half):
        z, cols = half_col(0, half)
        q_ref[:, cols] = (z * Q_SCALE).astype(jnp.bfloat16)

    def kv_piece(j, flat_ref, stage_ref, half):
        z, cols = half_col(j, half)
        heads_per_half = N_HEADS // 2
        for hh in range(heads_per_half):
            stage_ref[half * heads_per_half + hh] = z[:, hh * V_DIM:(hh + 1) * V_DIM]
        return z, cols

    def k_piece(half):
        z, cols = kv_piece(1, kb_ref, kst_ref, half)
        kb_ref[:, cols] = z.astype(jnp.bfloat16)

    def v_piece(half):
        z, cols = kv_piece(2, vt_ref, vst_ref, half)
        vt_ref[cols, :] = z.T.astype(jnp.bfloat16)

    def gate_piece(half):
        z, cols = half_col(3, half)
        sga_ref[:, cols] = _silu(z).astype(jnp.bfloat16)

    for piece in (q_piece, k_piece, v_piece, gate_piece):
        for half in range(2):
            piece(half)
    for cp in kv_copies(i):
        cp.start()
    for ch in range(rows // CONV_CHUNK):
        conv_chunk(ch)

    conv = _dot(act_ref[...], wp_ref[...]) + bp_ref[...]
    cg_ref[...] = (conv * sgc_ref[...]).astype(jnp.bfloat16)

    @pl.when(i == pl.num_programs(0) - 1)
    def _():
        for cp in kv_copies(i):
            cp.wait()


def _in_proj_prompt(x2d, norm_g, w_in_bf16, conv_w, *, batch, seq, block_rows):
    rows, d_model = x2d.shape
    c = w_in_bf16.shape[1] // 7
    tiles_per_seq = seq // block_rows
    hist = CONV_W - 1
    full = lambda a: pl.BlockSpec(a.shape, lambda i: (0,) * a.ndim)
    row_spec = pl.BlockSpec((block_rows, c), lambda i: (i, 0))
    bf16_rows = jax.ShapeDtypeStruct((rows, c), jnp.bfloat16)
    kv_rows = jax.ShapeDtypeStruct((rows, N_HEADS, V_DIM), jnp.float32)
    any_spec = pl.BlockSpec(memory_space=pl.ANY)
    kernel = functools.partial(_in_proj_prompt_kernel, tiles_per_seq=tiles_per_seq)
    return pl.pallas_call(
        kernel,
        grid=(rows // block_rows,),
        in_specs=[pl.BlockSpec((block_rows, d_model), lambda i: (i, 0)), full(norm_g),
                  full(w_in_bf16)] + [full(a) for a in conv_w],
        out_specs=[row_spec, row_spec, pl.BlockSpec((c, block_rows), lambda i: (0, i)), row_spec,
                   row_spec,
                   pl.BlockSpec((1, hist, c), lambda i: (i // tiles_per_seq, 0, 0)),
                   any_spec, any_spec],
        out_shape=[bf16_rows, bf16_rows, jax.ShapeDtypeStruct((c, rows), jnp.bfloat16), bf16_rows,
                   bf16_rows, jax.ShapeDtypeStruct((batch, hist, c), jnp.float32),
                   kv_rows, kv_rows],
        scratch_shapes=[
            pltpu.VMEM((block_rows, d_model), jnp.bfloat16),
            pltpu.VMEM((CONV_HALO + block_rows, c), jnp.float32),
            pltpu.VMEM((SUBLANES - 1, CONV_HALO + block_rows - SUBLANES, c), jnp.float32),
            pltpu.VMEM((block_rows, c), jnp.float32),
            pltpu.VMEM((block_rows, c), jnp.bfloat16),
            pltpu.VMEM((N_HEADS, block_rows, V_DIM), jnp.float32),
            pltpu.VMEM((N_HEADS, block_rows, V_DIM), jnp.float32),
            pltpu.SemaphoreType.DMA((2,)),
        ],
        compiler_params=pltpu.CompilerParams(
            dimension_semantics=("arbitrary",), vmem_limit_bytes=VMEM_LIMIT_BYTES),
        name="in_proj_prompt",
    )(x2d, norm_g, w_in_bf16, *conv_w)


def _lam(lq1_ref, lk1_ref, lq2_ref, lk2_ref):
    s1 = jnp.sum(lq1_ref[...] * lk1_ref[...], axis=-1, keepdims=True)
    s2 = jnp.sum(lq2_ref[...] * lk2_ref[...], axis=-1, keepdims=True)
    return jnp.exp(s1) - jnp.exp(s2) + LAM_INIT


def _split_maps(q):
    lane = lax.broadcasted_iota(jnp.int32, q.shape, 1)
    zero = jnp.zeros_like(q)
    return jnp.where(lane < HEAD_DIM, q, zero), jnp.where(lane >= HEAD_DIM, q, zero)


def _prompt_attn_kernel(q_ref, k_ref, vt_ref, sga_ref, lq1_ref, lk1_ref, lq2_ref, lk2_ref, g_ref,
                        o_ref, m_ref, l_ref, acc_ref, s_ref, *, block):
    qi = pl.program_id(2)
    qs = _split_maps(q_ref[...])

    m_ref[...] = jnp.full_like(m_ref, NEG_INF)
    l_ref[...] = jnp.zeros_like(l_ref)
    acc_ref[...] = jnp.zeros_like(acc_ref)

    n_sub = block // KEY_SUB

    def fold(x, op):
        return op(x.reshape(KEY_SUB // SUBLANES, SUBLANES, block), axis=0)

    def sub(c):
        return slice(c * KEY_SUB, (c + 1) * KEY_SUB)

    def scores(kc, mp):
        start = pl.multiple_of(kc * block, block)
        for c in range(n_sub):
            rows = pl.ds(start + c * KEY_SUB, KEY_SUB)
            s_ref[mp, sub(c), :] = _dot_nt(k_ref[rows, :], qs[mp])

    def mask_diagonal(mp):
        for c in range(n_sub):
            s = s_ref[mp, sub(c), :]
            key = lax.broadcasted_iota(jnp.int32, s.shape, 0) + c * KEY_SUB
            qry = lax.broadcasted_iota(jnp.int32, s.shape, 1)
            s_ref[mp, sub(c), :] = jnp.where(key <= qry, s, NEG_INF)

    def accumulate(kc, mp):
        start = pl.multiple_of(kc * block, block)
        mx = None
        for c in range(n_sub):
            part = fold(s_ref[mp, sub(c), :], jnp.max)
            mx = part if mx is None else jnp.maximum(mx, part)
        m_prev = m_ref[mp]
        m_new = jnp.maximum(m_prev, jnp.max(mx, axis=0, keepdims=True))
        alpha = jnp.exp2(m_prev - m_new)
        lsum = None
        pv = None
        for c in range(n_sub):
            p = jnp.exp2(s_ref[mp, sub(c), :] - m_new)
            part = fold(p, jnp.sum)
            lsum = part if lsum is None else lsum + part
            cols = pl.ds(start + c * KEY_SUB, KEY_SUB)
            d = _dot(vt_ref[:, cols], p.astype(jnp.bfloat16))
            pv = d if pv is None else pv + d
        l_ref[mp] = alpha * l_ref[mp] + jnp.sum(lsum, axis=0, keepdims=True)
        acc_ref[mp] = alpha * acc_ref[mp] + pv
        m_ref[mp] = m_new

    scores(0, 0)

    def pipelined(kc):
        scores(kc, 1)
        accumulate(kc, 0)
        scores(kc + 1, 0)
        accumulate(kc, 1)

    def pipelined_group(first, n):
        for j in range(n):
            pipelined(first + j)

    def unrolled(t, carry):
        pipelined_group(t * ATTN_UNROLL, ATTN_UNROLL)
        return carry

    n_unrolled = lax.shift_right_logical(qi, ATTN_UNROLL.bit_length() - 1)
    lax.fori_loop(0, n_unrolled, unrolled, 0)
    done = n_unrolled * ATTN_UNROLL
    n = ATTN_UNROLL // 2
    while n >= 1:
        @pl.when(lax.bitwise_and(qi, n) != 0)
        def _(n=n, first=done):
            pipelined_group(first, n)
        done = done + lax.bitwise_and(qi, n)
        n //= 2

    scores(qi, 1)
    for mp in range(2):
        mask_diagonal(mp)
        accumulate(qi, mp)

    lam = _lam(lq1_ref, lk1_ref, lq2_ref, lk2_ref)
    o = acc_ref[0] / l_ref[0] - lam * (acc_ref[1] / l_ref[1])
    y = o * lax.rsqrt(jnp.mean(o * o, axis=0, keepdims=True) + EPS)
    att = (y * g_ref[...] * (1.0 - LAM_INIT)).T
    o_ref[...] = (att * sga_ref[...].astype(jnp.float32)).astype(o_ref.dtype)


def _prompt_attn(q, kb, vt, sga, lams, subln_g, *, batch, seq, block):
    rows, d_att = q.shape
    nq = seq // block
    vec = lambda n: pl.BlockSpec((1, n), lambda b, h, i: (0, 0))
    q_spec = pl.BlockSpec((block, V_DIM), lambda b, h, i: (b * nq + i, h))
    kernel = functools.partial(_prompt_attn_kernel, block=block)
    return pl.pallas_call(
        kernel,
        grid=(batch, N_HEADS, nq),
        in_specs=[
            q_spec,
            pl.BlockSpec((seq, V_DIM), lambda b, h, i: (b, h)),
            pl.BlockSpec((V_DIM, seq), lambda b, h, i: (h, b)),
            q_spec,
            vec(HEAD_DIM), vec(HEAD_DIM), vec(HEAD_DIM), vec(HEAD_DIM),
            pl.BlockSpec((V_DIM, 1), lambda b, h, i: (0, 0)),
        ],
        out_specs=q_spec,
        out_shape=jax.ShapeDtypeStruct((rows, d_att), jnp.bfloat16),
        scratch_shapes=[
            pltpu.VMEM((2, 1, block), jnp.float32),
            pltpu.VMEM((2, 1, block), jnp.float32),
            pltpu.VMEM((2, V_DIM, block), jnp.float32),
            pltpu.VMEM((2, block, block), jnp.float32),
        ],
        compiler_params=pltpu.CompilerParams(
            dimension_semantics=("arbitrary", "arbitrary", "arbitrary"),
            vmem_limit_bytes=VMEM_LIMIT_BYTES),
        name="prompt_attn",
    )(q, kb, vt, sga, *lams, subln_g.reshape(V_DIM, 1))


def _sample_attn_kernel(pt_ref, q_ref, kn_ref, vn_ref, lq1_ref, lk1_ref, lq2_ref, lk2_ref, g_ref,
                        ck_hbm, cv_hbm, o_ref, kbuf, vbuf, sem, *, n_pages, page):
    s_idx = pl.program_id(0)
    n_seq = pl.num_programs(0)
    slot = s_idx % 2

    def page_copies(seq, slot_):
        copies = []
        for p in range(n_pages):
            pid = pt_ref[seq * n_pages + p]
            dst = pl.ds(p * page, page)
            for h in range(N_HEADS):
                copies.append(pltpu.make_async_copy(
                    ck_hbm.at[0, pid, :, h, :], kbuf.at[slot_, h, dst], sem.at[0, slot_]))
                copies.append(pltpu.make_async_copy(
                    cv_hbm.at[0, pid, :, h, :], vbuf.at[slot_, h, dst], sem.at[1, slot_]))
        return copies

    @pl.when(s_idx == 0)
    def _():
        for cp in page_copies(s_idx, slot):
            cp.start()

    @pl.when(s_idx + 1 < n_seq)
    def _():
        for cp in page_copies(s_idx + 1, 1 - slot):
            cp.start()

    for cp in page_copies(s_idx, slot):
        cp.wait()

    lam = _lam(lq1_ref, lk1_ref, lq2_ref, lk2_ref)
    t = q_ref.shape[0]
    row = lax.broadcasted_iota(jnp.int32, (t, t), 0)
    colk = lax.broadcasted_iota(jnp.int32, (t, t), 1)
    causal = jnp.concatenate([colk <= row] * 2, axis=0)
    heads = range(N_HEADS)
    qz = [jnp.concatenate(_split_maps(q_ref[:, h * V_DIM:(h + 1) * V_DIM].astype(jnp.bfloat16)),
                          axis=0) for h in heads]
    s_past = [_dot_nt(qz[h], kbuf[slot, h].astype(jnp.bfloat16)) for h in heads]
    s_new = [jnp.where(causal, _dot_nt(qz[h], kn_ref[:, h, :].astype(jnp.bfloat16)), NEG_INF)
             for h in heads]
    a_past, a_new = [], []
    for h in heads:
        m = jnp.maximum(jnp.max(s_past[h], axis=-1, keepdims=True),
                        jnp.max(s_new[h], axis=-1, keepdims=True))
        e_past = jnp.exp2(s_past[h] - m)
        e_new = jnp.exp2(s_new[h] - m)
        denom = jnp.sum(e_past, axis=-1, keepdims=True) + jnp.sum(e_new, axis=-1, keepdims=True)
        p_past = e_past / denom
        p_new = e_new / denom
        a_past.append((p_past[:t] - lam * p_past[t:]).astype(jnp.bfloat16))
        a_new.append((p_new[:t] - lam * p_new[t:]).astype(jnp.bfloat16))
    for h in heads:
        o = (_dot(a_past[h], vbuf[slot, h].astype(jnp.bfloat16))
             + _dot(a_new[h], vn_ref[:, h, :].astype(jnp.bfloat16)))
        y = o * lax.rsqrt(jnp.mean(o * o, axis=-1, keepdims=True) + EPS)
        o_ref[:, h * V_DIM:(h + 1) * V_DIM] = y * g_ref[...] * (1.0 - LAM_INIT)


def _sample_attn(page_table, q, k_new, v_new, lams, subln_g, cache_k, cache_v, *, n_new):
    n_seq, n_pages = page_table.shape
    _, _, page, n_heads, v_dim = cache_k.shape
    past = n_pages * page
    d_att = q.shape[1]
    row_spec = pl.BlockSpec((n_new, d_att), lambda s, pt: (s, 0))
    kv_spec = pl.BlockSpec((n_new, n_heads, v_dim), lambda s, pt: (s, 0, 0))
    vec = lambda n: pl.BlockSpec((1, n), lambda s, pt: (0, 0))
    kernel = functools.partial(_sample_attn_kernel, n_pages=n_pages, page=page)
    return pl.pallas_call(
        kernel,
        grid_spec=pltpu.PrefetchScalarGridSpec(
            num_scalar_prefetch=1,
            grid=(n_seq,),
            in_specs=[
                row_spec, kv_spec, kv_spec,
                vec(HEAD_DIM), vec(HEAD_DIM), vec(HEAD_DIM), vec(HEAD_DIM), vec(V_DIM),
                pl.BlockSpec(memory_space=pl.ANY),
                pl.BlockSpec(memory_space=pl.ANY),
            ],
            out_specs=row_spec,
            scratch_shapes=[
                pltpu.VMEM((2, n_heads, past, v_dim), jnp.float32),
                pltpu.VMEM((2, n_heads, past, v_dim), jnp.float32),
                pltpu.SemaphoreType.DMA((2, 2)),
            ],
        ),
        out_shape=jax.ShapeDtypeStruct(q.shape, jnp.float32),
        compiler_params=pltpu.CompilerParams(
            dimension_semantics=("arbitrary",), vmem_limit_bytes=VMEM_LIMIT_BYTES),
        name="sample_attn",
    )(page_table.reshape(-1), q, k_new, v_new, *lams, subln_g, cache_k, cache_v)


def _out_tail(x, attg, convg, wo_att_ref, wo_conv_ref, fg_ref):
    y = x + _dot(attg, wo_att_ref[...]) + _dot(convg, wo_conv_ref[...])
    return y * lax.rsqrt(jnp.mean(y * y, axis=-1, keepdims=True) + EPS) * fg_ref[...]


def _out_prompt_kernel(x_ref, attg_ref, cg_ref, wo_att_ref, wo_conv_ref, fg_ref, y_ref):
    y_ref[...] = _out_tail(x_ref[...], attg_ref[...], cg_ref[...], wo_att_ref, wo_conv_ref, fg_ref)


def _out_prompt(x2d, attg, convg, w_out_bf16, final_g, *, block_rows):
    rows, d_model = x2d.shape
    d_att = attg.shape[1]
    d_conv = convg.shape[1]
    row = lambda c: pl.BlockSpec((block_rows, c), lambda i: (i, 0))
    return pl.pallas_call(
        _out_prompt_kernel,
        grid=(rows // block_rows,),
        in_specs=[row(d_model), row(d_att), row(d_conv),
                  pl.BlockSpec((d_att, d_model), lambda i: (0, 0)),
                  pl.BlockSpec((d_conv, d_model), lambda i: (d_att // d_conv, 0)),
                  pl.BlockSpec((1, d_model), lambda i: (0, 0))],
        out_specs=row(d_model),
        out_shape=jax.ShapeDtypeStruct((rows, d_model), jnp.float32),
        compiler_params=pltpu.CompilerParams(
            dimension_semantics=("arbitrary",), vmem_limit_bytes=VMEM_LIMIT_BYTES),
        name="out_prompt",
    )(x2d, attg, convg, w_out_bf16, w_out_bf16, final_g)


def _weight_specs(d_conv, d_model, d_att):
    full = lambda r, c: pl.BlockSpec((r, c), lambda i: (0, 0))
    return [
        full(CONV_W, d_conv), full(1, d_conv), full(1, d_conv), full(1, d_conv),
        full(d_conv, d_conv), full(1, d_conv),
        pl.BlockSpec((d_att, d_model), lambda i: (0, 0)),
        pl.BlockSpec((d_conv, d_model), lambda i: (d_att // d_conv, 0)),
        full(1, d_model),
    ]


def _mix_sample_kernel(x_ref, att_ref, sga_ref, u_ref, st_ref, sgc_ref, dw_ref, db_ref,
                       ln_g_ref, ln_b_ref, wp_ref, bp_ref, wo_att_ref, wo_conv_ref, fg_ref,
                       y_ref, st_out_ref, ext_ref, *, n_new):
    g = st_ref.shape[0]
    d_conv = st_ref.shape[2]
    hist = CONV_W - 1
    ext_ref[:, 0:hist, :] = st_ref[...]
    ext_ref[:, hist:hist + n_new, :] = u_ref[...].reshape(g, n_new, d_conv)
    c = jnp.broadcast_to(db_ref[...].reshape(1, 1, d_conv), (g, n_new, d_conv))
    for w in range(CONV_W):
        c = c + ext_ref[:, w:w + n_new, :] * dw_ref[w:w + 1, :].reshape(1, 1, d_conv)
    st_out_ref[...] = ext_ref[:, n_new:n_new + hist, :]
    conv = _conv_post(c.reshape(g * n_new, d_conv), ln_g_ref, ln_b_ref, wp_ref, bp_ref)
    y_ref[...] = _out_tail(x_ref[...], (att_ref[...] * sga_ref[...]).astype(jnp.bfloat16),
                           (conv * sgc_ref[...]).astype(jnp.bfloat16), wo_att_ref, wo_conv_ref,
                           fg_ref)


def _mix_sample(x2d, att, sga, u, state, sgc, weights, *, n_new, seqs_per_block):
    rows, d_model = x2d.shape
    d_att = att.shape[1]
    n_seq, hist, d_conv = state.shape
    block_rows = seqs_per_block * n_new
    row = lambda c: pl.BlockSpec((block_rows, c), lambda i: (i, 0))
    st_spec = pl.BlockSpec((seqs_per_block, hist, d_conv), lambda i: (i, 0, 0))
    ext_rows = -(-(hist + n_new) // SUBLANES) * SUBLANES
    kernel = functools.partial(_mix_sample_kernel, n_new=n_new)
    w_out = weights[6]
    return pl.pallas_call(
        kernel,
        grid=(n_seq // seqs_per_block,),
        in_specs=[row(d_model), row(d_att), row(d_att), row(d_conv), st_spec, row(d_conv)]
        + _weight_specs(d_conv, d_model, d_att),
        out_specs=[row(d_model), st_spec],
        out_shape=[jax.ShapeDtypeStruct((rows, d_model), jnp.float32),
                   jax.ShapeDtypeStruct(state.shape, jnp.float32)],
        scratch_shapes=[pltpu.VMEM((seqs_per_block, ext_rows, d_conv), jnp.float32)],
        compiler_params=pltpu.CompilerParams(
            dimension_semantics=("arbitrary",), vmem_limit_bytes=VMEM_LIMIT_BYTES),
        name="mix_sample",
    )(x2d, att, sga, u, state, sgc, *weights[:6], w_out, w_out, weights[7])


def kernel(x_prompt, x_sample, cache_k, cache_v, state_conv, page_table, norm_g, w_in, lambda_q1,
           lambda_k1, lambda_q2, lambda_k2, subln_g, dw_w, dw_b, conv_ln_g, conv_ln_b, w_pw2,
           b_pw2, w_out, final_norm_g):
    batch, seq, d_model = x_prompt.shape
    n_seq, n_new, _ = x_sample.shape
    depth, _, _, n_heads, v_dim = cache_k.shape
    assert depth == 1 and n_heads == N_HEADS and v_dim == V_DIM
    d_conv = dw_w.shape[2]
    bf16 = jnp.bfloat16

    norm_g2 = norm_g[0].reshape(1, d_model)
    w_in_b = w_in[0].astype(bf16)
    lams = [t[0].reshape(1, HEAD_DIM) for t in (lambda_q1, lambda_k1, lambda_q2, lambda_k2)]
    subln = subln_g[0].reshape(1, V_DIM)
    mix_w = [dw_w[0], dw_b[0].reshape(1, d_conv), conv_ln_g[0].reshape(1, d_conv),
             conv_ln_b[0].reshape(1, d_conv), w_pw2[0].astype(bf16), b_pw2[0].reshape(1, d_conv),
             w_out[0].astype(bf16), final_norm_g.reshape(1, d_model)]

    xp = x_prompt.reshape(batch * seq, d_model)
    q, kb, vt, sga, convg, conv_p, k, v = _in_proj_prompt(
        xp, norm_g2, w_in_b, mix_w[:6], batch=batch, seq=seq, block_rows=512)
    attg = _prompt_attn(q, kb, vt, sga, lams, subln, batch=batch, seq=seq, block=512)
    yp = _out_prompt(xp, attg, convg, mix_w[6], mix_w[7], block_rows=512)
    kv_shape = (1, batch, seq, N_HEADS, V_DIM)

    xs = x_sample.reshape(n_seq * n_new, d_model)
    qs, ks, vs, sgas, us, sgcs = _in_proj_sample(xs, norm_g2, w_in_b, block_rows=512)
    att_s = _sample_attn(page_table, qs, ks, vs, lams, subln, cache_k, cache_v, n_new=n_new)
    ys, conv_s = _mix_sample(xs, att_s, sgas, us, state_conv[0], sgcs, mix_w, n_new=n_new,
                             seqs_per_block=32)
    kvs_shape = (1, n_seq, n_new, N_HEADS, V_DIM)

    return (yp.reshape(batch, seq, d_model), ys.reshape(n_seq, n_new, d_model),
            k.reshape(kv_shape), v.reshape(kv_shape), conv_p[None],
            ks.reshape(kvs_shape), vs.reshape(kvs_shape), conv_s[None])
```

```python
import functools
import math

import jax
import jax.numpy as jnp
from jax import lax
from jax.experimental import pallas as pl
from jax.experimental.pallas import tpu as pltpu

N_HEADS = 4
HEAD_DIM = 64
V_DIM = 2 * HEAD_DIM
CONV_W = 31
EPS = 1e-5
LAM_INIT = 0.8 - 0.6 * math.exp(-0.3 * 0)

Q_SCALE = HEAD_DIM ** -0.5 * math.log2(math.e)

SUBLANES = 8
VMEM_LIMIT_BYTES = 56 * 1024 * 1024

KEY_SUB = 256
ATTN_UNROLL = 4
CONV_HALO = 32
CONV_CHUNK = 64
NEG_INF = float("-inf")


def _silu(x):
    return x * jax.nn.sigmoid(x)


def _dot(a, b):
    return jnp.dot(a, b, preferred_element_type=jnp.float32)


def _dot_nt(a, b):
    return lax.dot_general(a, b, (((1,), (1,)), ((), ())), preferred_element_type=jnp.float32)


def _normed_input(x_ref, g_ref):
    x = x_ref[...]
    h = x * lax.rsqrt(jnp.mean(x * x, axis=-1, keepdims=True) + EPS) * g_ref[...]
    return h.astype(jnp.bfloat16)


def _layer_norm_silu(c, ln_g_ref, ln_b_ref):
    mu = jnp.mean(c, axis=-1, keepdims=True)
    d = c - mu
    var = jnp.mean(d * d, axis=-1, keepdims=True)
    ln = d * lax.rsqrt(var + EPS) * ln_g_ref[...] + ln_b_ref[...]
    return _silu(ln).astype(jnp.bfloat16)


def _in_proj_sample_kernel(x_ref, g_ref, w_ref, q_ref, k_ref, v_ref, sga_ref, u_ref, sgc_ref):
    hb = _normed_input(x_ref, g_ref)
    c = q_ref.shape[-1]

    def col(i):
        return _dot(hb, w_ref[:, i * c:(i + 1) * c])

    q_ref[...] = col(0) * Q_SCALE
    k = col(1)
    v = col(2)
    for h in range(N_HEADS):
        hs = slice(h * V_DIM, (h + 1) * V_DIM)
        k_ref[:, h, :] = k[:, hs]
        v_ref[:, h, :] = v[:, hs]
    sga_ref[...] = _silu(col(3))
    u_ref[...] = col(4) * jax.nn.sigmoid(col(5))
    sgc_ref[...] = _silu(col(6))


def _in_proj_sample(x2d, norm_g, w_in_bf16, *, block_rows):
    rows, d_model = x2d.shape
    c = w_in_bf16.shape[1] // 7
    row_spec = pl.BlockSpec((block_rows, c), lambda i: (i, 0))
    kv_spec = pl.BlockSpec((block_rows, N_HEADS, V_DIM), lambda i: (i, 0, 0))
    f32_rows = jax.ShapeDtypeStruct((rows, c), jnp.float32)
    kv_rows = jax.ShapeDtypeStruct((rows, N_HEADS, V_DIM), jnp.float32)
    return pl.pallas_call(
        _in_proj_sample_kernel,
        grid=(rows // block_rows,),
        in_specs=[
            pl.BlockSpec((block_rows, d_model), lambda i: (i, 0)),
            pl.BlockSpec((1, d_model), lambda i: (0, 0)),
            pl.BlockSpec(w_in_bf16.shape, lambda i: (0, 0)),
        ],
        out_specs=[row_spec, kv_spec, kv_spec, row_spec, row_spec, row_spec],
        out_shape=[f32_rows, kv_rows, kv_rows, f32_rows, f32_rows, f32_rows],
        compiler_params=pltpu.CompilerParams(
            dimension_semantics=("arbitrary",), vmem_limit_bytes=VMEM_LIMIT_BYTES),
        name="in_proj_sample",
    )(x2d, norm_g, w_in_bf16)


def _in_proj_prompt_kernel(x_ref, g_ref, w_ref, q_ref, kb_ref, vt_ref, sga_ref, u_ref, sgc_ref,
                           cstate_ref, k_hbm, v_hbm, kst_ref, vst_ref, sem):
    i = pl.program_id(0)
    rows = x_ref.shape[0]
    hb = _normed_input(x_ref, g_ref)
    c = q_ref.shape[-1]

    def col(j):
        return _dot(hb, w_ref[:, j * c:(j + 1) * c])

    def kv_copies(step):
        dst_rows = pl.ds(step * rows, rows)
        copies = []
        for h in range(N_HEADS):
            copies.append(pltpu.make_async_copy(kst_ref.at[h], k_hbm.at[dst_rows, h, :], sem.at[0]))
            copies.append(pltpu.make_async_copy(vst_ref.at[h], v_hbm.at[dst_rows, h, :], sem.at[1]))
        return copies

    q_ref[...] = (col(0) * Q_SCALE).astype(jnp.bfloat16)
    k = col(1)
    v = col(2)
    kb_ref[...] = k.astype(jnp.bfloat16)
    vt_ref[...] = v.T.astype(jnp.bfloat16)

    @pl.when(i > 0)
    def _():
        for cp in kv_copies(i - 1):
            cp.wait()

    for h in range(N_HEADS):
        kst_ref[h] = k[:, h * V_DIM:(h + 1) * V_DIM]
        vst_ref[h] = v[:, h * V_DIM:(h + 1) * V_DIM]
    for cp in kv_copies(i):
        cp.start()

    sga_ref[...] = _silu(col(3)).astype(jnp.bfloat16)
    u_ref[...] = col(4) * jax.nn.sigmoid(col(5))
    sgc_ref[...] = _silu(col(6)).astype(jnp.bfloat16)
    cstate_ref[0] = u_ref[rows - (CONV_W - 1):rows, :]

    @pl.when(i == pl.num_programs(0) - 1)
    def _():
        for cp in kv_copies(i):
            cp.wait()


def _in_proj_prompt(x2d, norm_g, w_in_bf16, *, batch, seq, block_rows):
    rows, d_model = x2d.shape
    c = w_in_bf16.shape[1] // 7
    tiles_per_seq = seq // block_rows
    hist = CONV_W - 1
    full = lambda a: pl.BlockSpec(a.shape, lambda i: (0,) * a.ndim)
    row_spec = pl.BlockSpec((block_rows, c), lambda i: (i, 0))
    bf16_rows = jax.ShapeDtypeStruct((rows, c), jnp.bfloat16)
    kv_rows = jax.ShapeDtypeStruct((rows, N_HEADS, V_DIM), jnp.float32)
    any_spec = pl.BlockSpec(memory_space=pl.ANY)
    return pl.pallas_call(
        _in_proj_prompt_kernel,
        grid=(rows // block_rows,),
        in_specs=[pl.BlockSpec((block_rows, d_model), lambda i: (i, 0)), full(norm_g),
                  full(w_in_bf16)],
        out_specs=[row_spec, row_spec, pl.BlockSpec((c, block_rows), lambda i: (0, i)), row_spec,
                   row_spec, row_spec,
                   pl.BlockSpec((1, hist, c), lambda i: (i // tiles_per_seq, 0, 0)),
                   any_spec, any_spec],
        out_shape=[bf16_rows, bf16_rows, jax.ShapeDtypeStruct((c, rows), jnp.bfloat16), bf16_rows,
                   jax.ShapeDtypeStruct((rows, c), jnp.float32), bf16_rows,
                   jax.ShapeDtypeStruct((batch, hist, c), jnp.float32), kv_rows, kv_rows],
        scratch_shapes=[
            pltpu.VMEM((N_HEADS, block_rows, V_DIM), jnp.float32),
            pltpu.VMEM((N_HEADS, block_rows, V_DIM), jnp.float32),
            pltpu.SemaphoreType.DMA((2,)),
        ],
        compiler_params=pltpu.CompilerParams(
            dimension_semantics=("arbitrary",), vmem_limit_bytes=VMEM_LIMIT_BYTES),
        name="in_proj_prompt",
    )(x2d, norm_g, w_in_bf16)


def _prompt_conv_history(rows, first, ext_ref):
    @pl.when(first)
    def _():
        ext_ref[0:CONV_HALO, :] = jnp.zeros((CONV_HALO, ext_ref.shape[1]), jnp.float32)

    @pl.when(jnp.logical_not(first))
    def _():
        ext_ref[0:CONV_HALO, :] = ext_ref[rows:rows + CONV_HALO, :]


def _prompt_conv_rows(u_ref, sgc_ref, dw_ref, db_ref, ln_g_ref, ln_b_ref, wp_ref, bp_ref,
                      cg_ref, ext_ref, shift_ref):
    rows, c = u_ref.shape
    hist = CONV_W - 1
    ext_ref[CONV_HALO:, :] = u_ref[...]
    for o in range(1, SUBLANES):
        shift_ref[o - 1] = ext_ref[o:o + shift_ref.shape[1], :]

    acts = []
    for ch in range(rows // CONV_CHUNK):
        r0 = ch * CONV_CHUNK
        acc = jnp.broadcast_to(db_ref[...], (CONV_CHUNK, c))
        for w in range(CONV_W):
            start = CONV_HALO - hist + w
            o = start % SUBLANES
            a = r0 + start - o
            src = ext_ref if o == 0 else shift_ref.at[o - 1]
            acc = acc + src[a:a + CONV_CHUNK, :] * dw_ref[w:w + 1, :]
        acts.append(_layer_norm_silu(acc, ln_g_ref, ln_b_ref))
    conv = _dot(jnp.concatenate(acts, axis=0), wp_ref[...]) + bp_ref[...]
    cg_ref[...] = (conv * sgc_ref[...].astype(jnp.float32)).astype(jnp.bfloat16)


def _lam(lq1_ref, lk1_ref, lq2_ref, lk2_ref):
    s1 = jnp.sum(lq1_ref[...] * lk1_ref[...], axis=-1, keepdims=True)
    s2 = jnp.sum(lq2_ref[...] * lk2_ref[...], axis=-1, keepdims=True)
    return jnp.exp(s1) - jnp.exp(s2) + LAM_INIT


def _split_maps(q):
    lane = lax.broadcasted_iota(jnp.int32, q.shape, 1)
    zero = jnp.zeros_like(q)
    return jnp.where(lane < HEAD_DIM, q, zero), jnp.where(lane >= HEAD_DIM, q, zero)


def _prompt_attn_kernel(q_ref, k_ref, vt_ref, sga_ref, lq1_ref, lk1_ref, lq2_ref, lk2_ref, g_ref,
                        o_ref, m_ref, l_ref, acc_ref, s_ref, *, block):
    qi = pl.program_id(2)
    qs = _split_maps(q_ref[...])

    m_ref[...] = jnp.full_like(m_ref, NEG_INF)
    l_ref[...] = jnp.zeros_like(l_ref)
    acc_ref[...] = jnp.zeros_like(acc_ref)

    n_sub = block // KEY_SUB

    def fold(x, op):
        return op(x.reshape(KEY_SUB // SUBLANES, SUBLANES, block), axis=0)

    def sub(c):
        return slice(c * KEY_SUB, (c + 1) * KEY_SUB)

    def scores(kc, mp):
        start = pl.multiple_of(kc * block, block)
        for c in range(n_sub):
            rows = pl.ds(start + c * KEY_SUB, KEY_SUB)
            s_ref[mp, sub(c), :] = _dot_nt(k_ref[rows, :], qs[mp])

    def mask_diagonal(mp):
        for c in range(n_sub):
            s = s_ref[mp, sub(c), :]
            key = lax.broadcasted_iota(jnp.int32, s.shape, 0) + c * KEY_SUB
            qry = lax.broadcasted_iota(jnp.int32, s.shape, 1)
            s_ref[mp, sub(c), :] = jnp.where(key <= qry, s, NEG_INF)

    def accumulate(kc, mp):
        start = pl.multiple_of(kc * block, block)
        mx = None
        for c in range(n_sub):
            part = fold(s_ref[mp, sub(c), :], jnp.max)
            mx = part if mx is None else jnp.maximum(mx, part)
        m_prev = m_ref[mp]
        m_new = jnp.maximum(m_prev, jnp.max(mx, axis=0, keepdims=True))
        alpha = jnp.exp2(m_prev - m_new)
        lsum = None
        pv = None
        for c in range(n_sub):
            p = jnp.exp2(s_ref[mp, sub(c), :] - m_new)
            part = fold(p, jnp.sum)
            lsum = part if lsum is None else lsum + part
            cols = pl.ds(start + c * KEY_SUB, KEY_SUB)
            d = _dot(vt_ref[:, cols], p.astype(jnp.bfloat16))
            pv = d if pv is None else pv + d
        l_ref[mp] = alpha * l_ref[mp] + jnp.sum(lsum, axis=0, keepdims=True)
        acc_ref[mp] = alpha * acc_ref[mp] + pv
        m_ref[mp] = m_new

    scores(0, 0)

    def pipelined(kc):
        scores(kc, 1)
        accumulate(kc, 0)
        scores(kc + 1, 0)
        accumulate(kc, 1)

    def pipelined_group(first, n):
        for j in range(n):
            pipelined(first + j)

    def unrolled(t, carry):
        pipelined_group(t * ATTN_UNROLL, ATTN_UNROLL)
        return carry

    n_unrolled = lax.shift_right_logical(qi, ATTN_UNROLL.bit_length() - 1)
    lax.fori_loop(0, n_unrolled, unrolled, 0)
    done = n_unrolled * ATTN_UNROLL
    n = ATTN_UNROLL // 2
    while n >= 1:
        @pl.when(lax.bitwise_and(qi, n) != 0)
        def _(n=n, first=done):
            pipelined_group(first, n)
        done = done + lax.bitwise_and(qi, n)
        n //= 2

    scores(qi, 1)
    for mp in range(2):
        mask_diagonal(mp)
        accumulate(qi, mp)

    lam = _lam(lq1_ref, lk1_ref, lq2_ref, lk2_ref)
    o = acc_ref[0] / l_ref[0] - lam * (acc_ref[1] / l_ref[1])
    y = o * lax.rsqrt(jnp.mean(o * o, axis=0, keepdims=True) + EPS)
    att = (y * g_ref[...] * (1.0 - LAM_INIT)).T
    o_ref[...] = (att * sga_ref[...].astype(jnp.float32)).astype(o_ref.dtype)


def _prompt_attn(q, kb, vt, sga, lams, subln_g, *, batch, seq, block):
    rows, d_att = q.shape
    nq = seq // block
    vec = lambda n: pl.BlockSpec((1, n), lambda b, h, i: (0, 0))
    q_spec = pl.BlockSpec((block, V_DIM), lambda b, h, i: (b * nq + i, h))
    kernel = functools.partial(_prompt_attn_kernel, block=block)
    return pl.pallas_call(
        kernel,
        grid=(batch, N_HEADS, nq),
        in_specs=[
            q_spec,
            pl.BlockSpec((seq, V_DIM), lambda b, h, i: (b, h)),
            pl.BlockSpec((V_DIM, seq), lambda b, h, i: (h, b)),
            q_spec,
            vec(HEAD_DIM), vec(HEAD_DIM), vec(HEAD_DIM), vec(HEAD_DIM),
            pl.BlockSpec((V_DIM, 1), lambda b, h, i: (0, 0)),
        ],
        out_specs=q_spec,
        out_shape=jax.ShapeDtypeStruct((rows, d_att), jnp.bfloat16),
        scratch_shapes=[
            pltpu.VMEM((2, 1, block), jnp.float32),
            pltpu.VMEM((2, 1, block), jnp.float32),
            pltpu.VMEM((2, V_DIM, block), jnp.float32),
            pltpu.VMEM((2, block, block), jnp.float32),
        ],
        compiler_params=pltpu.CompilerParams(
            dimension_semantics=("arbitrary", "arbitrary", "arbitrary"),
            vmem_limit_bytes=VMEM_LIMIT_BYTES),
        name="prompt_attn",
    )(q, kb, vt, sga, *lams, subln_g.reshape(V_DIM, 1))


def _sample_attn_kernel(pt_ref, q_ref, kn_ref, vn_ref, lq1_ref, lk1_ref, lq2_ref, lk2_ref, g_ref,
                        u_ref, sgc_ref, dw_ref, db_ref, ln_g_ref, ln_b_ref, wp_ref, bp_ref,
                        ck_hbm, cv_hbm, o_ref, cg_ref, kbuf, vbuf, ext_ref, shift_ref, sem, *,
                        n_pages, page, conv_tiles_per_seq):
    s_idx = pl.program_id(0)
    n_seq = pl.num_programs(0)
    slot = s_idx % 2

    def page_copies(seq, slot_):
        copies = []
        for p in range(n_pages):
            pid = pt_ref[seq * n_pages + p]
            dst = pl.ds(p * page, page)
            for h in range(N_HEADS):
                copies.append(pltpu.make_async_copy(
                    ck_hbm.at[0, pid, :, h, :], kbuf.at[slot_, h, dst], sem.at[0, slot_]))
                copies.append(pltpu.make_async_copy(
                    cv_hbm.at[0, pid, :, h, :], vbuf.at[slot_, h, dst], sem.at[1, slot_]))
        return copies

    @pl.when(s_idx == 0)
    def _():
        for cp in page_copies(s_idx, slot):
            cp.start()

    @pl.when(s_idx + 1 < n_seq)
    def _():
        for cp in page_copies(s_idx + 1, 1 - slot):
            cp.start()

    _prompt_conv_history(u_ref.shape[0], s_idx % conv_tiles_per_seq == 0, ext_ref)

    for cp in page_copies(s_idx, slot):
        cp.wait()

    _prompt_conv_rows(u_ref, sgc_ref, dw_ref, db_ref, ln_g_ref, ln_b_ref, wp_ref, bp_ref, cg_ref,
                      ext_ref, shift_ref)

    lam = _lam(lq1_ref, lk1_ref, lq2_ref, lk2_ref)
    t = q_ref.shape[0]
    row = lax.broadcasted_iota(jnp.int32, (t, t), 0)
    colk = lax.broadcasted_iota(jnp.int32, (t, t), 1)
    causal = jnp.concatenate([colk <= row] * 2, axis=0)
    heads = range(N_HEADS)
    qz = [jnp.concatenate(_split_maps(q_ref[:, h * V_DIM:(h + 1) * V_DIM].astype(jnp.bfloat16)),
                          axis=0) for h in heads]
    s_past = [_dot_nt(qz[h], kbuf[slot, h].astype(jnp.bfloat16)) for h in heads]
    s_new = [jnp.where(causal, _dot_nt(qz[h], kn_ref[:, h, :].astype(jnp.bfloat16)), NEG_INF)
             for h in heads]
    a_past, a_new = [], []
    for h in heads:
        m = jnp.maximum(jnp.max(s_past[h], axis=-1, keepdims=True),
                        jnp.max(s_new[h], axis=-1, keepdims=True))
        e_past = jnp.exp2(s_past[h] - m)
        e_new = jnp.exp2(s_new[h] - m)
        denom = jnp.sum(e_past, axis=-1, keepdims=True) + jnp.sum(e_new, axis=-1, keepdims=True)
        p_past = e_past / denom
        p_new = e_new / denom
        a_past.append((p_past[:t] - lam * p_past[t:]).astype(jnp.bfloat16))
        a_new.append((p_new[:t] - lam * p_new[t:]).astype(jnp.bfloat16))
    for h in heads:
        o = (_dot(a_past[h], vbuf[slot, h].astype(jnp.bfloat16))
             + _dot(a_new[h], vn_ref[:, h, :].astype(jnp.bfloat16)))
        y = o * lax.rsqrt(jnp.mean(o * o, axis=-1, keepdims=True) + EPS)
        o_ref[:, h * V_DIM:(h + 1) * V_DIM] = y * g_ref[...] * (1.0 - LAM_INIT)


def _sample_attn(page_table, q, k_new, v_new, lams, subln_g, cache_k, cache_v, u_prompt,
                 sgc_prompt, conv_w, *, n_new, prompt_seq):
    n_seq, n_pages = page_table.shape
    _, _, page, n_heads, v_dim = cache_k.shape
    past = n_pages * page
    d_att = q.shape[1]
    prompt_rows, d_conv = u_prompt.shape
    conv_rows = prompt_rows // n_seq
    assert conv_rows * n_seq == prompt_rows and conv_rows % CONV_CHUNK == 0
    assert prompt_seq % conv_rows == 0 and conv_rows >= CONV_HALO
    row_spec = pl.BlockSpec((n_new, d_att), lambda s, pt: (s, 0))
    kv_spec = pl.BlockSpec((n_new, n_heads, v_dim), lambda s, pt: (s, 0, 0))
    conv_spec = pl.BlockSpec((conv_rows, d_conv), lambda s, pt: (s, 0))
    vec = lambda n: pl.BlockSpec((1, n), lambda s, pt: (0, 0))
    full = lambda a: pl.BlockSpec(a.shape, lambda s, pt: (0,) * a.ndim)
    kernel = functools.partial(_sample_attn_kernel, n_pages=n_pages, page=page,
                               conv_tiles_per_seq=prompt_seq // conv_rows)
    return pl.pallas_call(
        kernel,
        grid_spec=pltpu.PrefetchScalarGridSpec(
            num_scalar_prefetch=1,
            grid=(n_seq,),
            in_specs=[
                row_spec, kv_spec, kv_spec,
                vec(HEAD_DIM), vec(HEAD_DIM), vec(HEAD_DIM), vec(HEAD_DIM), vec(V_DIM),
                conv_spec, conv_spec] + [full(a) for a in conv_w] + [
                pl.BlockSpec(memory_space=pl.ANY),
                pl.BlockSpec(memory_space=pl.ANY),
            ],
            out_specs=[row_spec, conv_spec],
            scratch_shapes=[
                pltpu.VMEM((2, n_heads, past, v_dim), jnp.float32),
                pltpu.VMEM((2, n_heads, past, v_dim), jnp.float32),
                pltpu.VMEM((CONV_HALO + conv_rows, d_conv), jnp.float32),
                pltpu.VMEM((SUBLANES - 1, CONV_HALO + conv_rows - SUBLANES, d_conv), jnp.float32),
                pltpu.SemaphoreType.DMA((2, 2)),
            ],
        ),
        out_shape=[jax.ShapeDtypeStruct(q.shape, jnp.float32),
                   jax.ShapeDtypeStruct((prompt_rows, d_conv), jnp.bfloat16)],
        compiler_params=pltpu.CompilerParams(
            dimension_semantics=("arbitrary",), vmem_limit_bytes=VMEM_LIMIT_BYTES),
        name="sample_attn",
    )(page_table.reshape(-1), q, k_new, v_new, *lams, subln_g, u_prompt, sgc_prompt, *conv_w,
      cache_k, cache_v)


def _out_tail(x, attg, convg, wo_att_ref, wo_conv_ref, fg_ref):
    y = x + _dot(attg, wo_att_ref[...]) + _dot(convg, wo_conv_ref[...])
    return y * lax.rsqrt(jnp.mean(y * y, axis=-1, keepdims=True) + EPS) * fg_ref[...]


def _out_prompt_kernel(x_ref, attg_ref, cg_ref, wo_att_ref, wo_conv_ref, fg_ref, y_ref):
    y_ref[...] = _out_tail(x_ref[...], attg_ref[...], cg_ref[...], wo_att_ref, wo_conv_ref, fg_ref)


def _out_prompt(x2d, attg, convg, w_out_bf16, final_g, *, block_rows):
    rows, d_model = x2d.shape
    d_att = attg.shape[1]
    d_conv = convg.shape[1]
    row = lambda c: pl.BlockSpec((block_rows, c), lambda i: (i, 0))
    return pl.pallas_call(
        _out_prompt_kernel,
        grid=(rows // block_rows,),
        in_specs=[row(d_model), row(d_att), row(d_conv),
                  pl.BlockSpec((d_att, d_model), lambda i: (0, 0)),
                  pl.BlockSpec((d_conv, d_model), lambda i: (d_att // d_conv, 0)),
                  pl.BlockSpec((1, d_model), lambda i: (0, 0))],
        out_specs=row(d_model),
        out_shape=jax.ShapeDtypeStruct((rows, d_model), jnp.float32),
        compiler_params=pltpu.CompilerParams(
            dimension_semantics=("arbitrary",), vmem_limit_bytes=VMEM_LIMIT_BYTES),
        name="out_prompt",
    )(x2d, attg, convg, w_out_bf16, w_out_bf16, final_g)


def _weight_specs(d_conv, d_model, d_att):
    full = lambda r, c: pl.BlockSpec((r, c), lambda i: (0, 0))
    return [
        full(CONV_W, d_conv), full(1, d_conv), full(1, d_conv), full(1, d_conv),
        full(d_conv, d_conv), full(1, d_conv),
        pl.BlockSpec((d_att, d_model), lambda i: (0, 0)),
        pl.BlockSpec((d_conv, d_model), lambda i: (d_att // d_conv, 0)),
        full(1, d_model),
    ]


def _mix_sample_kernel(x_ref, att_ref, sga_ref, u_ref, st_ref, sgc_ref, dw_ref, db_ref,
                       ln_g_ref, ln_b_ref, wp_ref, bp_ref, wo_att_ref, wo_conv_ref, fg_ref,
                       y_ref, st_out_ref, ext_ref, *, n_new):
    g = st_ref.shape[0]
    d_conv = st_ref.shape[2]
    hist = CONV_W - 1
    ext_ref[:, 0:hist, :] = st_ref[...]
    ext_ref[:, hist:hist + n_new, :] = u_ref[...].reshape(g, n_new, d_conv)
    c = jnp.broadcast_to(db_ref[...].reshape(1, 1, d_conv), (g, n_new, d_conv))
    for w in range(CONV_W):
        c = c + ext_ref[:, w:w + n_new, :] * dw_ref[w:w + 1, :].reshape(1, 1, d_conv)
    st_out_ref[...] = ext_ref[:, n_new:n_new + hist, :]
    act = _layer_norm_silu(c.reshape(g * n_new, d_conv), ln_g_ref, ln_b_ref)
    conv = _dot(act, wp_ref[...]) + bp_ref[...]
    y_ref[...] = _out_tail(x_ref[...], (att_ref[...] * sga_ref[...]).astype(jnp.bfloat16),
                           (conv * sgc_ref[...]).astype(jnp.bfloat16), wo_att_ref, wo_conv_ref,
                           fg_ref)


def _mix_sample(x2d, att, sga, u, state, sgc, weights, *, n_new, seqs_per_block):
    rows, d_model = x2d.shape
    d_att = att.shape[1]
    n_seq, hist, d_conv = state.shape
    block_rows = seqs_per_block * n_new
    row = lambda c: pl.BlockSpec((block_rows, c), lambda i: (i, 0))
    st_spec = pl.BlockSpec((seqs_per_block, hist, d_conv), lambda i: (i, 0, 0))
    ext_rows = -(-(hist + n_new) // SUBLANES) * SUBLANES
    kernel = functools.partial(_mix_sample_kernel, n_new=n_new)
    w_out = weights[6]
    return pl.pallas_call(
        kernel,
        grid=(n_seq // seqs_per_block,),
        in_specs=[row(d_model), row(d_att), row(d_att), row(d_conv), st_spec, row(d_conv)]
        + _weight_specs(d_conv, d_model, d_att),
        out_specs=[row(d_model), st_spec],
        out_shape=[jax.ShapeDtypeStruct((rows, d_model), jnp.float32),
                   jax.ShapeDtypeStruct(state.shape, jnp.float32)],
        scratch_shapes=[pltpu.VMEM((seqs_per_block, ext_rows, d_conv), jnp.float32)],
        compiler_params=pltpu.CompilerParams(
            dimension_semantics=("arbitrary",), vmem_limit_bytes=VMEM_LIMIT_BYTES),
        name="mix_sample",
    )(x2d, att, sga, u, state, sgc, *weights[:6], w_out, w_out, weights[7])


def kernel(x_prompt, x_sample, cache_k, cache_v, state_conv, page_table, norm_g, w_in, lambda_q1,
           lambda_k1, lambda_q2, lambda_k2, subln_g, dw_w, dw_b, conv_ln_g, conv_ln_b, w_pw2,
           b_pw2, w_out, final_norm_g):
    batch, seq, d_model = x_prompt.shape
    n_seq, n_new, _ = x_sample.shape
    depth, _, _, n_heads, v_dim = cache_k.shape
    assert depth == 1 and n_heads == N_HEADS and v_dim == V_DIM
    d_conv = dw_w.shape[2]
    bf16 = jnp.bfloat16

    norm_g2 = norm_g[0].reshape(1, d_model)
    w_in_b = w_in[0].astype(bf16)
    lams = [t[0].reshape(1, HEAD_DIM) for t in (lambda_q1, lambda_k1, lambda_q2, lambda_k2)]
    subln = subln_g[0].reshape(1, V_DIM)
    mix_w = [dw_w[0], dw_b[0].reshape(1, d_conv), conv_ln_g[0].reshape(1, d_conv),
             conv_ln_b[0].reshape(1, d_conv), w_pw2[0].astype(bf16), b_pw2[0].reshape(1, d_conv),
             w_out[0].astype(bf16), final_norm_g.reshape(1, d_model)]

    xp = x_prompt.reshape(batch * seq, d_model)
    xs = x_sample.reshape(n_seq * n_new, d_model)
    q, kb, vt, sga, u, sgc, conv_p, k, v = _in_proj_prompt(
        xp, norm_g2, w_in_b, batch=batch, seq=seq, block_rows=512)
    qs, ks, vs, sgas, us, sgcs = _in_proj_sample(xs, norm_g2, w_in_b, block_rows=512)
    att_s, convg = _sample_attn(page_table, qs, ks, vs, lams, subln, cache_k, cache_v, u, sgc,
                                mix_w[:6], n_new=n_new, prompt_seq=seq)
    attg = _prompt_attn(q, kb, vt, sga, lams, subln, batch=batch, seq=seq, block=512)
    yp = _out_prompt(xp, attg, convg, mix_w[6], mix_w[7], block_rows=512)
    ys, conv_s = _mix_sample(xs, att_s, sgas, us, state_conv[0], sgcs, mix_w, n_new=n_new,
                             seqs_per_block=32)
    kv_shape = (1, batch, seq, N_HEADS, V_DIM)
    kvs_shape = (1, n_seq, n_new, N_HEADS, V_DIM)

    return (yp.reshape(batch, seq, d_model), ys.reshape(n_seq, n_new, d_model),
            k.reshape(kv_shape), v.reshape(kv_shape), conv_p[None],
            ks.reshape(kvs_shape), vs.reshape(kvs_shape), conv_s[None])
```

```python
import functools
import math

import jax
import jax.numpy as jnp
from jax import lax
from jax.experimental import pallas as pl
from jax.experimental.pallas import tpu as pltpu

N_HEADS = 4
HEAD_DIM = 64
V_DIM = 2 * HEAD_DIM
CONV_W = 31
EPS = 1e-5
LAM_INIT = 0.8 - 0.6 * math.exp(-0.3 * 0)

Q_SCALE = HEAD_DIM ** -0.5 * math.log2(math.e)

SUBLANES = 8
VMEM_LIMIT_BYTES = 56 * 1024 * 1024

KEY_SUB = 512
ATTN_UNROLL = 4
PAGE_SLOTS = 3
CONV_HALO = 32
CONV_CHUNK = 64
NEG_INF = float("-inf")


def _silu(x):
    return x * jax.nn.sigmoid(x)


def _dot(a, b):
    return jnp.dot(a, b, preferred_element_type=jnp.float32)


def _dot_nt(a, b):
    return lax.dot_general(a, b, (((1,), (1,)), ((), ())), preferred_element_type=jnp.float32)


def _normed_input(x_ref, g_ref):
    x = x_ref[...]
    h = x * lax.rsqrt(jnp.mean(x * x, axis=-1, keepdims=True) + EPS) * g_ref[...]
    return h.astype(jnp.bfloat16)


def _layer_norm_silu(c, ln_g_ref, ln_b_ref):
    mu = jnp.mean(c, axis=-1, keepdims=True)
    d = c - mu
    var = jnp.mean(d * d, axis=-1, keepdims=True)
    ln = d * lax.rsqrt(var + EPS) * ln_g_ref[...] + ln_b_ref[...]
    return _silu(ln).astype(jnp.bfloat16)


def _in_proj_sample_kernel(x_ref, g_ref, w_ref, q_ref, k_ref, v_ref, sga_ref, u_ref, sgc_ref):
    hb = _normed_input(x_ref, g_ref)
    c = q_ref.shape[-1]

    def col(i):
        return _dot(hb, w_ref[:, i * c:(i + 1) * c])

    q_ref[...] = col(0) * Q_SCALE
    k = col(1)
    v = col(2)
    for h in range(N_HEADS):
        hs = slice(h * V_DIM, (h + 1) * V_DIM)
        k_ref[:, h, :] = k[:, hs]
        v_ref[:, h, :] = v[:, hs]
    sga_ref[...] = _silu(col(3))
    u_ref[...] = col(4) * jax.nn.sigmoid(col(5))
    sgc_ref[...] = _silu(col(6))


def _in_proj_sample(x2d, norm_g, w_in_bf16, *, block_rows):
    rows, d_model = x2d.shape
    c = w_in_bf16.shape[1] // 7
    row_spec = pl.BlockSpec((block_rows, c), lambda i: (i, 0))
    kv_spec = pl.BlockSpec((block_rows, N_HEADS, V_DIM), lambda i: (i, 0, 0))
    f32_rows = jax.ShapeDtypeStruct((rows, c), jnp.float32)
    kv_rows = jax.ShapeDtypeStruct((rows, N_HEADS, V_DIM), jnp.float32)
    return pl.pallas_call(
        _in_proj_sample_kernel,
        grid=(rows // block_rows,),
        in_specs=[
            pl.BlockSpec((block_rows, d_model), lambda i: (i, 0)),
            pl.BlockSpec((1, d_model), lambda i: (0, 0)),
            pl.BlockSpec(w_in_bf16.shape, lambda i: (0, 0)),
        ],
        out_specs=[row_spec, kv_spec, kv_spec, row_spec, row_spec, row_spec],
        out_shape=[f32_rows, kv_rows, kv_rows, f32_rows, f32_rows, f32_rows],
        compiler_params=pltpu.CompilerParams(
            dimension_semantics=("arbitrary",), vmem_limit_bytes=VMEM_LIMIT_BYTES),
        name="in_proj_sample",
    )(x2d, norm_g, w_in_bf16)


def _in_proj_prompt_kernel(x_ref, g_ref, w_ref, q_ref, kb_ref, vt_ref, sga_ref, u_ref, sgc_ref,
                           cstate_ref, k_hbm, v_hbm, kst_ref, vst_ref, sem):
    i = pl.program_id(0)
    rows = x_ref.shape[0]
    hb = _normed_input(x_ref, g_ref)
    c = q_ref.shape[-1]

    def col(j):
        return _dot(hb, w_ref[:, j * c:(j + 1) * c])

    def kv_copies(step):
        dst_rows = pl.ds(step * rows, rows)
        copies = []
        for h in range(N_HEADS):
            copies.append(pltpu.make_async_copy(kst_ref.at[h], k_hbm.at[dst_rows, h, :], sem.at[0]))
            copies.append(pltpu.make_async_copy(vst_ref.at[h], v_hbm.at[dst_rows, h, :], sem.at[1]))
        return copies

    q_ref[...] = (col(0) * Q_SCALE).astype(jnp.bfloat16)
    k = col(1)
    v = col(2)
    kb_ref[...] = k.astype(jnp.bfloat16)
    vt_ref[...] = v.T.astype(jnp.bfloat16)

    @pl.when(i > 0)
    def _():
        for cp in kv_copies(i - 1):
            cp.wait()

    for h in range(N_HEADS):
        kst_ref[h] = k[:, h * V_DIM:(h + 1) * V_DIM]
        vst_ref[h] = v[:, h * V_DIM:(h + 1) * V_DIM]
    for cp in kv_copies(i):
        cp.start()

    sga_ref[...] = _silu(col(3)).astype(jnp.bfloat16)
    u_ref[...] = col(4) * jax.nn.sigmoid(col(5))
    sgc_ref[...] = _silu(col(6)).astype(jnp.bfloat16)
    cstate_ref[0] = u_ref[rows - (CONV_W - 1):rows, :]

    @pl.when(i == pl.num_programs(0) - 1)
    def _():
        for cp in kv_copies(i):
            cp.wait()


def _in_proj_prompt(x2d, norm_g, w_in_bf16, *, batch, seq, block_rows):
    rows, d_model = x2d.shape
    c = w_in_bf16.shape[1] // 7
    tiles_per_seq = seq // block_rows
    hist = CONV_W - 1
    full = lambda a: pl.BlockSpec(a.shape, lambda i: (0,) * a.ndim)
    row_spec = pl.BlockSpec((block_rows, c), lambda i: (i, 0))
    bf16_rows = jax.ShapeDtypeStruct((rows, c), jnp.bfloat16)
    kv_rows = jax.ShapeDtypeStruct((rows, N_HEADS, V_DIM), jnp.float32)
    any_spec = pl.BlockSpec(memory_space=pl.ANY)
    return pl.pallas_call(
        _in_proj_prompt_kernel,
        grid=(rows // block_rows,),
        in_specs=[pl.BlockSpec((block_rows, d_model), lambda i: (i, 0)), full(norm_g),
                  full(w_in_bf16)],
        out_specs=[row_spec, row_spec, pl.BlockSpec((c, block_rows), lambda i: (0, i)), row_spec,
                   row_spec, row_spec,
                   pl.BlockSpec((1, hist, c), lambda i: (i // tiles_per_seq, 0, 0)),
                   any_spec, any_spec],
        out_shape=[bf16_rows, bf16_rows, jax.ShapeDtypeStruct((c, rows), jnp.bfloat16), bf16_rows,
                   jax.ShapeDtypeStruct((rows, c), jnp.float32), bf16_rows,
                   jax.ShapeDtypeStruct((batch, hist, c), jnp.float32), kv_rows, kv_rows],
        scratch_shapes=[
            pltpu.VMEM((N_HEADS, block_rows, V_DIM), jnp.float32),
            pltpu.VMEM((N_HEADS, block_rows, V_DIM), jnp.float32),
            pltpu.SemaphoreType.DMA((2,)),
        ],
        compiler_params=pltpu.CompilerParams(
            dimension_semantics=("arbitrary",), vmem_limit_bytes=VMEM_LIMIT_BYTES),
        name="in_proj_prompt",
    )(x2d, norm_g, w_in_bf16)


def _prompt_conv_history(rows, first, ext_ref):
    @pl.when(first)
    def _():
        ext_ref[0:CONV_HALO, :] = jnp.zeros((CONV_HALO, ext_ref.shape[1]), jnp.float32)

    @pl.when(jnp.logical_not(first))
    def _():
        ext_ref[0:CONV_HALO, :] = ext_ref[rows:rows + CONV_HALO, :]


def _prompt_conv_rows(u_ref, sgc_ref, dw_ref, db_ref, ln_g_ref, ln_b_ref, wp_ref, bp_ref,
                      cg_ref, ext_ref, shift_ref):
    rows, c = u_ref.shape
    hist = CONV_W - 1
    ext_ref[CONV_HALO:, :] = u_ref[...]
    for o in range(1, SUBLANES):
        shift_ref[o - 1] = ext_ref[o:o + shift_ref.shape[1], :]

    acts = []
    for ch in range(rows // CONV_CHUNK):
        r0 = ch * CONV_CHUNK
        acc = jnp.broadcast_to(db_ref[...], (CONV_CHUNK, c))
        for w in range(CONV_W):
            start = CONV_HALO - hist + w
            o = start % SUBLANES
            a = r0 + start - o
            src = ext_ref if o == 0 else shift_ref.at[o - 1]
            acc = acc + src[a:a + CONV_CHUNK, :] * dw_ref[w:w + 1, :]
        acts.append(_layer_norm_silu(acc, ln_g_ref, ln_b_ref))
    conv = _dot(jnp.concatenate(acts, axis=0), wp_ref[...]) + bp_ref[...]
    cg_ref[...] = (conv * sgc_ref[...].astype(jnp.float32)).astype(jnp.bfloat16)


def _lam(lq1_ref, lk1_ref, lq2_ref, lk2_ref):
    s1 = jnp.sum(lq1_ref[...] * lk1_ref[...], axis=-1, keepdims=True)
    s2 = jnp.sum(lq2_ref[...] * lk2_ref[...], axis=-1, keepdims=True)
    return jnp.exp(s1) - jnp.exp(s2) + LAM_INIT


def _split_maps(q):
    lane = lax.broadcasted_iota(jnp.int32, q.shape, 1)
    zero = jnp.zeros_like(q)
    return jnp.where(lane < HEAD_DIM, q, zero), jnp.where(lane >= HEAD_DIM, q, zero)


def _prompt_attn_kernel(q_ref, k_ref, vt_ref, sga_ref, lq1_ref, lk1_ref, lq2_ref, lk2_ref, g_ref,
                        o_ref, m_ref, l_ref, acc_ref, s_ref, *, block):
    qi = pl.program_id(2)
    qs = _split_maps(q_ref[...])

    m_ref[...] = jnp.full_like(m_ref, NEG_INF)
    l_ref[...] = jnp.zeros_like(l_ref)
    acc_ref[...] = jnp.zeros_like(acc_ref)

    n_sub = block // KEY_SUB

    def fold(x, op):
        return op(x.reshape(x.shape[0] // SUBLANES, SUBLANES, x.shape[1]), axis=0)

    def sub(c):
        return slice(c * KEY_SUB, (c + 1) * KEY_SUB)

    def scores(kc, mp):
        start = pl.multiple_of(kc * block, block)
        for c in range(n_sub):
            rows = pl.ds(start + c * KEY_SUB, KEY_SUB)
            s_ref[mp, sub(c), :] = _dot_nt(k_ref[rows, :], qs[mp])

    def mask_diagonal(mp):
        for c in range(n_sub):
            s = s_ref[mp, sub(c), :]
            key = lax.broadcasted_iota(jnp.int32, s.shape, 0) + c * KEY_SUB
            qry = lax.broadcasted_iota(jnp.int32, s.shape, 1)
            s_ref[mp, sub(c), :] = jnp.where(key <= qry, s, NEG_INF)

    def accumulate(kc, mp):
        start = pl.multiple_of(kc * block, block)
        mx = None
        for c in range(n_sub):
            part = fold(s_ref[mp, sub(c), :], jnp.max)
            mx = part if mx is None else jnp.maximum(mx, part)
        m_prev = m_ref[mp]
        m_new = jnp.maximum(m_prev, jnp.max(mx, axis=0, keepdims=True))
        alpha = jnp.exp2(m_prev - m_new)
        lsum = None
        pv = None
        for c in range(n_sub):
            p = jnp.exp2(s_ref[mp, sub(c), :] - m_new)
            part = fold(p, jnp.sum)
            lsum = part if lsum is None else lsum + part
            cols = pl.ds(start + c * KEY_SUB, KEY_SUB)
            d = _dot(vt_ref[:, cols], p.astype(jnp.bfloat16))
            pv = d if pv is None else pv + d
        l_ref[mp] = alpha * l_ref[mp] + jnp.sum(lsum, axis=0, keepdims=True)
        acc_ref[mp] = alpha * acc_ref[mp] + pv
        m_ref[mp] = m_new

    scores(0, 0)

    def pipelined(kc):
        scores(kc, 1)
        accumulate(kc, 0)
        scores(kc + 1, 0)
        accumulate(kc, 1)

    def pipelined_group(first, n):
        for j in range(n):
            pipelined(first + j)

    def unrolled(t, carry):
        pipelined_group(t * ATTN_UNROLL, ATTN_UNROLL)
        return carry

    n_unrolled = lax.shift_right_logical(qi, ATTN_UNROLL.bit_length() - 1)
    lax.fori_loop(0, n_unrolled, unrolled, 0)
    done = n_unrolled * ATTN_UNROLL
    n = ATTN_UNROLL // 2
    while n >= 1:
        @pl.when(lax.bitwise_and(qi, n) != 0)
        def _(n=n, first=done):
            pipelined_group(first, n)
        done = done + lax.bitwise_and(qi, n)
        n //= 2

    scores(qi, 1)
    for mp in range(2):
        mask_diagonal(mp)
        accumulate(qi, mp)

    lam = _lam(lq1_ref, lk1_ref, lq2_ref, lk2_ref)
    o = acc_ref[0] / l_ref[0] - lam * (acc_ref[1] / l_ref[1])
    y = o * lax.rsqrt(jnp.mean(o * o, axis=0, keepdims=True) + EPS)
    att = (y * g_ref[...] * (1.0 - LAM_INIT)).T
    o_ref[...] = (att * sga_ref[...].astype(jnp.float32)).astype(o_ref.dtype)


def _prompt_attn(q, kb, vt, sga, lams, subln_g, *, batch, seq, block):
    rows, d_att = q.shape
    nq = seq // block
    vec = lambda n: pl.BlockSpec((1, n), lambda b, h, i: (0, 0))
    q_spec = pl.BlockSpec((block, V_DIM), lambda b, h, i: (b * nq + i, h))
    kernel = functools.partial(_prompt_attn_kernel, block=block)
    return pl.pallas_call(
        kernel,
        grid=(batch, N_HEADS, nq),
        in_specs=[
            q_spec,
            pl.BlockSpec((seq, V_DIM), lambda b, h, i: (b, h)),
            pl.BlockSpec((V_DIM, seq), lambda b, h, i: (h, b)),
            q_spec,
            vec(HEAD_DIM), vec(HEAD_DIM), vec(HEAD_DIM), vec(HEAD_DIM),
            pl.BlockSpec((V_DIM, 1), lambda b, h, i: (0, 0)),
        ],
        out_specs=q_spec,
        out_shape=jax.ShapeDtypeStruct((rows, d_att), jnp.bfloat16),
        scratch_shapes=[
            pltpu.VMEM((2, 1, block), jnp.float32),
            pltpu.VMEM((2, 1, block), jnp.float32),
            pltpu.VMEM((2, V_DIM, block), jnp.float32),
            pltpu.VMEM((2, block, block), jnp.float32),
        ],
        compiler_params=pltpu.CompilerParams(
            dimension_semantics=("arbitrary", "arbitrary", "arbitrary"),
            vmem_limit_bytes=VMEM_LIMIT_BYTES),
        name="prompt_attn",
    )(q, kb, vt, sga, *lams, subln_g.reshape(V_DIM, 1))


def _sample_attn_kernel(pt_ref, q_ref, kn_ref, vn_ref, lq1_ref, lk1_ref, lq2_ref, lk2_ref, g_ref,
                        u_ref, sgc_ref, dw_ref, db_ref, ln_g_ref, ln_b_ref, wp_ref, bp_ref,
                        ck_hbm, cv_hbm, o_ref, cg_ref, kbuf, vbuf, ext_ref, shift_ref, sem, *,
                        n_pages, page, conv_tiles_per_seq, sample_seqs):
    s_idx = pl.program_id(0)
    n_seq = pl.num_programs(0)
    n_slots = kbuf.shape[0]
    ahead = n_slots - 1
    slot = lax.rem(s_idx, n_slots)

    def page_copies(seq, slot_):
        copies = []
        for p in range(n_pages):
            pid = pt_ref[seq * n_pages + p]
            dst = pl.ds(p * page, page)
            for h in range(N_HEADS):
                copies.append(pltpu.make_async_copy(
                    ck_hbm.at[0, pid, :, h, :], kbuf.at[slot_, h, dst], sem.at[0, slot_]))
                copies.append(pltpu.make_async_copy(
                    cv_hbm.at[0, pid, :, h, :], vbuf.at[slot_, h, dst], sem.at[1, slot_]))
        return copies

    assert sample_seqs >= ahead

    @pl.when(s_idx == 0)
    def _():
        for j in range(ahead):
            for cp in page_copies(j, j):
                cp.start()

    @pl.when(s_idx + ahead < n_seq)
    def _():
        for cp in page_copies(s_idx + ahead, lax.rem(s_idx + ahead, n_slots)):
            cp.start()

    _prompt_conv_history(u_ref.shape[0], s_idx % conv_tiles_per_seq == 0, ext_ref)

    for cp in page_copies(s_idx, slot):
        cp.wait()

    _prompt_conv_rows(u_ref, sgc_ref, dw_ref, db_ref, ln_g_ref, ln_b_ref, wp_ref, bp_ref, cg_ref,
                      ext_ref, shift_ref)

    lam = _lam(lq1_ref, lk1_ref, lq2_ref, lk2_ref)
    t = q_ref.shape[0]
    row = lax.broadcasted_iota(jnp.int32, (t, t), 0)
    colk = lax.broadcasted_iota(jnp.int32, (t, t), 1)
    causal = jnp.concatenate([colk <= row] * 2, axis=0)
    heads = range(N_HEADS)
    qz = [jnp.concatenate(_split_maps(q_ref[:, h * V_DIM:(h + 1) * V_DIM].astype(jnp.bfloat16)),
                          axis=0) for h in heads]
    s_past = [_dot_nt(qz[h], kbuf[slot, h].astype(jnp.bfloat16)) for h in heads]
    s_new = [jnp.where(causal, _dot_nt(qz[h], kn_ref[:, h, :].astype(jnp.bfloat16)), NEG_INF)
             for h in heads]
    a_past, a_new = [], []
    for h in heads:
        m = jnp.maximum(jnp.max(s_past[h], axis=-1, keepdims=True),
                        jnp.max(s_new[h], axis=-1, keepdims=True))
        e_past = jnp.exp2(s_past[h] - m)
        e_new = jnp.exp2(s_new[h] - m)
        denom = jnp.sum(e_past, axis=-1, keepdims=True) + jnp.sum(e_new, axis=-1, keepdims=True)
        p_past = e_past / denom
        p_new = e_new / denom
        a_past.append((p_past[:t] - lam * p_past[t:]).astype(jnp.bfloat16))
        a_new.append((p_new[:t] - lam * p_new[t:]).astype(jnp.bfloat16))
    for h in heads:
        o = (_dot(a_past[h], vbuf[slot, h].astype(jnp.bfloat16))
             + _dot(a_new[h], vn_ref[:, h, :].astype(jnp.bfloat16)))
        y = o * lax.rsqrt(jnp.mean(o * o, axis=-1, keepdims=True) + EPS)
        o_ref[:, h * V_DIM:(h + 1) * V_DIM] = y * g_ref[...] * (1.0 - LAM_INIT)


def _sample_attn(page_table, q, k_new, v_new, lams, subln_g, cache_k, cache_v, u_prompt,
                 sgc_prompt, conv_w, *, n_new, prompt_seq):
    n_seq, n_pages = page_table.shape
    _, _, page, n_heads, v_dim = cache_k.shape
    past = n_pages * page
    d_att = q.shape[1]
    prompt_rows, d_conv = u_prompt.shape
    conv_rows = prompt_rows // n_seq
    assert conv_rows * n_seq == prompt_rows and conv_rows % CONV_CHUNK == 0
    assert prompt_seq % conv_rows == 0 and conv_rows >= CONV_HALO
    row_spec = pl.BlockSpec((n_new, d_att), lambda s, pt: (s, 0))
    kv_spec = pl.BlockSpec((n_new, n_heads, v_dim), lambda s, pt: (s, 0, 0))
    conv_spec = pl.BlockSpec((conv_rows, d_conv), lambda s, pt: (s, 0))
    vec = lambda n: pl.BlockSpec((1, n), lambda s, pt: (0, 0))
    full = lambda a: pl.BlockSpec(a.shape, lambda s, pt: (0,) * a.ndim)
    kernel = functools.partial(_sample_attn_kernel, n_pages=n_pages, page=page,
                               conv_tiles_per_seq=prompt_seq // conv_rows, sample_seqs=n_seq)
    return pl.pallas_call(
        kernel,
        grid_spec=pltpu.PrefetchScalarGridSpec(
            num_scalar_prefetch=1,
            grid=(n_seq,),
            in_specs=[
                row_spec, kv_spec, kv_spec,
                vec(HEAD_DIM), vec(HEAD_DIM), vec(HEAD_DIM), vec(HEAD_DIM), vec(V_DIM),
                conv_spec, conv_spec] + [full(a) for a in conv_w] + [
                pl.BlockSpec(memory_space=pl.ANY),
                pl.BlockSpec(memory_space=pl.ANY),
            ],
            out_specs=[row_spec, conv_spec],
            scratch_shapes=[
                pltpu.VMEM((PAGE_SLOTS, n_heads, past, v_dim), jnp.float32),
                pltpu.VMEM((PAGE_SLOTS, n_heads, past, v_dim), jnp.float32),
                pltpu.VMEM((CONV_HALO + conv_rows, d_conv), jnp.float32),
                pltpu.VMEM((SUBLANES - 1, CONV_HALO + conv_rows - SUBLANES, d_conv), jnp.float32),
                pltpu.SemaphoreType.DMA((2, PAGE_SLOTS)),
            ],
        ),
        out_shape=[jax.ShapeDtypeStruct(q.shape, jnp.float32),
                   jax.ShapeDtypeStruct((prompt_rows, d_conv), jnp.bfloat16)],
        compiler_params=pltpu.CompilerParams(
            dimension_semantics=("arbitrary",), vmem_limit_bytes=VMEM_LIMIT_BYTES),
        name="sample_attn",
    )(page_table.reshape(-1), q, k_new, v_new, *lams, subln_g, u_prompt, sgc_prompt, *conv_w,
      cache_k, cache_v)


def _out_tail(x, attg, convg, wo_att_ref, wo_conv_ref, fg_ref):
    y = x + _dot(attg, wo_att_ref[...]) + _dot(convg, wo_conv_ref[...])
    return y * lax.rsqrt(jnp.mean(y * y, axis=-1, keepdims=True) + EPS) * fg_ref[...]


def _weight_specs(d_conv, d_model, d_att):
    full = lambda r, c: pl.BlockSpec((r, c), lambda i: (0, 0))
    return [
        full(CONV_W, d_conv), full(1, d_conv), full(1, d_conv), full(1, d_conv),
        full(d_conv, d_conv), full(1, d_conv),
        pl.BlockSpec((d_att, d_model), lambda i: (0, 0)),
        pl.BlockSpec((d_conv, d_model), lambda i: (d_att // d_conv, 0)),
        full(1, d_model),
    ]


def _out_prompt_kernel(x_ref, attg_ref, cg_ref, wo_att_ref, wo_conv_ref, fg_ref, y_ref):
    y_ref[...] = _out_tail(x_ref[...], attg_ref[...], cg_ref[...], wo_att_ref, wo_conv_ref, fg_ref)


def _out_prompt(x2d, attg, convg, w_out_bf16, final_g, *, block_rows):
    rows, d_model = x2d.shape
    d_att = attg.shape[1]
    d_conv = convg.shape[1]
    row = lambda c: pl.BlockSpec((block_rows, c), lambda i: (i, 0))
    return pl.pallas_call(
        _out_prompt_kernel,
        grid=(rows // block_rows,),
        in_specs=[row(d_model), row(d_att), row(d_conv),
                  pl.BlockSpec((d_att, d_model), lambda i: (0, 0)),
                  pl.BlockSpec((d_conv, d_model), lambda i: (d_att // d_conv, 0)),
                  pl.BlockSpec((1, d_model), lambda i: (0, 0))],
        out_specs=row(d_model),
        out_shape=jax.ShapeDtypeStruct((rows, d_model), jnp.float32),
        compiler_params=pltpu.CompilerParams(
            dimension_semantics=("arbitrary",), vmem_limit_bytes=VMEM_LIMIT_BYTES),
        name="out_prompt",
    )(x2d, attg, convg, w_out_bf16, w_out_bf16, final_g)


def _mix_sample_kernel(x_ref, att_ref, sga_ref, u_ref, st_ref, sgc_ref, dw_ref, db_ref,
                       ln_g_ref, ln_b_ref, wp_ref, bp_ref, wo_att_ref, wo_conv_ref, fg_ref,
                       y_ref, st_out_ref, ext_ref, *, n_new):
    g = st_ref.shape[0]
    d_conv = st_ref.shape[2]
    hist = CONV_W - 1
    ext_ref[:, 0:hist, :] = st_ref[...]
    ext_ref[:, hist:hist + n_new, :] = u_ref[...].reshape(g, n_new, d_conv)
    c = jnp.broadcast_to(db_ref[...].reshape(1, 1, d_conv), (g, n_new, d_conv))
    for w in range(CONV_W):
        c = c + ext_ref[:, w:w + n_new, :] * dw_ref[w:w + 1, :].reshape(1, 1, d_conv)
    st_out_ref[...] = ext_ref[:, n_new:n_new + hist, :]
    act = _layer_norm_silu(c.reshape(g * n_new, d_conv), ln_g_ref, ln_b_ref)
    conv = _dot(act, wp_ref[...]) + bp_ref[...]
    y_ref[...] = _out_tail(x_ref[...], (att_ref[...] * sga_ref[...]).astype(jnp.bfloat16),
                           (conv * sgc_ref[...]).astype(jnp.bfloat16), wo_att_ref, wo_conv_ref,
                           fg_ref)


def _mix_sample(x2d, att, sga, u, state, sgc, weights, *, n_new, seqs_per_block):
    rows, d_model = x2d.shape
    d_att = att.shape[1]
    n_seq, hist, d_conv = state.shape
    block_rows = seqs_per_block * n_new
    row = lambda c: pl.BlockSpec((block_rows, c), lambda i: (i, 0))
    st_spec = pl.BlockSpec((seqs_per_block, hist, d_conv), lambda i: (i, 0, 0))
    ext_rows = -(-(hist + n_new) // SUBLANES) * SUBLANES
    kernel = functools.partial(_mix_sample_kernel, n_new=n_new)
    w_out = weights[6]
    return pl.pallas_call(
        kernel,
        grid=(n_seq // seqs_per_block,),
        in_specs=[row(d_model), row(d_att), row(d_att), row(d_conv), st_spec, row(d_conv)]
        + _weight_specs(d_conv, d_model, d_att),
        out_specs=[row(d_model), st_spec],
        out_shape=[jax.ShapeDtypeStruct((rows, d_model), jnp.float32),
                   jax.ShapeDtypeStruct(state.shape, jnp.float32)],
        scratch_shapes=[pltpu.VMEM((seqs_per_block, ext_rows, d_conv), jnp.float32)],
        compiler_params=pltpu.CompilerParams(
            dimension_semantics=("arbitrary",), vmem_limit_bytes=VMEM_LIMIT_BYTES),
        name="mix_sample",
    )(x2d, att, sga, u, state, sgc, *weights[:6], w_out, w_out, weights[7])


def kernel(x_prompt, x_sample, cache_k, cache_v, state_conv, page_table, norm_g, w_in, lambda_q1,
           lambda_k1, lambda_q2, lambda_k2, subln_g, dw_w, dw_b, conv_ln_g, conv_ln_b, w_pw2,
           b_pw2, w_out, final_norm_g):
    batch, seq, d_model = x_prompt.shape
    n_seq, n_new, _ = x_sample.shape
    depth, _, _, n_heads, v_dim = cache_k.shape
    assert depth == 1 and n_heads == N_HEADS and v_dim == V_DIM
    d_conv = dw_w.shape[2]
    bf16 = jnp.bfloat16

    norm_g2 = norm_g[0].reshape(1, d_model)
    w_in_b = w_in[0].astype(bf16)
    lams = [t[0].reshape(1, HEAD_DIM) for t in (lambda_q1, lambda_k1, lambda_q2, lambda_k2)]
    subln = subln_g[0].reshape(1, V_DIM)
    mix_w = [dw_w[0], dw_b[0].reshape(1, d_conv), conv_ln_g[0].reshape(1, d_conv),
             conv_ln_b[0].reshape(1, d_conv), w_pw2[0].astype(bf16), b_pw2[0].reshape(1, d_conv),
             w_out[0].astype(bf16), final_norm_g.reshape(1, d_model)]

    xp = x_prompt.reshape(batch * seq, d_model)
    xs = x_sample.reshape(n_seq * n_new, d_model)
    q, kb, vt, sga, u, sgc, conv_p, k, v = _in_proj_prompt(
        xp, norm_g2, w_in_b, batch=batch, seq=seq, block_rows=512)
    qs, ks, vs, sgas, us, sgcs = _in_proj_sample(xs, norm_g2, w_in_b, block_rows=512)
    att_s, convg = _sample_attn(page_table, qs, ks, vs, lams, subln, cache_k, cache_v, u, sgc,
                                mix_w[:6], n_new=n_new, prompt_seq=seq)
    attg = _prompt_attn(q, kb, vt, sga, lams, subln, batch=batch, seq=seq, block=512)
    yp = _out_prompt(xp, attg, convg, mix_w[6], mix_w[7], block_rows=1024)
    ys, conv_s = _mix_sample(xs, att_s, sgas, us, state_conv[0], sgcs, mix_w, n_new=n_new,
                             seqs_per_block=32)
    kv_shape = (1, batch, seq, N_HEADS, V_DIM)
    kvs_shape = (1, n_seq, n_new, N_HEADS, V_DIM)

    return (yp.reshape(batch, seq, d_model), ys.reshape(n_seq, n_new, d_model),
            k.reshape(kv_shape), v.reshape(kv_shape), conv_p[None],
            ks.reshape(kvs_shape), vs.reshape(kvs_shape), conv_s[None])
```

```python
import functools
import math

import jax
import jax.numpy as jnp
from jax import lax
from jax.experimental import pallas as pl
from jax.experimental.pallas import tpu as pltpu

N_HEADS = 4
HEAD_DIM = 64
V_DIM = 2 * HEAD_DIM
CONV_W = 31
EPS = 1e-5
LAM_INIT = 0.8 - 0.6 * math.exp(-0.3 * 0)

Q_SCALE = HEAD_DIM ** -0.5 * math.log2(math.e)

SUBLANES = 8
VMEM_LIMIT_BYTES = 56 * 1024 * 1024

KEY_SUB = 512
ATTN_UNROLL = 4
ONES_ROWS = 16
PAGE_SLOTS = 3
CONV_HALO = 32
CONV_CHUNK = 64
NEG_INF = float("-inf")


def _silu(x):
    return x * jax.nn.sigmoid(x)


def _dot(a, b):
    return jnp.dot(a, b, preferred_element_type=jnp.float32)


def _dot_nt(a, b):
    return lax.dot_general(a, b, (((1,), (1,)), ((), ())), preferred_element_type=jnp.float32)


def _normed_input(x_ref, g_ref):
    x = x_ref[...]
    h = x * lax.rsqrt(jnp.mean(x * x, axis=-1, keepdims=True) + EPS) * g_ref[...]
    return h.astype(jnp.bfloat16)


def _layer_norm_silu(c, ln_g_ref, ln_b_ref):
    mu = jnp.mean(c, axis=-1, keepdims=True)
    d = c - mu
    var = jnp.mean(d * d, axis=-1, keepdims=True)
    ln = d * lax.rsqrt(var + EPS) * ln_g_ref[...] + ln_b_ref[...]
    return _silu(ln).astype(jnp.bfloat16)


def _in_proj_sample_kernel(x_ref, g_ref, w_ref, q_ref, k_ref, v_ref, sga_ref, u_ref, sgc_ref):
    hb = _normed_input(x_ref, g_ref)
    c = q_ref.shape[-1]

    def col(i):
        return _dot(hb, w_ref[:, i * c:(i + 1) * c])

    q_ref[...] = col(0) * Q_SCALE
    k = col(1)
    v = col(2)
    for h in range(N_HEADS):
        hs = slice(h * V_DIM, (h + 1) * V_DIM)
        k_ref[:, h, :] = k[:, hs]
        v_ref[:, h, :] = v[:, hs]
    sga_ref[...] = _silu(col(3))
    u_ref[...] = col(4) * jax.nn.sigmoid(col(5))
    sgc_ref[...] = _silu(col(6))


def _in_proj_sample(x2d, norm_g, w_in_bf16, *, block_rows):
    rows, d_model = x2d.shape
    c = w_in_bf16.shape[1] // 7
    row_spec = pl.BlockSpec((block_rows, c), lambda i: (i, 0))
    kv_spec = pl.BlockSpec((block_rows, N_HEADS, V_DIM), lambda i: (i, 0, 0))
    f32_rows = jax.ShapeDtypeStruct((rows, c), jnp.float32)
    kv_rows = jax.ShapeDtypeStruct((rows, N_HEADS, V_DIM), jnp.float32)
    return pl.pallas_call(
        _in_proj_sample_kernel,
        grid=(rows // block_rows,),
        in_specs=[
            pl.BlockSpec((block_rows, d_model), lambda i: (i, 0)),
            pl.BlockSpec((1, d_model), lambda i: (0, 0)),
            pl.BlockSpec(w_in_bf16.shape, lambda i: (0, 0)),
        ],
        out_specs=[row_spec, kv_spec, kv_spec, row_spec, row_spec, row_spec],
        out_shape=[f32_rows, kv_rows, kv_rows, f32_rows, f32_rows, f32_rows],
        compiler_params=pltpu.CompilerParams(
            dimension_semantics=("arbitrary",), vmem_limit_bytes=VMEM_LIMIT_BYTES),
        name="in_proj_sample",
    )(x2d, norm_g, w_in_bf16)


def _in_proj_prompt_kernel(x_ref, g_ref, w_ref, q_ref, kb_ref, vt_ref, sga_ref, u_ref, sgc_ref,
                           cstate_ref, k_hbm, v_hbm, kst_ref, vst_ref, sem):
    i = pl.program_id(0)
    rows = x_ref.shape[0]
    hb = _normed_input(x_ref, g_ref)
    c = q_ref.shape[-1]

    def col(j):
        return _dot(hb, w_ref[:, j * c:(j + 1) * c])

    def kv_copies(step):
        dst_rows = pl.ds(step * rows, rows)
        copies = []
        for h in range(N_HEADS):
            copies.append(pltpu.make_async_copy(kst_ref.at[h], k_hbm.at[dst_rows, h, :], sem.at[0]))
            copies.append(pltpu.make_async_copy(vst_ref.at[h], v_hbm.at[dst_rows, h, :], sem.at[1]))
        return copies

    q_ref[...] = (col(0) * Q_SCALE).astype(jnp.bfloat16)
    k = col(1)
    v = col(2)
    kb_ref[...] = k.astype(jnp.bfloat16)
    vt_ref[...] = v.T.astype(jnp.bfloat16)

    @pl.when(i > 0)
    def _():
        for cp in kv_copies(i - 1):
            cp.wait()

    for h in range(N_HEADS):
        kst_ref[h] = k[:, h * V_DIM:(h + 1) * V_DIM]
        vst_ref[h] = v[:, h * V_DIM:(h + 1) * V_DIM]
    for cp in kv_copies(i):
        cp.start()

    sga_ref[...] = _silu(col(3)).astype(jnp.bfloat16)
    u_ref[...] = col(4) * jax.nn.sigmoid(col(5))
    sgc_ref[...] = _silu(col(6)).astype(jnp.bfloat16)
    cstate_ref[0] = u_ref[rows - (CONV_W - 1):rows, :]

    @pl.when(i == pl.num_programs(0) - 1)
    def _():
        for cp in kv_copies(i):
            cp.wait()


def _in_proj_prompt(x2d, norm_g, w_in_bf16, *, batch, seq, block_rows):
    rows, d_model = x2d.shape
    c = w_in_bf16.shape[1] // 7
    tiles_per_seq = seq // block_rows
    hist = CONV_W - 1
    full = lambda a: pl.BlockSpec(a.shape, lambda i: (0,) * a.ndim)
    row_spec = pl.BlockSpec((block_rows, c), lambda i: (i, 0))
    bf16_rows = jax.ShapeDtypeStruct((rows, c), jnp.bfloat16)
    kv_rows = jax.ShapeDtypeStruct((rows, N_HEADS, V_DIM), jnp.float32)
    any_spec = pl.BlockSpec(memory_space=pl.ANY)
    return pl.pallas_call(
        _in_proj_prompt_kernel,
        grid=(rows // block_rows,),
        in_specs=[pl.BlockSpec((block_rows, d_model), lambda i: (i, 0)), full(norm_g),
                  full(w_in_bf16)],
        out_specs=[row_spec, row_spec, pl.BlockSpec((c, block_rows), lambda i: (0, i)), row_spec,
                   row_spec, row_spec,
                   pl.BlockSpec((1, hist, c), lambda i: (i // tiles_per_seq, 0, 0)),
                   any_spec, any_spec],
        out_shape=[bf16_rows, bf16_rows, jax.ShapeDtypeStruct((c, rows), jnp.bfloat16), bf16_rows,
                   jax.ShapeDtypeStruct((rows, c), jnp.float32), bf16_rows,
                   jax.ShapeDtypeStruct((batch, hist, c), jnp.float32), kv_rows, kv_rows],
        scratch_shapes=[
            pltpu.VMEM((N_HEADS, block_rows, V_DIM), jnp.float32),
            pltpu.VMEM((N_HEADS, block_rows, V_DIM), jnp.float32),
            pltpu.SemaphoreType.DMA((2,)),
        ],
        compiler_params=pltpu.CompilerParams(
            dimension_semantics=("arbitrary",), vmem_limit_bytes=VMEM_LIMIT_BYTES),
        name="in_proj_prompt",
    )(x2d, norm_g, w_in_bf16)


def _prompt_conv_history(rows, first, ext_ref):
    @pl.when(first)
    def _():
        ext_ref[0:CONV_HALO, :] = jnp.zeros((CONV_HALO, ext_ref.shape[1]), jnp.float32)

    @pl.when(jnp.logical_not(first))
    def _():
        ext_ref[0:CONV_HALO, :] = ext_ref[rows:rows + CONV_HALO, :]


def _prompt_conv_rows(u_ref, sgc_ref, dw_ref, db_ref, ln_g_ref, ln_b_ref, wp_ref, bp_ref,
                      cg_ref, ext_ref, shift_ref):
    rows, c = u_ref.shape
    hist = CONV_W - 1
    ext_ref[CONV_HALO:, :] = u_ref[...]
    for o in range(1, SUBLANES):
        shift_ref[o - 1] = ext_ref[o:o + shift_ref.shape[1], :]

    acts = []
    for ch in range(rows // CONV_CHUNK):
        r0 = ch * CONV_CHUNK
        acc = jnp.broadcast_to(db_ref[...], (CONV_CHUNK, c))
        for w in range(CONV_W):
            start = CONV_HALO - hist + w
            o = start % SUBLANES
            a = r0 + start - o
            src = ext_ref if o == 0 else shift_ref.at[o - 1]
            acc = acc + src[a:a + CONV_CHUNK, :] * dw_ref[w:w + 1, :]
        acts.append(_layer_norm_silu(acc, ln_g_ref, ln_b_ref))
    conv = _dot(jnp.concatenate(acts, axis=0), wp_ref[...]) + bp_ref[...]
    cg_ref[...] = (conv * sgc_ref[...].astype(jnp.float32)).astype(jnp.bfloat16)


def _lam(lq1_ref, lk1_ref, lq2_ref, lk2_ref):
    s1 = jnp.sum(lq1_ref[...] * lk1_ref[...], axis=-1, keepdims=True)
    s2 = jnp.sum(lq2_ref[...] * lk2_ref[...], axis=-1, keepdims=True)
    return jnp.exp(s1) - jnp.exp(s2) + LAM_INIT


def _split_maps(q):
    lane = lax.broadcasted_iota(jnp.int32, q.shape, 1)
    zero = jnp.zeros_like(q)
    return jnp.where(lane < HEAD_DIM, q, zero), jnp.where(lane >= HEAD_DIM, q, zero)


def _prompt_attn_kernel(q_ref, k_ref, vt_ref, sga_ref, lq1_ref, lk1_ref, lq2_ref, lk2_ref, g_ref,
                        o_ref, m_ref, acc_ref, s_ref, *, block):
    qi = pl.program_id(2)
    qs = _split_maps(q_ref[...])

    m_ref[...] = jnp.full_like(m_ref, NEG_INF)
    acc_ref[...] = jnp.zeros_like(acc_ref)

    n_sub = block // KEY_SUB
    ones_rows = jnp.ones((ONES_ROWS, KEY_SUB), jnp.bfloat16)

    def fold(x, op):
        return op(x.reshape(x.shape[0] // SUBLANES, SUBLANES, x.shape[1]), axis=0)

    def sub(c):
        return slice(c * KEY_SUB, (c + 1) * KEY_SUB)

    def scores(kc, mp):
        start = pl.multiple_of(kc * block, block)
        for c in range(n_sub):
            rows = pl.ds(start + c * KEY_SUB, KEY_SUB)
            s_ref[mp, sub(c), :] = _dot_nt(k_ref[rows, :], qs[mp])

    def mask_diagonal(mp):
        for c in range(n_sub):
            s = s_ref[mp, sub(c), :]
            key = lax.broadcasted_iota(jnp.int32, s.shape, 0) + c * KEY_SUB
            qry = lax.broadcasted_iota(jnp.int32, s.shape, 1)
            s_ref[mp, sub(c), :] = jnp.where(key <= qry, s, NEG_INF)

    def accumulate(kc, mp):
        start = pl.multiple_of(kc * block, block)
        mx = None
        for c in range(n_sub):
            part = fold(s_ref[mp, sub(c), :], jnp.max)
            mx = part if mx is None else jnp.maximum(mx, part)
        m_prev = m_ref[mp]
        m_new = jnp.maximum(m_prev, jnp.max(mx, axis=0, keepdims=True))
        alpha = jnp.exp2(m_prev - m_new)
        pv = None
        for c in range(n_sub):
            p = jnp.exp2(s_ref[mp, sub(c), :] - m_new)
            cols = pl.ds(start + c * KEY_SUB, KEY_SUB)
            vt_ones = jnp.concatenate([vt_ref[:, cols], ones_rows], axis=0)
            d = _dot(vt_ones, p.astype(jnp.bfloat16))
            pv = d if pv is None else pv + d
        acc_ref[mp] = alpha * acc_ref[mp] + pv
        m_ref[mp] = m_new

    scores(0, 0)

    def pipelined(kc):
        scores(kc, 1)
        accumulate(kc, 0)
        scores(kc + 1, 0)
        accumulate(kc, 1)

    def pipelined_group(first, n):
        for j in range(n):
            pipelined(first + j)

    def unrolled(t, carry):
        pipelined_group(t * ATTN_UNROLL, ATTN_UNROLL)
        return carry

    n_unrolled = lax.shift_right_logical(qi, ATTN_UNROLL.bit_length() - 1)
    lax.fori_loop(0, n_unrolled, unrolled, 0)
    done = n_unrolled * ATTN_UNROLL
    n = ATTN_UNROLL // 2
    while n >= 1:
        @pl.when(lax.bitwise_and(qi, n) != 0)
        def _(n=n, first=done):
            pipelined_group(first, n)
        done = done + lax.bitwise_and(qi, n)
        n //= 2

    scores(qi, 1)
    for mp in range(2):
        mask_diagonal(mp)
        accumulate(qi, mp)

    lam = _lam(lq1_ref, lk1_ref, lq2_ref, lk2_ref)
    o = (acc_ref[0, :V_DIM] / acc_ref[0, V_DIM:V_DIM + 1]
         - lam * (acc_ref[1, :V_DIM] / acc_ref[1, V_DIM:V_DIM + 1]))
    y = o * lax.rsqrt(jnp.mean(o * o, axis=0, keepdims=True) + EPS)
    att = (y * g_ref[...] * (1.0 - LAM_INIT)).T
    o_ref[...] = (att * sga_ref[...].astype(jnp.float32)).astype(o_ref.dtype)


def _prompt_attn(q, kb, vt, sga, lams, subln_g, *, batch, seq, block):
    rows, d_att = q.shape
    nq = seq // block
    vec = lambda n: pl.BlockSpec((1, n), lambda b, h, i: (0, 0))
    q_spec = pl.BlockSpec((block, V_DIM), lambda b, h, i: (b * nq + i, h))
    kernel = functools.partial(_prompt_attn_kernel, block=block)
    return pl.pallas_call(
        kernel,
        grid=(batch, N_HEADS, nq),
        in_specs=[
            q_spec,
            pl.BlockSpec((seq, V_DIM), lambda b, h, i: (b, h)),
            pl.BlockSpec((V_DIM, seq), lambda b, h, i: (h, b)),
            q_spec,
            vec(HEAD_DIM), vec(HEAD_DIM), vec(HEAD_DIM), vec(HEAD_DIM),
            pl.BlockSpec((V_DIM, 1), lambda b, h, i: (0, 0)),
        ],
        out_specs=q_spec,
        out_shape=jax.ShapeDtypeStruct((rows, d_att), jnp.bfloat16),
        scratch_shapes=[
            pltpu.VMEM((2, 1, block), jnp.float32),
            pltpu.VMEM((2, V_DIM + ONES_ROWS, block), jnp.float32),
            pltpu.VMEM((2, block, block), jnp.float32),
        ],
        compiler_params=pltpu.CompilerParams(
            dimension_semantics=("arbitrary", "arbitrary", "arbitrary"),
            vmem_limit_bytes=VMEM_LIMIT_BYTES),
        name="prompt_attn",
    )(q, kb, vt, sga, *lams, subln_g.reshape(V_DIM, 1))


def _sample_attn_kernel(pt_ref, q_ref, kn_ref, vn_ref, lq1_ref, lk1_ref, lq2_ref, lk2_ref, g_ref,
                        u_ref, sgc_ref, dw_ref, db_ref, ln_g_ref, ln_b_ref, wp_ref, bp_ref,
                        ck_hbm, cv_hbm, o_ref, cg_ref, kbuf, vbuf, ext_ref, shift_ref, sem, *,
                        n_pages, page, conv_tiles_per_seq, sample_seqs):
    s_idx = pl.program_id(0)
    n_seq = pl.num_programs(0)
    n_slots = kbuf.shape[0]
    ahead = n_slots - 1
    slot = lax.rem(s_idx, n_slots)

    def page_copies(seq, slot_):
        copies = []
        for p in range(n_pages):
            pid = pt_ref[seq * n_pages + p]
            dst = pl.ds(p * page, page)
            for h in range(N_HEADS):
                copies.append(pltpu.make_async_copy(
                    ck_hbm.at[0, pid, :, h, :], kbuf.at[slot_, h, dst], sem.at[0, slot_]))
                copies.append(pltpu.make_async_copy(
                    cv_hbm.at[0, pid, :, h, :], vbuf.at[slot_, h, dst], sem.at[1, slot_]))
        return copies

    assert sample_seqs >= ahead

    @pl.when(s_idx == 0)
    def _():
        for j in range(ahead):
            for cp in page_copies(j, j):
                cp.start()

    @pl.when(s_idx + ahead < n_seq)
    def _():
        for cp in page_copies(s_idx + ahead, lax.rem(s_idx + ahead, n_slots)):
            cp.start()

    _prompt_conv_history(u_ref.shape[0], s_idx % conv_tiles_per_seq == 0, ext_ref)

    for cp in page_copies(s_idx, slot):
        cp.wait()

    _prompt_conv_rows(u_ref, sgc_ref, dw_ref, db_ref, ln_g_ref, ln_b_ref, wp_ref, bp_ref, cg_ref,
                      ext_ref, shift_ref)

    lam = _lam(lq1_ref, lk1_ref, lq2_ref, lk2_ref)
    t = q_ref.shape[0]
    row = lax.broadcasted_iota(jnp.int32, (t, t), 0)
    colk = lax.broadcasted_iota(jnp.int32, (t, t), 1)
    causal = jnp.concatenate([colk <= row] * 2, axis=0)
    heads = range(N_HEADS)
    qz = [jnp.concatenate(_split_maps(q_ref[:, h * V_DIM:(h + 1) * V_DIM].astype(jnp.bfloat16)),
                          axis=0) for h in heads]
    s_past = [_dot_nt(qz[h], kbuf[slot, h].astype(jnp.bfloat16)) for h in heads]
    s_new = [jnp.where(causal, _dot_nt(qz[h], kn_ref[:, h, :].astype(jnp.bfloat16)), NEG_INF)
             for h in heads]
    a_past, a_new = [], []
    for h in heads:
        m = jnp.maximum(jnp.max(s_past[h], axis=-1, keepdims=True),
                        jnp.max(s_new[h], axis=-1, keepdims=True))
        e_past = jnp.exp2(s_past[h] - m)
        e_new = jnp.exp2(s_new[h] - m)
        denom = jnp.sum(e_past, axis=-1, keepdims=True) + jnp.sum(e_new, axis=-1, keepdims=True)
        p_past = e_past / denom
        p_new = e_new / denom
        a_past.append((p_past[:t] - lam * p_past[t:]).astype(jnp.bfloat16))
        a_new.append((p_new[:t] - lam * p_new[t:]).astype(jnp.bfloat16))
    for h in heads:
        o = (_dot(a_past[h], vbuf[slot, h].astype(jnp.bfloat16))
             + _dot(a_new[h], vn_ref[:, h, :].astype(jnp.bfloat16)))
        y = o * lax.rsqrt(jnp.mean(o * o, axis=-1, keepdims=True) + EPS)
        o_ref[:, h * V_DIM:(h + 1) * V_DIM] = y * g_ref[...] * (1.0 - LAM_INIT)


def _sample_attn(page_table, q, k_new, v_new, lams, subln_g, cache_k, cache_v, u_prompt,
                 sgc_prompt, conv_w, *, n_new, prompt_seq):
    n_seq, n_pages = page_table.shape
    _, _, page, n_heads, v_dim = cache_k.shape
    past = n_pages * page
    d_att = q.shape[1]
    prompt_rows, d_conv = u_prompt.shape
    conv_rows = prompt_rows // n_seq
    assert conv_rows * n_seq == prompt_rows and conv_rows % CONV_CHUNK == 0
    assert prompt_seq % conv_rows == 0 and conv_rows >= CONV_HALO
    row_spec = pl.BlockSpec((n_new, d_att), lambda s, pt: (s, 0))
    kv_spec = pl.BlockSpec((n_new, n_heads, v_dim), lambda s, pt: (s, 0, 0))
    conv_spec = pl.BlockSpec((conv_rows, d_conv), lambda s, pt: (s, 0))
    vec = lambda n: pl.BlockSpec((1, n), lambda s, pt: (0, 0))
    full = lambda a: pl.BlockSpec(a.shape, lambda s, pt: (0,) * a.ndim)
    kernel = functools.partial(_sample_attn_kernel, n_pages=n_pages, page=page,
                               conv_tiles_per_seq=prompt_seq // conv_rows, sample_seqs=n_seq)
    return pl.pallas_call(
        kernel,
        grid_spec=pltpu.PrefetchScalarGridSpec(
            num_scalar_prefetch=1,
            grid=(n_seq,),
            in_specs=[
                row_spec, kv_spec, kv_spec,
                vec(HEAD_DIM), vec(HEAD_DIM), vec(HEAD_DIM), vec(HEAD_DIM), vec(V_DIM),
                conv_spec, conv_spec] + [full(a) for a in conv_w] + [
                pl.BlockSpec(memory_space=pl.ANY),
                pl.BlockSpec(memory_space=pl.ANY),
            ],
            out_specs=[row_spec, conv_spec],
            scratch_shapes=[
                pltpu.VMEM((PAGE_SLOTS, n_heads, past, v_dim), jnp.float32),
                pltpu.VMEM((PAGE_SLOTS, n_heads, past, v_dim), jnp.float32),
                pltpu.VMEM((CONV_HALO + conv_rows, d_conv), jnp.float32),
                pltpu.VMEM((SUBLANES - 1, CONV_HALO + conv_rows - SUBLANES, d_conv), jnp.float32),
                pltpu.SemaphoreType.DMA((2, PAGE_SLOTS)),
            ],
        ),
        out_shape=[jax.ShapeDtypeStruct(q.shape, jnp.float32),
                   jax.ShapeDtypeStruct((prompt_rows, d_conv), jnp.bfloat16)],
        compiler_params=pltpu.CompilerParams(
            dimension_semantics=("arbitrary",), vmem_limit_bytes=VMEM_LIMIT_BYTES),
        name="sample_attn",
    )(page_table.reshape(-1), q, k_new, v_new, *lams, subln_g, u_prompt, sgc_prompt, *conv_w,
      cache_k, cache_v)


def _out_tail(x, attg, convg, wo_att_ref, wo_conv_ref, fg_ref):
    y = x + _dot(attg, wo_att_ref[...]) + _dot(convg, wo_conv_ref[...])
    return y * lax.rsqrt(jnp.mean(y * y, axis=-1, keepdims=True) + EPS) * fg_ref[...]


def _weight_specs(d_conv, d_model, d_att):
    full = lambda r, c: pl.BlockSpec((r, c), lambda i: (0, 0))
    return [
        full(CONV_W, d_conv), full(1, d_conv), full(1, d_conv), full(1, d_conv),
        full(d_conv, d_conv), full(1, d_conv),
        pl.BlockSpec((d_att, d_model), lambda i: (0, 0)),
        pl.BlockSpec((d_conv, d_model), lambda i: (d_att // d_conv, 0)),
        full(1, d_model),
    ]


def _out_prompt_kernel(x_ref, attg_ref, cg_ref, wo_att_ref, wo_conv_ref, fg_ref, y_ref):
    y_ref[...] = _out_tail(x_ref[...], attg_ref[...], cg_ref[...], wo_att_ref, wo_conv_ref, fg_ref)


def _out_prompt(x2d, attg, convg, w_out_bf16, final_g, *, block_rows):
    rows, d_model = x2d.shape
    d_att = attg.shape[1]
    d_conv = convg.shape[1]
    row = lambda c: pl.BlockSpec((block_rows, c), lambda i: (i, 0))
    return pl.pallas_call(
        _out_prompt_kernel,
        grid=(rows // block_rows,),
        in_specs=[row(d_model), row(d_att), row(d_conv),
                  pl.BlockSpec((d_att, d_model), lambda i: (0, 0)),
                  pl.BlockSpec((d_conv, d_model), lambda i: (d_att // d_conv, 0)),
                  pl.BlockSpec((1, d_model), lambda i: (0, 0))],
        out_specs=row(d_model),
        out_shape=jax.ShapeDtypeStruct((rows, d_model), jnp.float32),
        compiler_params=pltpu.CompilerParams(
            dimension_semantics=("arbitrary",), vmem_limit_bytes=VMEM_LIMIT_BYTES),
        name="out_prompt",
    )(x2d, attg, convg, w_out_bf16, w_out_bf16, final_g)


def _mix_sample_kernel(x_ref, att_ref, sga_ref, u_ref, st_ref, sgc_ref, dw_ref, db_ref,
                       ln_g_ref, ln_b_ref, wp_ref, bp_ref, wo_att_ref, wo_conv_ref, fg_ref,
                       y_ref, st_out_ref, ext_ref, *, n_new):
    g = st_ref.shape[0]
    d_conv = st_ref.shape[2]
    hist = CONV_W - 1
    ext_ref[:, 0:hist, :] = st_ref[...]
    ext_ref[:, hist:hist + n_new, :] = u_ref[...].reshape(g, n_new, d_conv)
    c = jnp.broadcast_to(db_ref[...].reshape(1, 1, d_conv), (g, n_new, d_conv))
    for w in range(CONV_W):
        c = c + ext_ref[:, w:w + n_new, :] * dw_ref[w:w + 1, :].reshape(1, 1, d_conv)
    st_out_ref[...] = ext_ref[:, n_new:n_new + hist, :]
    act = _layer_norm_silu(c.reshape(g * n_new, d_conv), ln_g_ref, ln_b_ref)
    conv = _dot(act, wp_ref[...]) + bp_ref[...]
    y_ref[...] = _out_tail(x_ref[...], (att_ref[...] * sga_ref[...]).astype(jnp.bfloat16),
                           (conv * sgc_ref[...]).astype(jnp.bfloat16), wo_att_ref, wo_conv_ref,
                           fg_ref)


def _mix_sample(x2d, att, sga, u, state, sgc, weights, *, n_new, seqs_per_block):
    rows, d_model = x2d.shape
    d_att = att.shape[1]
    n_seq, hist, d_conv = state.shape
    block_rows = seqs_per_block * n_new
    row = lambda c: pl.BlockSpec((block_rows, c), lambda i: (i, 0))
    st_spec = pl.BlockSpec((seqs_per_block, hist, d_conv), lambda i: (i, 0, 0))
    ext_rows = -(-(hist + n_new) // SUBLANES) * SUBLANES
    kernel = functools.partial(_mix_sample_kernel, n_new=n_new)
    w_out = weights[6]
    return pl.pallas_call(
        kernel,
        grid=(n_seq // seqs_per_block,),
        in_specs=[row(d_model), row(d_att), row(d_att), row(d_conv), st_spec, row(d_conv)]
        + _weight_specs(d_conv, d_model, d_att),
        out_specs=[row(d_model), st_spec],
        out_shape=[jax.ShapeDtypeStruct((rows, d_model), jnp.float32),
                   jax.ShapeDtypeStruct(state.shape, jnp.float32)],
        scratch_shapes=[pltpu.VMEM((seqs_per_block, ext_rows, d_conv), jnp.float32)],
        compiler_params=pltpu.CompilerParams(
            dimension_semantics=("arbitrary",), vmem_limit_bytes=VMEM_LIMIT_BYTES),
        name="mix_sample",
    )(x2d, att, sga, u, state, sgc, *weights[:6], w_out, w_out, weights[7])


def kernel(x_prompt, x_sample, cache_k, cache_v, state_conv, page_table, norm_g, w_in, lambda_q1,
           lambda_k1, lambda_q2, lambda_k2, subln_g, dw_w, dw_b, conv_ln_g, conv_ln_b, w_pw2,
           b_pw2, w_out, final_norm_g):
    batch, seq, d_model = x_prompt.shape
    n_seq, n_new, _ = x_sample.shape
    depth, _, _, n_heads, v_dim = cache_k.shape
    assert depth == 1 and n_heads == N_HEADS and v_dim == V_DIM
    d_conv = dw_w.shape[2]
    bf16 = jnp.bfloat16

    norm_g2 = norm_g[0].reshape(1, d_model)
    w_in_b = w_in[0].astype(bf16)
    lams = [t[0].reshape(1, HEAD_DIM) for t in (lambda_q1, lambda_k1, lambda_q2, lambda_k2)]
    subln = subln_g[0].reshape(1, V_DIM)
    mix_w = [dw_w[0], dw_b[0].reshape(1, d_conv), conv_ln_g[0].reshape(1, d_conv),
             conv_ln_b[0].reshape(1, d_conv), w_pw2[0].astype(bf16), b_pw2[0].reshape(1, d_conv),
             w_out[0].astype(bf16), final_norm_g.reshape(1, d_model)]

    xp = x_prompt.reshape(batch * seq, d_model)
    xs = x_sample.reshape(n_seq * n_new, d_model)
    q, kb, vt, sga, u, sgc, conv_p, k, v = _in_proj_prompt(
        xp, norm_g2, w_in_b, batch=batch, seq=seq, block_rows=512)
    qs, ks, vs, sgas, us, sgcs = _in_proj_sample(xs, norm_g2, w_in_b, block_rows=512)
    att_s, convg = _sample_attn(page_table, qs, ks, vs, lams, subln, cache_k, cache_v, u, sgc,
                                mix_w[:6], n_new=n_new, prompt_seq=seq)
    attg = _prompt_attn(q, kb, vt, sga, lams, subln, batch=batch, seq=seq, block=512)
    yp = _out_prompt(xp, attg, convg, mix_w[6], mix_w[7], block_rows=1024)
    ys, conv_s = _mix_sample(xs, att_s, sgas, us, state_conv[0], sgcs, mix_w, n_new=n_new,
                             seqs_per_block=32)
    kv_shape = (1, batch, seq, N_HEADS, V_DIM)
    kvs_shape = (1, n_seq, n_new, N_HEADS, V_DIM)

    return (yp.reshape(batch, seq, d_model), ys.reshape(n_seq, n_new, d_model),
            k.reshape(kv_shape), v.reshape(kv_shape), conv_p[None],
            ks.reshape(kvs_shape), vs.reshape(kvs_shape), conv_s[None])
```

```python
import functools
import math

import jax
import jax.numpy as jnp
from jax import lax
from jax.experimental import pallas as pl
from jax.experimental.pallas import tpu as pltpu

N_HEADS = 4
HEAD_DIM = 64
V_DIM = 2 * HEAD_DIM
CONV_W = 31
EPS = 1e-5
LAM_INIT = 0.8 - 0.6 * math.exp(-0.3 * 0)

Q_SCALE = HEAD_DIM ** -0.5 * math.log2(math.e)

SUBLANES = 8
VMEM_LIMIT_BYTES = 56 * 1024 * 1024

KEY_SUB = 512
ATTN_UNROLL = 4
ONES_ROWS = 16
PAGE_SLOTS = 3
CONV_HALO = 32
CONV_CHUNK = 64
NEG_INF = float("-inf")


def _silu(x):
    return x * jax.nn.sigmoid(x)


def _dot(a, b):
    return jnp.dot(a, b, preferred_element_type=jnp.float32)


def _dot_nt(a, b):
    return lax.dot_general(a, b, (((1,), (1,)), ((), ())), preferred_element_type=jnp.float32)


def _normed_input(x_ref, g_ref):
    x = x_ref[...]
    h = x * lax.rsqrt(jnp.mean(x * x, axis=-1, keepdims=True) + EPS) * g_ref[...]
    return h.astype(jnp.bfloat16)


def _layer_norm_silu(c, ln_g_ref, ln_b_ref):
    mu = jnp.mean(c, axis=-1, keepdims=True)
    d = c - mu
    var = jnp.mean(d * d, axis=-1, keepdims=True)
    ln = d * lax.rsqrt(var + EPS) * ln_g_ref[...] + ln_b_ref[...]
    return _silu(ln).astype(jnp.bfloat16)


def _in_proj_sample_kernel(x_ref, g_ref, w_ref, q_ref, k_ref, v_ref, sga_ref, u_ref, sgc_ref):
    hb = _normed_input(x_ref, g_ref)
    c = q_ref.shape[-1]

    def col(i):
        return _dot(hb, w_ref[:, i * c:(i + 1) * c])

    q_ref[...] = col(0) * Q_SCALE
    k = col(1)
    v = col(2)
    for h in range(N_HEADS):
        hs = slice(h * V_DIM, (h + 1) * V_DIM)
        k_ref[:, h, :] = k[:, hs]
        v_ref[:, h, :] = v[:, hs]
    sga_ref[...] = _silu(col(3))
    u_ref[...] = col(4) * jax.nn.sigmoid(col(5))
    sgc_ref[...] = _silu(col(6))


def _in_proj_sample(x2d, norm_g, w_in_bf16, *, block_rows):
    rows, d_model = x2d.shape
    c = w_in_bf16.shape[1] // 7
    row_spec = pl.BlockSpec((block_rows, c), lambda i: (i, 0))
    kv_spec = pl.BlockSpec((block_rows, N_HEADS, V_DIM), lambda i: (i, 0, 0))
    f32_rows = jax.ShapeDtypeStruct((rows, c), jnp.float32)
    kv_rows = jax.ShapeDtypeStruct((rows, N_HEADS, V_DIM), jnp.float32)
    return pl.pallas_call(
        _in_proj_sample_kernel,
        grid=(rows // block_rows,),
        in_specs=[
            pl.BlockSpec((block_rows, d_model), lambda i: (i, 0)),
            pl.BlockSpec((1, d_model), lambda i: (0, 0)),
            pl.BlockSpec(w_in_bf16.shape, lambda i: (0, 0)),
        ],
        out_specs=[row_spec, kv_spec, kv_spec, row_spec, row_spec, row_spec],
        out_shape=[f32_rows, kv_rows, kv_rows, f32_rows, f32_rows, f32_rows],
        compiler_params=pltpu.CompilerParams(
            dimension_semantics=("arbitrary",), vmem_limit_bytes=VMEM_LIMIT_BYTES),
        name="in_proj_sample",
    )(x2d, norm_g, w_in_bf16)


def _in_proj_prompt_kernel(x_ref, g_ref, w_ref, q_ref, kb_ref, vt_ref, sga_ref, u_ref, sgc_ref,
                           cstate_ref, k_hbm, v_hbm, kst_ref, vst_ref, sem):
    i = pl.program_id(0)
    rows = x_ref.shape[0]
    hb = _normed_input(x_ref, g_ref)
    c = q_ref.shape[-1]

    def col(j):
        return _dot(hb, w_ref[:, j * c:(j + 1) * c])

    def kv_copies(step):
        dst_rows = pl.ds(step * rows, rows)
        copies = []
        for h in range(N_HEADS):
            copies.append(pltpu.make_async_copy(kst_ref.at[h], k_hbm.at[dst_rows, h, :], sem.at[0]))
            copies.append(pltpu.make_async_copy(vst_ref.at[h], v_hbm.at[dst_rows, h, :], sem.at[1]))
        return copies

    q_ref[...] = (col(0) * Q_SCALE).astype(jnp.bfloat16)
    k = col(1)
    v = col(2)
    kb_ref[...] = k.astype(jnp.bfloat16)
    vt_ref[...] = v.T.astype(jnp.bfloat16)

    @pl.when(i > 0)
    def _():
        for cp in kv_copies(i - 1):
            cp.wait()

    for h in range(N_HEADS):
        kst_ref[h] = k[:, h * V_DIM:(h + 1) * V_DIM]
        vst_ref[h] = v[:, h * V_DIM:(h + 1) * V_DIM]
    for cp in kv_copies(i):
        cp.start()

    sga_ref[...] = _silu(col(3)).astype(jnp.bfloat16)
    u_ref[...] = col(4) * jax.nn.sigmoid(col(5))
    sgc_ref[...] = _silu(col(6)).astype(jnp.bfloat16)
    cstate_ref[0] = u_ref[rows - (CONV_W - 1):rows, :]

    @pl.when(i == pl.num_programs(0) - 1)
    def _():
        for cp in kv_copies(i):
            cp.wait()


def _in_proj_prompt(x2d, norm_g, w_in_bf16, *, batch, seq, block_rows):
    rows, d_model = x2d.shape
    c = w_in_bf16.shape[1] // 7
    tiles_per_seq = seq // block_rows
    hist = CONV_W - 1
    full = lambda a: pl.BlockSpec(a.shape, lambda i: (0,) * a.ndim)
    row_spec = pl.BlockSpec((block_rows, c), lambda i: (i, 0))
    bf16_rows = jax.ShapeDtypeStruct((rows, c), jnp.bfloat16)
    kv_rows = jax.ShapeDtypeStruct((rows, N_HEADS, V_DIM), jnp.float32)
    any_spec = pl.BlockSpec(memory_space=pl.ANY)
    return pl.pallas_call(
        _in_proj_prompt_kernel,
        grid=(rows // block_rows,),
        in_specs=[pl.BlockSpec((block_rows, d_model), lambda i: (i, 0)), full(norm_g),
                  full(w_in_bf16)],
        out_specs=[row_spec, row_spec, pl.BlockSpec((c, block_rows), lambda i: (0, i)), row_spec,
                   row_spec, row_spec,
                   pl.BlockSpec((1, hist, c), lambda i: (i // tiles_per_seq, 0, 0)),
                   any_spec, any_spec],
        out_shape=[bf16_rows, bf16_rows, jax.ShapeDtypeStruct((c, rows), jnp.bfloat16), bf16_rows,
                   jax.ShapeDtypeStruct((rows, c), jnp.float32), bf16_rows,
                   jax.ShapeDtypeStruct((batch, hist, c), jnp.float32), kv_rows, kv_rows],
        scratch_shapes=[
            pltpu.VMEM((N_HEADS, block_rows, V_DIM), jnp.float32),
            pltpu.VMEM((N_HEADS, block_rows, V_DIM), jnp.float32),
            pltpu.SemaphoreType.DMA((2,)),
        ],
        compiler_params=pltpu.CompilerParams(
            dimension_semantics=("arbitrary",), vmem_limit_bytes=VMEM_LIMIT_BYTES),
        name="in_proj_prompt",
    )(x2d, norm_g, w_in_bf16)


def _prompt_conv_history(rows, first, ext_ref):
    @pl.when(first)
    def _():
        ext_ref[0:CONV_HALO, :] = jnp.zeros((CONV_HALO, ext_ref.shape[1]), jnp.float32)

    @pl.when(jnp.logical_not(first))
    def _():
        ext_ref[0:CONV_HALO, :] = ext_ref[rows:rows + CONV_HALO, :]


def _prompt_conv_rows(u_ref, sgc_ref, dw_ref, db_ref, ln_g_ref, ln_b_ref, wp_ref, bp_ref,
                      cg_ref, ext_ref, shift_ref):
    rows, c = u_ref.shape
    hist = CONV_W - 1
    ext_ref[CONV_HALO:, :] = u_ref[...]
    for o in range(1, SUBLANES):
        shift_ref[o - 1] = ext_ref[o:o + shift_ref.shape[1], :]

    acts = []
    for ch in range(rows // CONV_CHUNK):
        r0 = ch * CONV_CHUNK
        acc = jnp.broadcast_to(db_ref[...], (CONV_CHUNK, c))
        for w in range(CONV_W):
            start = CONV_HALO - hist + w
            o = start % SUBLANES
            a = r0 + start - o
            src = ext_ref if o == 0 else shift_ref.at[o - 1]
            acc = acc + src[a:a + CONV_CHUNK, :] * dw_ref[w:w + 1, :]
        acts.append(_layer_norm_silu(acc, ln_g_ref, ln_b_ref))
    conv = _dot(jnp.concatenate(acts, axis=0), wp_ref[...]) + bp_ref[...]
    cg_ref[...] = (conv * sgc_ref[...].astype(jnp.float32)).astype(jnp.bfloat16)


def _lam(lq1_ref, lk1_ref, lq2_ref, lk2_ref):
    s1 = jnp.sum(lq1_ref[...] * lk1_ref[...], axis=-1, keepdims=True)
    s2 = jnp.sum(lq2_ref[...] * lk2_ref[...], axis=-1, keepdims=True)
    return jnp.exp(s1) - jnp.exp(s2) + LAM_INIT


def _split_maps(q):
    lane = lax.broadcasted_iota(jnp.int32, q.shape, 1)
    zero = jnp.zeros_like(q)
    return jnp.where(lane < HEAD_DIM, q, zero), jnp.where(lane >= HEAD_DIM, q, zero)


def _prompt_attn_kernel(q_ref, k_ref, vt_ref, sga_ref, lq1_ref, lk1_ref, lq2_ref, lk2_ref, g_ref,
                        o_ref, m_ref, acc_ref, s_ref, *, block):
    qi = pl.program_id(2)
    qs = _split_maps(q_ref[...])

    m_ref[...] = jnp.full_like(m_ref, NEG_INF)
    acc_ref[...] = jnp.zeros_like(acc_ref)

    n_sub = block // KEY_SUB
    ones_rows = jnp.ones((ONES_ROWS, KEY_SUB), jnp.bfloat16)

    def fold(x, op):
        return op(x.reshape(x.shape[0] // SUBLANES, SUBLANES, x.shape[1]), axis=0)

    def sub(c):
        return slice(c * KEY_SUB, (c + 1) * KEY_SUB)

    def scores(kc, mp):
        start = pl.multiple_of(kc * block, block)
        for c in range(n_sub):
            rows = pl.ds(start + c * KEY_SUB, KEY_SUB)
            s_ref[mp, sub(c), :] = _dot_nt(k_ref[rows, :], qs[mp])

    def mask_diagonal(mp):
        for c in range(n_sub):
            s = s_ref[mp, sub(c), :]
            key = lax.broadcasted_iota(jnp.int32, s.shape, 0) + c * KEY_SUB
            qry = lax.broadcasted_iota(jnp.int32, s.shape, 1)
            s_ref[mp, sub(c), :] = jnp.where(key <= qry, s, NEG_INF)

    def accumulate(kc, mp):
        start = pl.multiple_of(kc * block, block)
        mx = None
        for c in range(n_sub):
            part = fold(s_ref[mp, sub(c), :], jnp.max)
            mx = part if mx is None else jnp.maximum(mx, part)
        m_prev = m_ref[mp]
        m_new = jnp.maximum(m_prev, jnp.max(mx, axis=0, keepdims=True))
        alpha = jnp.exp2(m_prev - m_new)
        pv = None
        for c in range(n_sub):
            p = jnp.exp2(s_ref[mp, sub(c), :] - m_new)
            cols = pl.ds(start + c * KEY_SUB, KEY_SUB)
            vt_ones = jnp.concatenate([vt_ref[:, cols], ones_rows], axis=0)
            d = _dot(vt_ones, p.astype(jnp.bfloat16))
            pv = d if pv is None else pv + d
        acc_ref[mp] = alpha * acc_ref[mp] + pv
        m_ref[mp] = m_new

    scores(0, 0)

    def pipelined(kc):
        scores(kc, 1)
        accumulate(kc, 0)
        scores(kc + 1, 0)
        accumulate(kc, 1)

    def pipelined_group(first, n):
        for j in range(n):
            pipelined(first + j)

    def unrolled(t, carry):
        pipelined_group(t * ATTN_UNROLL, ATTN_UNROLL)
        return carry

    n_unrolled = lax.shift_right_logical(qi, ATTN_UNROLL.bit_length() - 1)
    lax.fori_loop(0, n_unrolled, unrolled, 0)
    done = n_unrolled * ATTN_UNROLL
    n = ATTN_UNROLL // 2
    while n >= 1:
        @pl.when(lax.bitwise_and(qi, n) != 0)
        def _(n=n, first=done):
            pipelined_group(first, n)
        done = done + lax.bitwise_and(qi, n)
        n //= 2

    scores(qi, 1)
    for mp in range(2):
        mask_diagonal(mp)
        accumulate(qi, mp)

    lam = _lam(lq1_ref, lk1_ref, lq2_ref, lk2_ref)
    o = (acc_ref[0, :V_DIM] / acc_ref[0, V_DIM:V_DIM + 1]
         - lam * (acc_ref[1, :V_DIM] / acc_ref[1, V_DIM:V_DIM + 1]))
    y = o * lax.rsqrt(jnp.mean(o * o, axis=0, keepdims=True) + EPS)
    att = (y * g_ref[...] * (1.0 - LAM_INIT)).T
    o_ref[...] = (att * sga_ref[...].astype(jnp.float32)).astype(o_ref.dtype)


def _prompt_attn(q, kb, vt, sga, lams, subln_g, *, batch, seq, block):
    rows, d_att = q.shape
    nq = seq // block
    vec = lambda n: pl.BlockSpec((1, n), lambda b, h, i: (0, 0))
    q_spec = pl.BlockSpec((block, V_DIM), lambda b, h, i: (b * nq + i, h))
    kernel = functools.partial(_prompt_attn_kernel, block=block)
    return pl.pallas_call(
        kernel,
        grid=(batch, N_HEADS, nq),
        in_specs=[
            q_spec,
            pl.BlockSpec((seq, V_DIM), lambda b, h, i: (b, h)),
            pl.BlockSpec((V_DIM, seq), lambda b, h, i: (h, b)),
            q_spec,
            vec(HEAD_DIM), vec(HEAD_DIM), vec(HEAD_DIM), vec(HEAD_DIM),
            pl.BlockSpec((V_DIM, 1), lambda b, h, i: (0, 0)),
        ],
        out_specs=q_spec,
        out_shape=jax.ShapeDtypeStruct((rows, d_att), jnp.bfloat16),
        scratch_shapes=[
            pltpu.VMEM((2, 1, block), jnp.float32),
            pltpu.VMEM((2, V_DIM + ONES_ROWS, block), jnp.float32),
            pltpu.VMEM((2, block, block), jnp.float32),
        ],
        compiler_params=pltpu.CompilerParams(
            dimension_semantics=("arbitrary", "arbitrary", "arbitrary"),
            vmem_limit_bytes=VMEM_LIMIT_BYTES),
        name="prompt_attn",
    )(q, kb, vt, sga, *lams, subln_g.reshape(V_DIM, 1))


def _sample_attn_kernel(pt_ref, q_ref, kn_ref, vn_ref, lq1_ref, lk1_ref, lq2_ref, lk2_ref, g_ref,
                        u_ref, sgc_ref, dw_ref, db_ref, ln_g_ref, ln_b_ref, wp_ref, bp_ref,
                        ck_hbm, cv_hbm, o_ref, cg_ref, kbuf, vbuf, ext_ref, shift_ref, sem, *,
                        n_pages, page, conv_tiles_per_seq, sample_seqs):
    s_idx = pl.program_id(0)
    n_seq = pl.num_programs(0)
    n_slots = kbuf.shape[0]
    ahead = n_slots - 1
    slot = lax.rem(s_idx, n_slots)

    def page_copies(seq, slot_):
        copies = []
        for p in range(n_pages):
            pid = pt_ref[seq * n_pages + p]
            dst = pl.ds(p * page, page)
            for h in range(N_HEADS):
                copies.append(pltpu.make_async_copy(
                    ck_hbm.at[0, pid, :, h, :], kbuf.at[slot_, h, dst], sem.at[0, slot_]))
                copies.append(pltpu.make_async_copy(
                    cv_hbm.at[0, pid, :, h, :], vbuf.at[slot_, h, dst], sem.at[1, slot_]))
        return copies

    assert sample_seqs >= ahead

    @pl.when(s_idx == 0)
    def _():
        for j in range(ahead):
            for cp in page_copies(j, j):
                cp.start()

    @pl.when(s_idx + ahead < n_seq)
    def _():
        for cp in page_copies(s_idx + ahead, lax.rem(s_idx + ahead, n_slots)):
            cp.start()

    _prompt_conv_history(u_ref.shape[0], s_idx % conv_tiles_per_seq == 0, ext_ref)

    for cp in page_copies(s_idx, slot):
        cp.wait()

    _prompt_conv_rows(u_ref, sgc_ref, dw_ref, db_ref, ln_g_ref, ln_b_ref, wp_ref, bp_ref, cg_ref,
                      ext_ref, shift_ref)

    lam = _lam(lq1_ref, lk1_ref, lq2_ref, lk2_ref)
    t = q_ref.shape[0]
    row = lax.broadcasted_iota(jnp.int32, (t, t), 0)
    colk = lax.broadcasted_iota(jnp.int32, (t, t), 1)
    causal = jnp.concatenate([colk <= row] * 4, axis=0)
    bf16 = jnp.bfloat16
    pairs = [(h, h + 1) for h in range(0, N_HEADS, 2)]

    def lanes(ref_fn, pair):
        return jnp.concatenate([ref_fn(h).astype(bf16) for h in pair], axis=1)

    s_past, s_new = [], []
    for ha, hb in pairs:
        qa, qb = [jnp.concatenate(_split_maps(q_ref[:, h * V_DIM:(h + 1) * V_DIM].astype(bf16)),
                                  axis=0) for h in (ha, hb)]
        zero = jnp.zeros_like(qa)
        q_pair = jnp.concatenate([jnp.concatenate([qa, zero], axis=1),
                                  jnp.concatenate([zero, qb], axis=1)], axis=0)
        s_past.append(_dot_nt(q_pair, lanes(lambda h: kbuf[slot, h], (ha, hb))))
        s_new.append(jnp.where(
            causal, _dot_nt(q_pair, lanes(lambda h: kn_ref[:, h, :], (ha, hb))), NEG_INF))
    a_past, a_new = [], []
    for i in range(len(pairs)):
        m = jnp.maximum(jnp.max(s_past[i], axis=-1, keepdims=True),
                        jnp.max(s_new[i], axis=-1, keepdims=True))
        e_past = jnp.exp2(s_past[i] - m)
        e_new = jnp.exp2(s_new[i] - m)
        denom = jnp.sum(e_past, axis=-1, keepdims=True) + jnp.sum(e_new, axis=-1, keepdims=True)
        p_past = e_past / denom
        p_new = e_new / denom

        def differential(p):
            return jnp.concatenate([p[0:t] - lam * p[t:2 * t],
                                    p[2 * t:3 * t] - lam * p[3 * t:4 * t]], axis=0).astype(bf16)

        a_past.append(differential(p_past))
        a_new.append(differential(p_new))
    for i, pair in enumerate(pairs):
        o_pair = (_dot(a_past[i], lanes(lambda h: vbuf[slot, h], pair))
                  + _dot(a_new[i], lanes(lambda h: vn_ref[:, h, :], pair)))
        for j, h in enumerate(pair):
            o = o_pair[j * t:(j + 1) * t, j * V_DIM:(j + 1) * V_DIM]
            y = o * lax.rsqrt(jnp.mean(o * o, axis=-1, keepdims=True) + EPS)
            o_ref[:, h * V_DIM:(h + 1) * V_DIM] = y * g_ref[...] * (1.0 - LAM_INIT)


def _sample_attn(page_table, q, k_new, v_new, lams, subln_g, cache_k, cache_v, u_prompt,
                 sgc_prompt, conv_w, *, n_new, prompt_seq):
    n_seq, n_pages = page_table.shape
    _, _, page, n_heads, v_dim = cache_k.shape
    past = n_pages * page
    d_att = q.shape[1]
    prompt_rows, d_conv = u_prompt.shape
    conv_rows = prompt_rows // n_seq
    assert conv_rows * n_seq == prompt_rows and conv_rows % CONV_CHUNK == 0
    assert prompt_seq % conv_rows == 0 and conv_rows >= CONV_HALO
    row_spec = pl.BlockSpec((n_new, d_att), lambda s, pt: (s, 0))
    kv_spec = pl.BlockSpec((n_new, n_heads, v_dim), lambda s, pt: (s, 0, 0))
    conv_spec = pl.BlockSpec((conv_rows, d_conv), lambda s, pt: (s, 0))
    vec = lambda n: pl.BlockSpec((1, n), lambda s, pt: (0, 0))
    full = lambda a: pl.BlockSpec(a.shape, lambda s, pt: (0,) * a.ndim)
    kernel = functools.partial(_sample_attn_kernel, n_pages=n_pages, page=page,
                               conv_tiles_per_seq=prompt_seq // conv_rows, sample_seqs=n_seq)
    return pl.pallas_call(
        kernel,
        grid_spec=pltpu.PrefetchScalarGridSpec(
            num_scalar_prefetch=1,
            grid=(n_seq,),
            in_specs=[
                row_spec, kv_spec, kv_spec,
                vec(HEAD_DIM), vec(HEAD_DIM), vec(HEAD_DIM), vec(HEAD_DIM), vec(V_DIM),
                conv_spec, conv_spec] + [full(a) for a in conv_w] + [
                pl.BlockSpec(memory_space=pl.ANY),
                pl.BlockSpec(memory_space=pl.ANY),
            ],
            out_specs=[row_spec, conv_spec],
            scratch_shapes=[
                pltpu.VMEM((PAGE_SLOTS, n_heads, past, v_dim), jnp.float32),
                pltpu.VMEM((PAGE_SLOTS, n_heads, past, v_dim), jnp.float32),
                pltpu.VMEM((CONV_HALO + conv_rows, d_conv), jnp.float32),
                pltpu.VMEM((SUBLANES - 1, CONV_HALO + conv_rows - SUBLANES, d_conv), jnp.float32),
                pltpu.SemaphoreType.DMA((2, PAGE_SLOTS)),
            ],
        ),
        out_shape=[jax.ShapeDtypeStruct(q.shape, jnp.float32),
                   jax.ShapeDtypeStruct((prompt_rows, d_conv), jnp.bfloat16)],
        compiler_params=pltpu.CompilerParams(
            dimension_semantics=("arbitrary",), vmem_limit_bytes=VMEM_LIMIT_BYTES),
        name="sample_attn",
    )(page_table.reshape(-1), q, k_new, v_new, *lams, subln_g, u_prompt, sgc_prompt, *conv_w,
      cache_k, cache_v)


def _out_tail(x, attg, convg, wo_att_ref, wo_conv_ref, fg_ref):
    y = x + _dot(attg, wo_att_ref[...]) + _dot(convg, wo_conv_ref[...])
    return y * lax.rsqrt(jnp.mean(y * y, axis=-1, keepdims=True) + EPS) * fg_ref[...]


def _weight_specs(d_conv, d_model, d_att):
    full = lambda r, c: pl.BlockSpec((r, c), lambda i: (0, 0))
    return [
        full(CONV_W, d_conv), full(1, d_conv), full(1, d_conv), full(1, d_conv),
        full(d_conv, d_conv), full(1, d_conv),
        pl.BlockSpec((d_att, d_model), lambda i: (0, 0)),
        pl.BlockSpec((d_conv, d_model), lambda i: (d_att // d_conv, 0)),
        full(1, d_model),
    ]


def _out_prompt_kernel(x_ref, attg_ref, cg_ref, wo_att_ref, wo_conv_ref, fg_ref, y_ref):
    y_ref[...] = _out_tail(x_ref[...], attg_ref[...], cg_ref[...], wo_att_ref, wo_conv_ref, fg_ref)


def _out_prompt(x2d, attg, convg, w_out_bf16, final_g, *, block_rows):
    rows, d_model = x2d.shape
    d_att = attg.shape[1]
    d_conv = convg.shape[1]
    row = lambda c: pl.BlockSpec((block_rows, c), lambda i: (i, 0))
    return pl.pallas_call(
        _out_prompt_kernel,
        grid=(rows // block_rows,),
        in_specs=[row(d_model), row(d_att), row(d_conv),
                  pl.BlockSpec((d_att, d_model), lambda i: (0, 0)),
                  pl.BlockSpec((d_conv, d_model), lambda i: (d_att // d_conv, 0)),
                  pl.BlockSpec((1, d_model), lambda i: (0, 0))],
        out_specs=row(d_model),
        out_shape=jax.ShapeDtypeStruct((rows, d_model), jnp.float32),
        compiler_params=pltpu.CompilerParams(
            dimension_semantics=("arbitrary",), vmem_limit_bytes=VMEM_LIMIT_BYTES),
        name="out_prompt",
    )(x2d, attg, convg, w_out_bf16, w_out_bf16, final_g)


def _mix_sample_kernel(x_ref, att_ref, sga_ref, u_ref, st_ref, sgc_ref, dw_ref, db_ref,
                       ln_g_ref, ln_b_ref, wp_ref, bp_ref, wo_att_ref, wo_conv_ref, fg_ref,
                       y_ref, st_out_ref, ext_ref, *, n_new):
    g = st_ref.shape[0]
    d_conv = st_ref.shape[2]
    hist = CONV_W - 1
    ext_ref[:, 0:hist, :] = st_ref[...]
    ext_ref[:, hist:hist + n_new, :] = u_ref[...].reshape(g, n_new, d_conv)
    c = jnp.broadcast_to(db_ref[...].reshape(1, 1, d_conv), (g, n_new, d_conv))
    for w in range(CONV_W):
        c = c + ext_ref[:, w:w + n_new, :] * dw_ref[w:w + 1, :].reshape(1, 1, d_conv)
    st_out_ref[...] = ext_ref[:, n_new:n_new + hist, :]
    act = _layer_norm_silu(c.reshape(g * n_new, d_conv), ln_g_ref, ln_b_ref)
    conv = _dot(act, wp_ref[...]) + bp_ref[...]
    y_ref[...] = _out_tail(x_ref[...], (att_ref[...] * sga_ref[...]).astype(jnp.bfloat16),
                           (conv * sgc_ref[...]).astype(jnp.bfloat16), wo_att_ref, wo_conv_ref,
                           fg_ref)


def _mix_sample(x2d, att, sga, u, state, sgc, weights, *, n_new, seqs_per_block):
    rows, d_model = x2d.shape
    d_att = att.shape[1]
    n_seq, hist, d_conv = state.shape
    block_rows = seqs_per_block * n_new
    row = lambda c: pl.BlockSpec((block_rows, c), lambda i: (i, 0))
    st_spec = pl.BlockSpec((seqs_per_block, hist, d_conv), lambda i: (i, 0, 0))
    ext_rows = -(-(hist + n_new) // SUBLANES) * SUBLANES
    kernel = functools.partial(_mix_sample_kernel, n_new=n_new)
    w_out = weights[6]
    return pl.pallas_call(
        kernel,
        grid=(n_seq // seqs_per_block,),
        in_specs=[row(d_model), row(d_att), row(d_att), row(d_conv), st_spec, row(d_conv)]
        + _weight_specs(d_conv, d_model, d_att),
        out_specs=[row(d_model), st_spec],
        out_shape=[jax.ShapeDtypeStruct((rows, d_model), jnp.float32),
                   jax.ShapeDtypeStruct(state.shape, jnp.float32)],
        scratch_shapes=[pltpu.VMEM((seqs_per_block, ext_rows, d_conv), jnp.float32)],
        compiler_params=pltpu.CompilerParams(
            dimension_semantics=("arbitrary",), vmem_limit_bytes=VMEM_LIMIT_BYTES),
        name="mix_sample",
    )(x2d, att, sga, u, state, sgc, *weights[:6], w_out, w_out, weights[7])


def kernel(x_prompt, x_sample, cache_k, cache_v, state_conv, page_table, norm_g, w_in, lambda_q1,
           lambda_k1, lambda_q2, lambda_k2, subln_g, dw_w, dw_b, conv_ln_g, conv_ln_b, w_pw2,
           b_pw2, w_out, final_norm_g):
    batch, seq, d_model = x_prompt.shape
    n_seq, n_new, _ = x_sample.shape
    depth, _, _, n_heads, v_dim = cache_k.shape
    assert depth == 1 and n_heads == N_HEADS and v_dim == V_DIM
    d_conv = dw_w.shape[2]
    bf16 = jnp.bfloat16

    norm_g2 = norm_g[0].reshape(1, d_model)
    w_in_b = w_in[0].astype(bf16)
    lams = [t[0].reshape(1, HEAD_DIM) for t in (lambda_q1, lambda_k1, lambda_q2, lambda_k2)]
    subln = subln_g[0].reshape(1, V_DIM)
    mix_w = [dw_w[0], dw_b[0].reshape(1, d_conv), conv_ln_g[0].reshape(1, d_conv),
             conv_ln_b[0].reshape(1, d_conv), w_pw2[0].astype(bf16), b_pw2[0].reshape(1, d_conv),
             w_out[0].astype(bf16), final_norm_g.reshape(1, d_model)]

    xp = x_prompt.reshape(batch * seq, d_model)
    xs = x_sample.reshape(n_seq * n_new, d_model)
    q, kb, vt, sga, u, sgc, conv_p, k, v = _in_proj_prompt(
        xp, norm_g2, w_in_b, batch=batch, seq=seq, block_rows=512)
    qs, ks, vs, sgas, us, sgcs = _in_proj_sample(xs, norm_g2, w_in_b, block_rows=512)
    att_s, convg = _sample_attn(page_table, qs, ks, vs, lams, subln, cache_k, cache_v, u, sgc,
                                mix_w[:6], n_new=n_new, prompt_seq=seq)
    attg = _prompt_attn(q, kb, vt, sga, lams, subln, batch=batch, seq=seq, block=512)
    yp = _out_prompt(xp, attg, convg, mix_w[6], mix_w[7], block_rows=1024)
    ys, conv_s = _mix_sample(xs, att_s, sgas, us, state_conv[0], sgcs, mix_w, n_new=n_new,
                             seqs_per_block=32)
    kv_shape = (1, batch, seq, N_HEADS, V_DIM)
    kvs_shape = (1, n_seq, n_new, N_HEADS, V_DIM)

    return (yp.reshape(batch, seq, d_model), ys.reshape(n_seq, n_new, d_model),
            k.reshape(kv_shape), v.reshape(kv_shape), conv_p[None],
            ks.reshape(kvs_shape), vs.reshape(kvs_shape), conv_s[None])
```

```python
import functools
import math

import jax
import jax.numpy as jnp
from jax import lax
from jax.experimental import pallas as pl
from jax.experimental.pallas import tpu as pltpu

N_HEADS = 4
HEAD_DIM = 64
V_DIM = 2 * HEAD_DIM
CONV_W = 31
EPS = 1e-5
LAM_INIT = 0.8 - 0.6 * math.exp(-0.3 * 0)

Q_SCALE = HEAD_DIM ** -0.5 * math.log2(math.e)

SUBLANES = 8
VMEM_LIMIT_BYTES = 56 * 1024 * 1024

KEY_SUB = 512
ATTN_UNROLL = 4
ONES_ROWS = 16
PAGE_SLOTS = 3
CONV_HALO = 32
CONV_CHUNK = 64
NEG_INF = float("-inf")


def _silu(x):
    return x * jax.nn.sigmoid(x)


def _dot(a, b):
    return jnp.dot(a, b, preferred_element_type=jnp.float32)


def _dot_nt(a, b):
    return lax.dot_general(a, b, (((1,), (1,)), ((), ())), preferred_element_type=jnp.float32)


def _normed_input(x_ref, g_ref):
    x = x_ref[...]
    h = x * lax.rsqrt(jnp.mean(x * x, axis=-1, keepdims=True) + EPS) * g_ref[...]
    return h.astype(jnp.bfloat16)


def _layer_norm_silu(c, ln_g_ref, ln_b_ref):
    mu = jnp.mean(c, axis=-1, keepdims=True)
    d = c - mu
    var = jnp.mean(d * d, axis=-1, keepdims=True)
    ln = d * lax.rsqrt(var + EPS) * ln_g_ref[...] + ln_b_ref[...]
    return _silu(ln).astype(jnp.bfloat16)


def _in_proj_sample_kernel(x_ref, g_ref, w_ref, q_ref, k_ref, v_ref, sga_ref, u_ref, sgc_ref):
    hb = _normed_input(x_ref, g_ref)
    c = q_ref.shape[-1]

    def col(i):
        return _dot(hb, w_ref[:, i * c:(i + 1) * c])

    q_ref[...] = col(0) * Q_SCALE
    k = col(1)
    v = col(2)
    for h in range(N_HEADS):
        hs = slice(h * V_DIM, (h + 1) * V_DIM)
        k_ref[:, h, :] = k[:, hs]
        v_ref[:, h, :] = v[:, hs]
    sga_ref[...] = _silu(col(3))
    u_ref[...] = col(4) * jax.nn.sigmoid(col(5))
    sgc_ref[...] = _silu(col(6))


def _in_proj_sample(x2d, norm_g, w_in_bf16, *, block_rows):
    rows, d_model = x2d.shape
    c = w_in_bf16.shape[1] // 7
    row_spec = pl.BlockSpec((block_rows, c), lambda i: (i, 0))
    kv_spec = pl.BlockSpec((block_rows, N_HEADS, V_DIM), lambda i: (i, 0, 0))
    f32_rows = jax.ShapeDtypeStruct((rows, c), jnp.float32)
    kv_rows = jax.ShapeDtypeStruct((rows, N_HEADS, V_DIM), jnp.float32)
    return pl.pallas_call(
        _in_proj_sample_kernel,
        grid=(rows // block_rows,),
        in_specs=[
            pl.BlockSpec((block_rows, d_model), lambda i: (i, 0)),
            pl.BlockSpec((1, d_model), lambda i: (0, 0)),
            pl.BlockSpec(w_in_bf16.shape, lambda i: (0, 0)),
        ],
        out_specs=[row_spec, kv_spec, kv_spec, row_spec, row_spec, row_spec],
        out_shape=[f32_rows, kv_rows, kv_rows, f32_rows, f32_rows, f32_rows],
        compiler_params=pltpu.CompilerParams(
            dimension_semantics=("arbitrary",), vmem_limit_bytes=VMEM_LIMIT_BYTES),
        name="in_proj_sample",
    )(x2d, norm_g, w_in_bf16)


def _in_proj_prompt_kernel(x_ref, g_ref, w_ref, q_ref, kb_ref, vt_ref, sga_ref, u_ref, sgc_ref,
                           cstate_ref, k_hbm, v_hbm, kst_ref, vst_ref, sem):
    i = pl.program_id(0)
    rows = x_ref.shape[0]
    hb = _normed_input(x_ref, g_ref)
    c = q_ref.shape[-1]

    def col(j):
        return _dot(hb, w_ref[:, j * c:(j + 1) * c])

    def kv_copies(step):
        dst_rows = pl.ds(step * rows, rows)
        copies = []
        for h in range(N_HEADS):
            copies.append(pltpu.make_async_copy(kst_ref.at[h], k_hbm.at[dst_rows, h, :], sem.at[0]))
            copies.append(pltpu.make_async_copy(vst_ref.at[h], v_hbm.at[dst_rows, h, :], sem.at[1]))
        return copies

    q_ref[...] = (col(0) * Q_SCALE).astype(jnp.bfloat16)
    k = col(1)
    v = col(2)
    kb_ref[...] = k.astype(jnp.bfloat16)
    vt_ref[...] = v.T.astype(jnp.bfloat16)

    @pl.when(i > 0)
    def _():
        for cp in kv_copies(i - 1):
            cp.wait()

    for h in range(N_HEADS):
        kst_ref[h] = k[:, h * V_DIM:(h + 1) * V_DIM]
        vst_ref[h] = v[:, h * V_DIM:(h + 1) * V_DIM]
    for cp in kv_copies(i):
        cp.start()

    sga_ref[...] = _silu(col(3)).astype(jnp.bfloat16)
    u_ref[...] = col(4) * jax.nn.sigmoid(col(5))
    sgc_ref[...] = _silu(col(6)).astype(jnp.bfloat16)
    cstate_ref[0] = u_ref[rows - (CONV_W - 1):rows, :]

    @pl.when(i == pl.num_programs(0) - 1)
    def _():
        for cp in kv_copies(i):
            cp.wait()


def _in_proj_prompt(x2d, norm_g, w_in_bf16, *, batch, seq, block_rows):
    rows, d_model = x2d.shape
    c = w_in_bf16.shape[1] // 7
    tiles_per_seq = seq // block_rows
    hist = CONV_W - 1
    full = lambda a: pl.BlockSpec(a.shape, lambda i: (0,) * a.ndim)
    row_spec = pl.BlockSpec((block_rows, c), lambda i: (i, 0))
    bf16_rows = jax.ShapeDtypeStruct((rows, c), jnp.bfloat16)
    kv_rows = jax.ShapeDtypeStruct((rows, N_HEADS, V_DIM), jnp.float32)
    any_spec = pl.BlockSpec(memory_space=pl.ANY)
    return pl.pallas_call(
        _in_proj_prompt_kernel,
        grid=(rows // block_rows,),
        in_specs=[pl.BlockSpec((block_rows, d_model), lambda i: (i, 0)), full(norm_g),
                  full(w_in_bf16)],
        out_specs=[row_spec, row_spec, pl.BlockSpec((c, block_rows), lambda i: (0, i)), row_spec,
                   row_spec, row_spec,
                   pl.BlockSpec((1, hist, c), lambda i: (i // tiles_per_seq, 0, 0)),
                   any_spec, any_spec],
        out_shape=[bf16_rows, bf16_rows, jax.ShapeDtypeStruct((c, rows), jnp.bfloat16), bf16_rows,
                   jax.ShapeDtypeStruct((rows, c), jnp.float32), bf16_rows,
                   jax.ShapeDtypeStruct((batch, hist, c), jnp.float32), kv_rows, kv_rows],
        scratch_shapes=[
            pltpu.VMEM((N_HEADS, block_rows, V_DIM), jnp.float32),
            pltpu.VMEM((N_HEADS, block_rows, V_DIM), jnp.float32),
            pltpu.SemaphoreType.DMA((2,)),
        ],
        compiler_params=pltpu.CompilerParams(
            dimension_semantics=("arbitrary",), vmem_limit_bytes=VMEM_LIMIT_BYTES),
        name="in_proj_prompt",
    )(x2d, norm_g, w_in_bf16)


def _prompt_conv_history(rows, first, ext_ref):
    @pl.when(first)
    def _():
        ext_ref[0:CONV_HALO, :] = jnp.zeros((CONV_HALO, ext_ref.shape[1]), jnp.float32)

    @pl.when(jnp.logical_not(first))
    def _():
        ext_ref[0:CONV_HALO, :] = ext_ref[rows:rows + CONV_HALO, :]


def _prompt_conv_rows(u_ref, sgc_ref, dw_ref, db_ref, ln_g_ref, ln_b_ref, wp_ref, bp_ref,
                      cg_ref, ext_ref, shift_ref):
    rows, c = u_ref.shape
    hist = CONV_W - 1
    ext_ref[CONV_HALO:, :] = u_ref[...]
    for o in range(1, SUBLANES):
        shift_ref[o - 1] = ext_ref[o:o + shift_ref.shape[1], :]

    acts = []
    for ch in range(rows // CONV_CHUNK):
        r0 = ch * CONV_CHUNK
        acc = jnp.broadcast_to(db_ref[...], (CONV_CHUNK, c))
        for w in range(CONV_W):
            start = CONV_HALO - hist + w
            o = start % SUBLANES
            a = r0 + start - o
            src = ext_ref if o == 0 else shift_ref.at[o - 1]
            acc = acc + src[a:a + CONV_CHUNK, :] * dw_ref[w:w + 1, :]
        acts.append(_layer_norm_silu(acc, ln_g_ref, ln_b_ref))
    conv = _dot(jnp.concatenate(acts, axis=0), wp_ref[...]) + bp_ref[...]
    cg_ref[...] = (conv * sgc_ref[...].astype(jnp.float32)).astype(jnp.bfloat16)


def _lam(lq1_ref, lk1_ref, lq2_ref, lk2_ref):
    s1 = jnp.sum(lq1_ref[...] * lk1_ref[...], axis=-1, keepdims=True)
    s2 = jnp.sum(lq2_ref[...] * lk2_ref[...], axis=-1, keepdims=True)
    return jnp.exp(s1) - jnp.exp(s2) + LAM_INIT


def _split_maps(q):
    lane = lax.broadcasted_iota(jnp.int32, q.shape, 1)
    zero = jnp.zeros_like(q)
    return jnp.where(lane < HEAD_DIM, q, zero), jnp.where(lane >= HEAD_DIM, q, zero)


def _prompt_attn_kernel(q_ref, k_ref, vt_ref, sga_ref, lq1_ref, lk1_ref, lq2_ref, lk2_ref, g_ref,
                        o_ref, m_ref, acc_ref, s_ref, *, block):
    qi = pl.program_id(2)
    qs = _split_maps(q_ref[...])

    m_ref[...] = jnp.full_like(m_ref, NEG_INF)
    acc_ref[...] = jnp.zeros_like(acc_ref)

    n_sub = block // KEY_SUB
    ones_rows = jnp.ones((ONES_ROWS, KEY_SUB), jnp.bfloat16)

    def fold(x, op):
        return op(x.reshape(x.shape[0] // SUBLANES, SUBLANES, x.shape[1]), axis=0)

    def sub(c):
        return slice(c * KEY_SUB, (c + 1) * KEY_SUB)

    def scores(kc, mp):
        start = pl.multiple_of(kc * block, block)
        for c in range(n_sub):
            rows = pl.ds(start + c * KEY_SUB, KEY_SUB)
            s_ref[mp, sub(c), :] = _dot_nt(k_ref[rows, :], qs[mp])

    def mask_diagonal(mp):
        for c in range(n_sub):
            s = s_ref[mp, sub(c), :]
            key = lax.broadcasted_iota(jnp.int32, s.shape, 0) + c * KEY_SUB
            qry = lax.broadcasted_iota(jnp.int32, s.shape, 1)
            s_ref[mp, sub(c), :] = jnp.where(key <= qry, s, NEG_INF)

    def accumulate(kc, mp):
        start = pl.multiple_of(kc * block, block)
        mx = None
        for c in range(n_sub):
            part = fold(s_ref[mp, sub(c), :], jnp.max)
            mx = part if mx is None else jnp.maximum(mx, part)
        m_prev = m_ref[mp]
        m_new = jnp.maximum(m_prev, jnp.max(mx, axis=0, keepdims=True))
        alpha = jnp.exp2(m_prev - m_new)
        pv = None
        for c in range(n_sub):
            p = jnp.exp2(s_ref[mp, sub(c), :] - m_new)
            cols = pl.ds(start + c * KEY_SUB, KEY_SUB)
            vt_ones = jnp.concatenate([vt_ref[:, cols], ones_rows], axis=0)
            d = _dot(vt_ones, p.astype(jnp.bfloat16))
            pv = d if pv is None else pv + d
        acc_ref[mp] = alpha * acc_ref[mp] + pv
        m_ref[mp] = m_new

    scores(0, 0)

    def pipelined(kc):
        scores(kc, 1)
        accumulate(kc, 0)
        scores(kc + 1, 0)
        accumulate(kc, 1)

    def pipelined_group(first, n):
        for j in range(n):
            pipelined(first + j)

    def unrolled(t, carry):
        pipelined_group(t * ATTN_UNROLL, ATTN_UNROLL)
        return carry

    n_unrolled = lax.shift_right_logical(qi, ATTN_UNROLL.bit_length() - 1)
    lax.fori_loop(0, n_unrolled, unrolled, 0)
    done = n_unrolled * ATTN_UNROLL
    n = ATTN_UNROLL // 2
    while n >= 1:
        @pl.when(lax.bitwise_and(qi, n) != 0)
        def _(n=n, first=done):
            pipelined_group(first, n)
        done = done + lax.bitwise_and(qi, n)
        n //= 2

    scores(qi, 1)
    for mp in range(2):
        mask_diagonal(mp)
        accumulate(qi, mp)

    lam = _lam(lq1_ref, lk1_ref, lq2_ref, lk2_ref)
    o = (acc_ref[0, :V_DIM] / acc_ref[0, V_DIM:V_DIM + 1]
         - lam * (acc_ref[1, :V_DIM] / acc_ref[1, V_DIM:V_DIM + 1]))
    y = o * lax.rsqrt(jnp.mean(o * o, axis=0, keepdims=True) + EPS)
    att = (y * g_ref[...] * (1.0 - LAM_INIT)).T
    o_ref[...] = (att * sga_ref[...].astype(jnp.float32)).astype(o_ref.dtype)


def _prompt_attn(q, kb, vt, sga, lams, subln_g, *, batch, seq, block):
    rows, d_att = q.shape
    nq = seq // block
    vec = lambda n: pl.BlockSpec((1, n), lambda b, h, i: (0, 0))
    q_spec = pl.BlockSpec((block, V_DIM), lambda b, h, i: (b * nq + i, h))
    kernel = functools.partial(_prompt_attn_kernel, block=block)
    return pl.pallas_call(
        kernel,
        grid=(batch, N_HEADS, nq),
        in_specs=[
            q_spec,
            pl.BlockSpec((seq, V_DIM), lambda b, h, i: (b, h)),
            pl.BlockSpec((V_DIM, seq), lambda b, h, i: (h, b)),
            q_spec,
            vec(HEAD_DIM), vec(HEAD_DIM), vec(HEAD_DIM), vec(HEAD_DIM),
            pl.BlockSpec((V_DIM, 1), lambda b, h, i: (0, 0)),
        ],
        out_specs=q_spec,
        out_shape=jax.ShapeDtypeStruct((rows, d_att), jnp.bfloat16),
        scratch_shapes=[
            pltpu.VMEM((2, 1, block), jnp.float32),
            pltpu.VMEM((2, V_DIM + ONES_ROWS, block), jnp.float32),
            pltpu.VMEM((2, block, block), jnp.float32),
        ],
        compiler_params=pltpu.CompilerParams(
            dimension_semantics=("arbitrary", "arbitrary", "arbitrary"),
            vmem_limit_bytes=VMEM_LIMIT_BYTES),
        name="prompt_attn",
    )(q, kb, vt, sga, *lams, subln_g.reshape(V_DIM, 1))


def _sample_attn_kernel(pt_ref, q_ref, kn_ref, vn_ref, lq1_ref, lk1_ref, lq2_ref, lk2_ref, g_ref,
                        u_ref, sgc_ref, dw_ref, db_ref, ln_g_ref, ln_b_ref, wp_ref, bp_ref,
                        ck_hbm, cv_hbm, o_ref, cg_ref, kbuf, vbuf, ext_ref, shift_ref, sem, *,
                        n_pages, page, conv_tiles_per_seq, sample_seqs):
    s_idx = pl.program_id(0)
    n_seq = pl.num_programs(0)
    n_slots = kbuf.shape[0]
    ahead = n_slots - 1
    slot = lax.rem(s_idx, n_slots)

    def page_copies(seq, slot_):
        copies = []
        for p in range(n_pages):
            pid = pt_ref[seq * n_pages + p]
            dst = pl.ds(p * page, page)
            for h in range(N_HEADS):
                copies.append(pltpu.make_async_copy(
                    ck_hbm.at[0, pid, :, h, :], kbuf.at[slot_, h, dst], sem.at[0, slot_]))
                copies.append(pltpu.make_async_copy(
                    cv_hbm.at[0, pid, :, h, :], vbuf.at[slot_, h, dst], sem.at[1, slot_]))
        return copies

    def start_pages(seq, slot_):
        for i, cp in enumerate(page_copies(seq, slot_)):
            cp.start(priority=i % 2)

    assert sample_seqs >= ahead

    @pl.when(s_idx == 0)
    def _():
        for j in range(ahead):
            start_pages(j, j)

    @pl.when(s_idx + ahead < n_seq)
    def _():
        start_pages(s_idx + ahead, lax.rem(s_idx + ahead, n_slots))

    _prompt_conv_history(u_ref.shape[0], s_idx % conv_tiles_per_seq == 0, ext_ref)

    for cp in page_copies(s_idx, slot):
        cp.wait()

    _prompt_conv_rows(u_ref, sgc_ref, dw_ref, db_ref, ln_g_ref, ln_b_ref, wp_ref, bp_ref, cg_ref,
                      ext_ref, shift_ref)

    lam = _lam(lq1_ref, lk1_ref, lq2_ref, lk2_ref)
    t = q_ref.shape[0]
    row = lax.broadcasted_iota(jnp.int32, (t, t), 0)
    colk = lax.broadcasted_iota(jnp.int32, (t, t), 1)
    causal = jnp.concatenate([colk <= row] * 4, axis=0)
    bf16 = jnp.bfloat16
    pairs = [(h, h + 1) for h in range(0, N_HEADS, 2)]

    def lanes(ref_fn, pair):
        return jnp.concatenate([ref_fn(h).astype(bf16) for h in pair], axis=1)

    s_past, s_new = [], []
    for ha, hb in pairs:
        qa, qb = [jnp.concatenate(_split_maps(q_ref[:, h * V_DIM:(h + 1) * V_DIM].astype(bf16)),
                                  axis=0) for h in (ha, hb)]
        zero = jnp.zeros_like(qa)
        q_pair = jnp.concatenate([jnp.concatenate([qa, zero], axis=1),
                                  jnp.concatenate([zero, qb], axis=1)], axis=0)
        s_past.append(_dot_nt(q_pair, lanes(lambda h: kbuf[slot, h], (ha, hb))))
        s_new.append(jnp.where(
            causal, _dot_nt(q_pair, lanes(lambda h: kn_ref[:, h, :], (ha, hb))), NEG_INF))
    a_past, a_new = [], []
    for i in range(len(pairs)):
        m = jnp.maximum(jnp.max(s_past[i], axis=-1, keepdims=True),
                        jnp.max(s_new[i], axis=-1, keepdims=True))
        e_past = jnp.exp2(s_past[i] - m)
        e_new = jnp.exp2(s_new[i] - m)
        denom = jnp.sum(e_past, axis=-1, keepdims=True) + jnp.sum(e_new, axis=-1, keepdims=True)
        p_past = e_past / denom
        p_new = e_new / denom

        def differential(p):
            return jnp.concatenate([p[0:t] - lam * p[t:2 * t],
                                    p[2 * t:3 * t] - lam * p[3 * t:4 * t]], axis=0).astype(bf16)

        a_past.append(differential(p_past))
        a_new.append(differential(p_new))
    for i, pair in enumerate(pairs):
        o_pair = (_dot(a_past[i], lanes(lambda h: vbuf[slot, h], pair))
                  + _dot(a_new[i], lanes(lambda h: vn_ref[:, h, :], pair)))
        for j, h in enumerate(pair):
            o = o_pair[j * t:(j + 1) * t, j * V_DIM:(j + 1) * V_DIM]
            y = o * lax.rsqrt(jnp.mean(o * o, axis=-1, keepdims=True) + EPS)
            o_ref[:, h * V_DIM:(h + 1) * V_DIM] = y * g_ref[...] * (1.0 - LAM_INIT)


def _sample_attn(page_table, q, k_new, v_new, lams, subln_g, cache_k, cache_v, u_prompt,
                 sgc_prompt, conv_w, *, n_new, prompt_seq):
    n_seq, n_pages = page_table.shape
    _, _, page, n_heads, v_dim = cache_k.shape
    past = n_pages * page
    d_att = q.shape[1]
    prompt_rows, d_conv = u_prompt.shape
    conv_rows = prompt_rows // n_seq
    assert conv_rows * n_seq == prompt_rows and conv_rows % CONV_CHUNK == 0
    assert prompt_seq % conv_rows == 0 and conv_rows >= CONV_HALO
    row_spec = pl.BlockSpec((n_new, d_att), lambda s, pt: (s, 0))
    kv_spec = pl.BlockSpec((n_new, n_heads, v_dim), lambda s, pt: (s, 0, 0))
    conv_spec = pl.BlockSpec((conv_rows, d_conv), lambda s, pt: (s, 0))
    vec = lambda n: pl.BlockSpec((1, n), lambda s, pt: (0, 0))
    full = lambda a: pl.BlockSpec(a.shape, lambda s, pt: (0,) * a.ndim)
    kernel = functools.partial(_sample_attn_kernel, n_pages=n_pages, page=page,
                               conv_tiles_per_seq=prompt_seq // conv_rows, sample_seqs=n_seq)
    return pl.pallas_call(
        kernel,
        grid_spec=pltpu.PrefetchScalarGridSpec(
            num_scalar_prefetch=1,
            grid=(n_seq,),
            in_specs=[
                row_spec, kv_spec, kv_spec,
                vec(HEAD_DIM), vec(HEAD_DIM), vec(HEAD_DIM), vec(HEAD_DIM), vec(V_DIM),
                conv_spec, conv_spec] + [full(a) for a in conv_w] + [
                pl.BlockSpec(memory_space=pl.ANY),
                pl.BlockSpec(memory_space=pl.ANY),
            ],
            out_specs=[row_spec, conv_spec],
            scratch_shapes=[
                pltpu.VMEM((PAGE_SLOTS, n_heads, past, v_dim), jnp.float32),
                pltpu.VMEM((PAGE_SLOTS, n_heads, past, v_dim), jnp.float32),
                pltpu.VMEM((CONV_HALO + conv_rows, d_conv), jnp.float32),
                pltpu.VMEM((SUBLANES - 1, CONV_HALO + conv_rows - SUBLANES, d_conv), jnp.float32),
                pltpu.SemaphoreType.DMA((2, PAGE_SLOTS)),
            ],
        ),
        out_shape=[jax.ShapeDtypeStruct(q.shape, jnp.float32),
                   jax.ShapeDtypeStruct((prompt_rows, d_conv), jnp.bfloat16)],
        compiler_params=pltpu.CompilerParams(
            dimension_semantics=("arbitrary",), vmem_limit_bytes=VMEM_LIMIT_BYTES),
        name="sample_attn",
    )(page_table.reshape(-1), q, k_new, v_new, *lams, subln_g, u_prompt, sgc_prompt, *conv_w,
      cache_k, cache_v)


def _out_tail(x, attg, convg, wo_att_ref, wo_conv_ref, fg_ref):
    y = x + _dot(attg, wo_att_ref[...]) + _dot(convg, wo_conv_ref[...])
    return y * lax.rsqrt(jnp.mean(y * y, axis=-1, keepdims=True) + EPS) * fg_ref[...]


def _weight_specs(d_conv, d_model, d_att):
    full = lambda r, c: pl.BlockSpec((r, c), lambda i: (0, 0))
    return [
        full(CONV_W, d_conv), full(1, d_conv), full(1, d_conv), full(1, d_conv),
        full(d_conv, d_conv), full(1, d_conv),
        pl.BlockSpec((d_att, d_model), lambda i: (0, 0)),
        pl.BlockSpec((d_conv, d_model), lambda i: (d_att // d_conv, 0)),
        full(1, d_model),
    ]


def _out_prompt_kernel(x_ref, attg_ref, cg_ref, wo_att_ref, wo_conv_ref, fg_ref, y_ref):
    y_ref[...] = _out_tail(x_ref[...], attg_ref[...], cg_ref[...], wo_att_ref, wo_conv_ref, fg_ref)


def _out_prompt(x2d, attg, convg, w_out_bf16, final_g, *, block_rows):
    rows, d_model = x2d.shape
    d_att = attg.shape[1]
    d_conv = convg.shape[1]
    row = lambda c: pl.BlockSpec((block_rows, c), lambda i: (i, 0))
    return pl.pallas_call(
        _out_prompt_kernel,
        grid=(rows // block_rows,),
        in_specs=[row(d_model), row(d_att), row(d_conv),
                  pl.BlockSpec((d_att, d_model), lambda i: (0, 0)),
                  pl.BlockSpec((d_conv, d_model), lambda i: (d_att // d_conv, 0)),
                  pl.BlockSpec((1, d_model), lambda i: (0, 0))],
        out_specs=row(d_model),
        out_shape=jax.ShapeDtypeStruct((rows, d_model), jnp.float32),
        compiler_params=pltpu.CompilerParams(
            dimension_semantics=("arbitrary",), vmem_limit_bytes=VMEM_LIMIT_BYTES),
        name="out_prompt",
    )(x2d, attg, convg, w_out_bf16, w_out_bf16, final_g)


def _mix_sample_kernel(x_ref, att_ref, sga_ref, u_ref, st_ref, sgc_ref, dw_ref, db_ref,
                       ln_g_ref, ln_b_ref, wp_ref, bp_ref, wo_att_ref, wo_conv_ref, fg_ref,
                       y_ref, st_out_ref, ext_ref, *, n_new):
    g = st_ref.shape[0]
    d_conv = st_ref.shape[2]
    hist = CONV_W - 1
    ext_ref[:, 0:hist, :] = st_ref[...]
    ext_ref[:, hist:hist + n_new, :] = u_ref[...].reshape(g, n_new, d_conv)
    c = jnp.broadcast_to(db_ref[...].reshape(1, 1, d_conv), (g, n_new, d_conv))
    for w in range(CONV_W):
        c = c + ext_ref[:, w:w + n_new, :] * dw_ref[w:w + 1, :].reshape(1, 1, d_conv)
    st_out_ref[...] = ext_ref[:, n_new:n_new + hist, :]
    act = _layer_norm_silu(c.reshape(g * n_new, d_conv), ln_g_ref, ln_b_ref)
    conv = _dot(act, wp_ref[...]) + bp_ref[...]
    y_ref[...] = _out_tail(x_ref[...], (att_ref[...] * sga_ref[...]).astype(jnp.bfloat16),
                           (conv * sgc_ref[...]).astype(jnp.bfloat16), wo_att_ref, wo_conv_ref,
                           fg_ref)


def _mix_sample(x2d, att, sga, u, state, sgc, weights, *, n_new, seqs_per_block):
    rows, d_model = x2d.shape
    d_att = att.shape[1]
    n_seq, hist, d_conv = state.shape
    block_rows = seqs_per_block * n_new
    row = lambda c: pl.BlockSpec((block_rows, c), lambda i: (i, 0))
    st_spec = pl.BlockSpec((seqs_per_block, hist, d_conv), lambda i: (i, 0, 0))
    ext_rows = -(-(hist + n_new) // SUBLANES) * SUBLANES
    kernel = functools.partial(_mix_sample_kernel, n_new=n_new)
    w_out = weights[6]
    return pl.pallas_call(
        kernel,
        grid=(n_seq // seqs_per_block,),
        in_specs=[row(d_model), row(d_att), row(d_att), row(d_conv), st_spec, row(d_conv)]
        + _weight_specs(d_conv, d_model, d_att),
        out_specs=[row(d_model), st_spec],
        out_shape=[jax.ShapeDtypeStruct((rows, d_model), jnp.float32),
                   jax.ShapeDtypeStruct(state.shape, jnp.float32)],
        scratch_shapes=[pltpu.VMEM((seqs_per_block, ext_rows, d_conv), jnp.float32)],
        compiler_params=pltpu.CompilerParams(
            dimension_semantics=("arbitrary",), vmem_limit_bytes=VMEM_LIMIT_BYTES),
        name="mix_sample",
    )(x2d, att, sga, u, state, sgc, *weights[:6], w_out, w_out, weights[7])


def kernel(x_prompt, x_sample, cache_k, cache_v, state_conv, page_table, norm_g, w_in, lambda_q1,
           lambda_k1, lambda_q2, lambda_k2, subln_g, dw_w, dw_b, conv_ln_g, conv_ln_b, w_pw2,
           b_pw2, w_out, final_norm_g):
    batch, seq, d_model = x_prompt.shape
    n_seq, n_new, _ = x_sample.shape
    depth, _, _, n_heads, v_dim = cache_k.shape
    assert depth == 1 and n_heads == N_HEADS and v_dim == V_DIM
    d_conv = dw_w.shape[2]
    bf16 = jnp.bfloat16

    norm_g2 = norm_g[0].reshape(1, d_model)
    w_in_b = w_in[0].astype(bf16)
    lams = [t[0].reshape(1, HEAD_DIM) for t in (lambda_q1, lambda_k1, lambda_q2, lambda_k2)]
    subln = subln_g[0].reshape(1, V_DIM)
    mix_w = [dw_w[0], dw_b[0].reshape(1, d_conv), conv_ln_g[0].reshape(1, d_conv),
             conv_ln_b[0].reshape(1, d_conv), w_pw2[0].astype(bf16), b_pw2[0].reshape(1, d_conv),
             w_out[0].astype(bf16), final_norm_g.reshape(1, d_model)]

    xp = x_prompt.reshape(batch * seq, d_model)
    xs = x_sample.reshape(n_seq * n_new, d_model)
    q, kb, vt, sga, u, sgc, conv_p, k, v = _in_proj_prompt(
        xp, norm_g2, w_in_b, batch=batch, seq=seq, block_rows=512)
    qs, ks, vs, sgas, us, sgcs = _in_proj_sample(xs, norm_g2, w_in_b, block_rows=512)
    att_s, convg = _sample_attn(page_table, qs, ks, vs, lams, subln, cache_k, cache_v, u, sgc,
                                mix_w[:6], n_new=n_new, prompt_seq=seq)
    attg = _prompt_attn(q, kb, vt, sga, lams, subln, batch=batch, seq=seq, block=512)
    yp = _out_prompt(xp, attg, convg, mix_w[6], mix_w[7], block_rows=1024)
    ys, conv_s = _mix_sample(xs, att_s, sgas, us, state_conv[0], sgcs, mix_w, n_new=n_new,
                             seqs_per_block=32)
    kv_shape = (1, batch, seq, N_HEADS, V_DIM)
    kvs_shape = (1, n_seq, n_new, N_HEADS, V_DIM)

    return (yp.reshape(batch, seq, d_model), ys.reshape(n_seq, n_new, d_model),
            k.reshape(kv_shape), v.reshape(kv_shape), conv_p[None],
            ks.reshape(kvs_shape), vs.reshape(kvs_shape), conv_s[None])
```

```python
import functools
import math

import jax
import jax.numpy as jnp
from jax import lax
from jax.experimental import pallas as pl
from jax.experimental.pallas import tpu as pltpu

N_HEADS = 4
HEAD_DIM = 64
V_DIM = 2 * HEAD_DIM
CONV_W = 31
EPS = 1e-5
LAM_INIT = 0.8 - 0.6 * math.exp(-0.3 * 0)

Q_SCALE = HEAD_DIM ** -0.5 * math.log2(math.e)

SUBLANES = 8
VMEM_LIMIT_BYTES = 56 * 1024 * 1024

KEY_SUB = 512
ATTN_UNROLL = 4
ONES_ROWS = 16
PAGE_SLOTS = 3
CONV_HALO = 32
CONV_CHUNK = 64
NEG_INF = float("-inf")


def _silu(x):
    return x * jax.nn.sigmoid(x)


def _dot(a, b):
    return jnp.dot(a, b, preferred_element_type=jnp.float32)


def _dot_nt(a, b):
    return lax.dot_general(a, b, (((1,), (1,)), ((), ())), preferred_element_type=jnp.float32)


def _normed_input(x_ref, g_ref):
    x = x_ref[...]
    h = x * lax.rsqrt(jnp.mean(x * x, axis=-1, keepdims=True) + EPS) * g_ref[...]
    return h.astype(jnp.bfloat16)


def _layer_norm_silu(c, ln_g_ref, ln_b_ref):
    mu = jnp.mean(c, axis=-1, keepdims=True)
    d = c - mu
    var = jnp.mean(d * d, axis=-1, keepdims=True)
    ln = d * lax.rsqrt(var + EPS) * ln_g_ref[...] + ln_b_ref[...]
    return _silu(ln).astype(jnp.bfloat16)


def _in_proj_sample_kernel(x_ref, g_ref, w_ref, q_ref, k_ref, v_ref, sga_ref, u_ref, sgc_ref):
    hb = _normed_input(x_ref, g_ref)
    c = q_ref.shape[-1]

    def col(i):
        return _dot(hb, w_ref[:, i * c:(i + 1) * c])

    q_ref[...] = col(0) * Q_SCALE
    k = col(1)
    v = col(2)
    for h in range(N_HEADS):
        hs = slice(h * V_DIM, (h + 1) * V_DIM)
        k_ref[:, h, :] = k[:, hs]
        v_ref[:, h, :] = v[:, hs]
    sga_ref[...] = _silu(col(3))
    u_ref[...] = col(4) * jax.nn.sigmoid(col(5))
    sgc_ref[...] = _silu(col(6))


def _in_proj_sample(x2d, norm_g, w_in_bf16, *, block_rows):
    rows, d_model = x2d.shape
    c = w_in_bf16.shape[1] // 7
    row_spec = pl.BlockSpec((block_rows, c), lambda i: (i, 0))
    kv_spec = pl.BlockSpec((block_rows, N_HEADS, V_DIM), lambda i: (i, 0, 0))
    f32_rows = jax.ShapeDtypeStruct((rows, c), jnp.float32)
    kv_rows = jax.ShapeDtypeStruct((rows, N_HEADS, V_DIM), jnp.float32)
    return pl.pallas_call(
        _in_proj_sample_kernel,
        grid=(rows // block_rows,),
        in_specs=[
            pl.BlockSpec((block_rows, d_model), lambda i: (i, 0)),
            pl.BlockSpec((1, d_model), lambda i: (0, 0)),
            pl.BlockSpec(w_in_bf16.shape, lambda i: (0, 0)),
        ],
        out_specs=[row_spec, kv_spec, kv_spec, row_spec, row_spec, row_spec],
        out_shape=[f32_rows, kv_rows, kv_rows, f32_rows, f32_rows, f32_rows],
        compiler_params=pltpu.CompilerParams(
            dimension_semantics=("arbitrary",), vmem_limit_bytes=VMEM_LIMIT_BYTES),
        name="in_proj_sample",
    )(x2d, norm_g, w_in_bf16)


def _in_proj_prompt_kernel(x_ref, g_ref, w_ref, q_ref, kb_ref, vt_ref, sga_ref, u_ref, sgc_ref,
                           cstate_ref, k_hbm, v_hbm, kst_ref, vst_ref, sem):
    i = pl.program_id(0)
    rows = x_ref.shape[0]
    hb = _normed_input(x_ref, g_ref)
    c = q_ref.shape[-1]

    def col(j):
        return _dot(hb, w_ref[:, j * c:(j + 1) * c])

    def kv_copies(step):
        dst_rows = pl.ds(step * rows, rows)
        copies = []
        for h in range(N_HEADS):
            copies.append(pltpu.make_async_copy(kst_ref.at[h], k_hbm.at[dst_rows, h, :], sem.at[0]))
            copies.append(pltpu.make_async_copy(vst_ref.at[h], v_hbm.at[dst_rows, h, :], sem.at[1]))
        return copies

    q_ref[...] = (col(0) * Q_SCALE).astype(jnp.bfloat16)
    k = col(1)
    v = col(2)
    kb_ref[...] = k.astype(jnp.bfloat16)
    vt_ref[...] = v.T.astype(jnp.bfloat16)

    @pl.when(i > 0)
    def _():
        for cp in kv_copies(i - 1):
            cp.wait()

    for h in range(N_HEADS):
        kst_ref[h] = k[:, h * V_DIM:(h + 1) * V_DIM]
        vst_ref[h] = v[:, h * V_DIM:(h + 1) * V_DIM]
    for cp in kv_copies(i):
        cp.start()

    sga_ref[...] = _silu(col(3)).astype(jnp.bfloat16)
    u_ref[...] = col(4) * jax.nn.sigmoid(col(5))
    sgc_ref[...] = _silu(col(6)).astype(jnp.bfloat16)
    cstate_ref[0] = u_ref[rows - (CONV_W - 1):rows, :]

    @pl.when(i == pl.num_programs(0) - 1)
    def _():
        for cp in kv_copies(i):
            cp.wait()


def _in_proj_prompt(x2d, norm_g, w_in_bf16, *, batch, seq, block_rows):
    rows, d_model = x2d.shape
    c = w_in_bf16.shape[1] // 7
    tiles_per_seq = seq // block_rows
    hist = CONV_W - 1
    full = lambda a: pl.BlockSpec(a.shape, lambda i: (0,) * a.ndim)
    row_spec = pl.BlockSpec((block_rows, c), lambda i: (i, 0))
    bf16_rows = jax.ShapeDtypeStruct((rows, c), jnp.bfloat16)
    kv_rows = jax.ShapeDtypeStruct((rows, N_HEADS, V_DIM), jnp.float32)
    any_spec = pl.BlockSpec(memory_space=pl.ANY)
    return pl.pallas_call(
        _in_proj_prompt_kernel,
        grid=(rows // block_rows,),
        in_specs=[pl.BlockSpec((block_rows, d_model), lambda i: (i, 0)), full(norm_g),
                  full(w_in_bf16)],
        out_specs=[row_spec, row_spec, pl.BlockSpec((c, block_rows), lambda i: (0, i)), row_spec,
                   row_spec, row_spec,
                   pl.BlockSpec((1, hist, c), lambda i: (i // tiles_per_seq, 0, 0)),
                   any_spec, any_spec],
        out_shape=[bf16_rows, bf16_rows, jax.ShapeDtypeStruct((c, rows), jnp.bfloat16), bf16_rows,
                   jax.ShapeDtypeStruct((rows, c), jnp.float32), bf16_rows,
                   jax.ShapeDtypeStruct((batch, hist, c), jnp.float32), kv_rows, kv_rows],
        scratch_shapes=[
            pltpu.VMEM((N_HEADS, block_rows, V_DIM), jnp.float32),
            pltpu.VMEM((N_HEADS, block_rows, V_DIM), jnp.float32),
            pltpu.SemaphoreType.DMA((2,)),
        ],
        compiler_params=pltpu.CompilerParams(
            dimension_semantics=("arbitrary",), vmem_limit_bytes=VMEM_LIMIT_BYTES),
        name="in_proj_prompt",
    )(x2d, norm_g, w_in_bf16)


def _prompt_conv_history(rows, first, ext_ref):
    @pl.when(first)
    def _():
        ext_ref[0:CONV_HALO, :] = jnp.zeros((CONV_HALO, ext_ref.shape[1]), jnp.float32)

    @pl.when(jnp.logical_not(first))
    def _():
        ext_ref[0:CONV_HALO, :] = ext_ref[rows:rows + CONV_HALO, :]


def _prompt_conv_rows(u_ref, sgc_ref, dw_ref, db_ref, ln_g_ref, ln_b_ref, wp_ref, bp_ref,
                      cg_ref, ext_ref, shift_ref):
    rows, c = u_ref.shape
    hist = CONV_W - 1
    ext_ref[CONV_HALO:, :] = u_ref[...]
    for o in range(1, SUBLANES):
        shift_ref[o - 1] = ext_ref[o:o + shift_ref.shape[1], :]

    acts = []
    for ch in range(rows // CONV_CHUNK):
        r0 = ch * CONV_CHUNK
        acc = jnp.broadcast_to(db_ref[...], (CONV_CHUNK, c))
        for w in range(CONV_W):
            start = CONV_HALO - hist + w
            o = start % SUBLANES
            a = r0 + start - o
            src = ext_ref if o == 0 else shift_ref.at[o - 1]
            acc = acc + src[a:a + CONV_CHUNK, :] * dw_ref[w:w + 1, :]
        acts.append(_layer_norm_silu(acc, ln_g_ref, ln_b_ref))
    conv = _dot(jnp.concatenate(acts, axis=0), wp_ref[...]) + bp_ref[...]
    cg_ref[...] = (conv * sgc_ref[...].astype(jnp.float32)).astype(jnp.bfloat16)


def _lam(lq1_ref, lk1_ref, lq2_ref, lk2_ref):
    s1 = jnp.sum(lq1_ref[...] * lk1_ref[...], axis=-1, keepdims=True)
    s2 = jnp.sum(lq2_ref[...] * lk2_ref[...], axis=-1, keepdims=True)
    return jnp.exp(s1) - jnp.exp(s2) + LAM_INIT


def _split_maps(q):
    lane = lax.broadcasted_iota(jnp.int32, q.shape, 1)
    zero = jnp.zeros_like(q)
    return jnp.where(lane < HEAD_DIM, q, zero), jnp.where(lane >= HEAD_DIM, q, zero)


def _prompt_attn_kernel(q_ref, k_ref, vt_ref, sga_ref, lq1_ref, lk1_ref, lq2_ref, lk2_ref, g_ref,
                        o_ref, m_ref, acc_ref, s_ref, *, block):
    qi = pl.program_id(2)
    qs = _split_maps(q_ref[...])

    m_ref[...] = jnp.full_like(m_ref, NEG_INF)
    acc_ref[...] = jnp.zeros_like(acc_ref)

    n_sub = block // KEY_SUB
    ones_rows = jnp.ones((ONES_ROWS, KEY_SUB), jnp.bfloat16)

    def fold(x, op):
        return op(x.reshape(x.shape[0] // SUBLANES, SUBLANES, x.shape[1]), axis=0)

    def sub(c):
        return slice(c * KEY_SUB, (c + 1) * KEY_SUB)

    def scores(kc, mp):
        start = pl.multiple_of(kc * block, block)
        for c in range(n_sub):
            rows = pl.ds(start + c * KEY_SUB, KEY_SUB)
            s_ref[mp, sub(c), :] = _dot_nt(k_ref[rows, :], qs[mp])

    def mask_diagonal(mp):
        for c in range(n_sub):
            s = s_ref[mp, sub(c), :]
            key = lax.broadcasted_iota(jnp.int32, s.shape, 0) + c * KEY_SUB
            qry = lax.broadcasted_iota(jnp.int32, s.shape, 1)
            s_ref[mp, sub(c), :] = jnp.where(key <= qry, s, NEG_INF)

    def accumulate(kc, mp):
        start = pl.multiple_of(kc * block, block)
        mx = None
        for c in range(n_sub):
            part = fold(s_ref[mp, sub(c), :], jnp.max)
            mx = part if mx is None else jnp.maximum(mx, part)
        m_prev = m_ref[mp]
        m_new = jnp.maximum(m_prev, jnp.max(mx, axis=0, keepdims=True))
        alpha = jnp.exp2(m_prev - m_new)
        pv = None
        for c in range(n_sub):
            p = jnp.exp2(s_ref[mp, sub(c), :] - m_new)
            cols = pl.ds(start + c * KEY_SUB, KEY_SUB)
            vt_ones = jnp.concatenate([vt_ref[:, cols], ones_rows], axis=0)
            d = _dot(vt_ones, p.astype(jnp.bfloat16))
            pv = d if pv is None else pv + d
        acc_ref[mp] = alpha * acc_ref[mp] + pv
        m_ref[mp] = m_new

    scores(0, 0)

    def pipelined(kc):
        scores(kc, 1)
        accumulate(kc, 0)
        scores(kc + 1, 0)
        accumulate(kc, 1)

    def pipelined_group(first, n):
        for j in range(n):
            pipelined(first + j)

    def unrolled(t, carry):
        pipelined_group(t * ATTN_UNROLL, ATTN_UNROLL)
        return carry

    n_unrolled = lax.shift_right_logical(qi, ATTN_UNROLL.bit_length() - 1)
    lax.fori_loop(0, n_unrolled, unrolled, 0)
    done = n_unrolled * ATTN_UNROLL
    n = ATTN_UNROLL // 2
    while n >= 1:
        @pl.when(lax.bitwise_and(qi, n) != 0)
        def _(n=n, first=done):
            pipelined_group(first, n)
        done = done + lax.bitwise_and(qi, n)
        n //= 2

    scores(qi, 1)
    for mp in range(2):
        mask_diagonal(mp)
        accumulate(qi, mp)

    lam = _lam(lq1_ref, lk1_ref, lq2_ref, lk2_ref)
    o = (acc_ref[0, :V_DIM] / acc_ref[0, V_DIM:V_DIM + 1]
         - lam * (acc_ref[1, :V_DIM] / acc_ref[1, V_DIM:V_DIM + 1]))
    y = o * lax.rsqrt(jnp.mean(o * o, axis=0, keepdims=True) + EPS)
    att = (y * g_ref[...] * (1.0 - LAM_INIT)).T
    o_ref[...] = (att * sga_ref[...].astype(jnp.float32)).astype(o_ref.dtype)


def _prompt_attn(q, kb, vt, sga, lams, subln_g, *, batch, seq, block):
    rows, d_att = q.shape
    nq = seq // block
    vec = lambda n: pl.BlockSpec((1, n), lambda b, h, i: (0, 0))
    q_spec = pl.BlockSpec((block, V_DIM), lambda b, h, i: (b * nq + i, h))
    kernel = functools.partial(_prompt_attn_kernel, block=block)
    return pl.pallas_call(
        kernel,
        grid=(batch, N_HEADS, nq),
        in_specs=[
            q_spec,
            pl.BlockSpec((seq, V_DIM), lambda b, h, i: (b, h)),
            pl.BlockSpec((V_DIM, seq), lambda b, h, i: (h, b)),
            q_spec,
            vec(HEAD_DIM), vec(HEAD_DIM), vec(HEAD_DIM), vec(HEAD_DIM),
            pl.BlockSpec((V_DIM, 1), lambda b, h, i: (0, 0)),
        ],
        out_specs=q_spec,
        out_shape=jax.ShapeDtypeStruct((rows, d_att), jnp.bfloat16),
        scratch_shapes=[
            pltpu.VMEM((2, 1, block), jnp.float32),
            pltpu.VMEM((2, V_DIM + ONES_ROWS, block), jnp.float32),
            pltpu.VMEM((2, block, block), jnp.float32),
        ],
        compiler_params=pltpu.CompilerParams(
            dimension_semantics=("arbitrary", "arbitrary", "arbitrary"),
            vmem_limit_bytes=VMEM_LIMIT_BYTES),
        name="prompt_attn",
    )(q, kb, vt, sga, *lams, subln_g.reshape(V_DIM, 1))


def _sample_attn_kernel(pt_ref, q_ref, kn_ref, vn_ref, lq1_ref, lk1_ref, lq2_ref, lk2_ref, g_ref,
                        u_ref, sgc_ref, dw_ref, db_ref, ln_g_ref, ln_b_ref, wp_ref, bp_ref,
                        ck_hbm, cv_hbm, o_ref, cg_ref, kbuf, vbuf, ext_ref, shift_ref, sem, *,
                        n_pages, page, conv_tiles_per_seq, sample_seqs):
    s_idx = pl.program_id(0)
    n_seq = pl.num_programs(0)
    n_slots = kbuf.shape[0]
    ahead = n_slots - 1
    slot = lax.rem(s_idx, n_slots)

    def page_copies(seq, slot_):
        copies = []
        for p in range(n_pages):
            pid = pt_ref[seq * n_pages + p]
            dst = pl.ds(p * page, page)
            for h in range(N_HEADS):
                copies.append(pltpu.make_async_copy(
                    ck_hbm.at[0, pid, :, h, :], kbuf.at[slot_, h, dst], sem.at[0, slot_]))
                copies.append(pltpu.make_async_copy(
                    cv_hbm.at[0, pid, :, h, :], vbuf.at[slot_, h, dst], sem.at[1, slot_]))
        return copies

    assert sample_seqs >= ahead

    @pl.when(s_idx == 0)
    def _():
        for j in range(ahead):
            for cp in page_copies(j, j):
                cp.start()

    @pl.when(s_idx + ahead < n_seq)
    def _():
        for cp in page_copies(s_idx + ahead, lax.rem(s_idx + ahead, n_slots)):
            cp.start()

    _prompt_conv_history(u_ref.shape[0], s_idx % conv_tiles_per_seq == 0, ext_ref)

    for cp in page_copies(s_idx, slot):
        cp.wait()

    _prompt_conv_rows(u_ref, sgc_ref, dw_ref, db_ref, ln_g_ref, ln_b_ref, wp_ref, bp_ref, cg_ref,
                      ext_ref, shift_ref)

    lam = _lam(lq1_ref, lk1_ref, lq2_ref, lk2_ref)
    t = q_ref.shape[0]
    row = lax.broadcasted_iota(jnp.int32, (t, t), 0)
    colk = lax.broadcasted_iota(jnp.int32, (t, t), 1)
    causal = jnp.concatenate([colk <= row] * 4, axis=0)
    bf16 = jnp.bfloat16
    pairs = [(h, h + 1) for h in range(0, N_HEADS, 2)]

    def lanes(ref_fn, pair):
        return jnp.concatenate([ref_fn(h).astype(bf16) for h in pair], axis=1)

    s_past, s_new = [], []
    for ha, hb in pairs:
        qa, qb = [jnp.concatenate(_split_maps(q_ref[:, h * V_DIM:(h + 1) * V_DIM].astype(bf16)),
                                  axis=0) for h in (ha, hb)]
        zero = jnp.zeros_like(qa)
        q_pair = jnp.concatenate([jnp.concatenate([qa, zero], axis=1),
                                  jnp.concatenate([zero, qb], axis=1)], axis=0)
        s_past.append(_dot_nt(q_pair, lanes(lambda h: kbuf[slot, h], (ha, hb))))
        s_new.append(jnp.where(
            causal, _dot_nt(q_pair, lanes(lambda h: kn_ref[:, h, :], (ha, hb))), NEG_INF))
    a_past, a_new = [], []
    for i in range(len(pairs)):
        m = jnp.maximum(jnp.max(s_past[i], axis=-1, keepdims=True),
                        jnp.max(s_new[i], axis=-1, keepdims=True))
        e_past = jnp.exp2(s_past[i] - m)
        e_new = jnp.exp2(s_new[i] - m)
        denom = jnp.sum(e_past, axis=-1, keepdims=True) + jnp.sum(e_new, axis=-1, keepdims=True)
        p_past = e_past / denom
        p_new = e_new / denom

        def differential(p):
            return jnp.concatenate([p[0:t] - lam * p[t:2 * t],
                                    p[2 * t:3 * t] - lam * p[3 * t:4 * t]], axis=0).astype(bf16)

        a_past.append(differential(p_past))
        a_new.append(differential(p_new))
    for i, pair in enumerate(pairs):
        o_pair = (_dot(a_past[i], lanes(lambda h: vbuf[slot, h], pair))
                  + _dot(a_new[i], lanes(lambda h: vn_ref[:, h, :], pair)))
        for j, h in enumerate(pair):
            o = o_pair[j * t:(j + 1) * t, j * V_DIM:(j + 1) * V_DIM]
            y = o * lax.rsqrt(jnp.mean(o * o, axis=-1, keepdims=True) + EPS)
            o_ref[:, h * V_DIM:(h + 1) * V_DIM] = y * g_ref[...] * (1.0 - LAM_INIT)


def _sample_attn(page_table, q, k_new, v_new, lams, subln_g, cache_k, cache_v, u_prompt,
                 sgc_prompt, conv_w, *, n_new, prompt_seq):
    n_seq, n_pages = page_table.shape
    _, _, page, n_heads, v_dim = cache_k.shape
    past = n_pages * page
    d_att = q.shape[1]
    prompt_rows, d_conv = u_prompt.shape
    conv_rows = prompt_rows // n_seq
    assert conv_rows * n_seq == prompt_rows and conv_rows % CONV_CHUNK == 0
    assert prompt_seq % conv_rows == 0 and conv_rows >= CONV_HALO
    row_spec = pl.BlockSpec((n_new, d_att), lambda s, pt: (s, 0))
    kv_spec = pl.BlockSpec((n_new, n_heads, v_dim), lambda s, pt: (s, 0, 0))
    conv_spec = pl.BlockSpec((conv_rows, d_conv), lambda s, pt: (s, 0))
    vec = lambda n: pl.BlockSpec((1, n), lambda s, pt: (0, 0))
    full = lambda a: pl.BlockSpec(a.shape, lambda s, pt: (0,) * a.ndim)
    kernel = functools.partial(_sample_attn_kernel, n_pages=n_pages, page=page,
                               conv_tiles_per_seq=prompt_seq // conv_rows, sample_seqs=n_seq)
    return pl.pallas_call(
        kernel,
        grid_spec=pltpu.PrefetchScalarGridSpec(
            num_scalar_prefetch=1,
            grid=(n_seq,),
            in_specs=[
                row_spec, kv_spec, kv_spec,
                vec(HEAD_DIM), vec(HEAD_DIM), vec(HEAD_DIM), vec(HEAD_DIM), vec(V_DIM),
                conv_spec, conv_spec] + [full(a) for a in conv_w] + [
                pl.BlockSpec(memory_space=pl.ANY),
                pl.BlockSpec(memory_space=pl.ANY),
            ],
            out_specs=[row_spec, conv_spec],
            scratch_shapes=[
                pltpu.VMEM((PAGE_SLOTS, n_heads, past, v_dim), jnp.float32),
                pltpu.VMEM((PAGE_SLOTS, n_heads, past, v_dim), jnp.float32),
                pltpu.VMEM((CONV_HALO + conv_rows, d_conv), jnp.float32),
                pltpu.VMEM((SUBLANES - 1, CONV_HALO + conv_rows - SUBLANES, d_conv), jnp.float32),
                pltpu.SemaphoreType.DMA((2, PAGE_SLOTS)),
            ],
        ),
        out_shape=[jax.ShapeDtypeStruct(q.shape, jnp.float32),
                   jax.ShapeDtypeStruct((prompt_rows, d_conv), jnp.bfloat16)],
        compiler_params=pltpu.CompilerParams(
            dimension_semantics=("arbitrary",), vmem_limit_bytes=VMEM_LIMIT_BYTES),
        name="sample_attn",
    )(page_table.reshape(-1), q, k_new, v_new, *lams, subln_g, u_prompt, sgc_prompt, *conv_w,
      cache_k, cache_v)


def _out_tail(x, attg, convg, wo_att_ref, wo_conv_ref, fg_ref):
    y = x + _dot(attg, wo_att_ref[...]) + _dot(convg, wo_conv_ref[...])
    return y * lax.rsqrt(jnp.mean(y * y, axis=-1, keepdims=True) + EPS) * fg_ref[...]


def _weight_specs(d_conv, d_model, d_att):
    full = lambda r, c: pl.BlockSpec((r, c), lambda i: (0, 0))
    return [
        full(CONV_W, d_conv), full(1, d_conv), full(1, d_conv), full(1, d_conv),
        full(d_conv, d_conv), full(1, d_conv),
        pl.BlockSpec((d_att, d_model), lambda i: (0, 0)),
        pl.BlockSpec((d_conv, d_model), lambda i: (d_att // d_conv, 0)),
        full(1, d_model),
    ]


def _out_prompt_kernel(x_ref, attg_ref, cg_ref, wo_att_ref, wo_conv_ref, fg_ref, y_ref):
    y_ref[...] = _out_tail(x_ref[...], attg_ref[...], cg_ref[...], wo_att_ref, wo_conv_ref, fg_ref)


def _out_prompt(x2d, attg, convg, w_out_bf16, final_g, *, block_rows):
    rows, d_model = x2d.shape
    d_att = attg.shape[1]
    d_conv = convg.shape[1]
    row = lambda c: pl.BlockSpec((block_rows, c), lambda i: (i, 0))
    return pl.pallas_call(
        _out_prompt_kernel,
        grid=(rows // block_rows,),
        in_specs=[row(d_model), row(d_att), row(d_conv),
                  pl.BlockSpec((d_att, d_model), lambda i: (0, 0)),
                  pl.BlockSpec((d_conv, d_model), lambda i: (d_att // d_conv, 0)),
                  pl.BlockSpec((1, d_model), lambda i: (0, 0))],
        out_specs=row(d_model),
        out_shape=jax.ShapeDtypeStruct((rows, d_model), jnp.float32),
        compiler_params=pltpu.CompilerParams(
            dimension_semantics=("arbitrary",), vmem_limit_bytes=VMEM_LIMIT_BYTES),
        name="out_prompt",
    )(x2d, attg, convg, w_out_bf16, w_out_bf16, final_g)


def _mix_sample_kernel(x_ref, att_ref, sga_ref, u_ref, st_ref, sgc_ref, dw_ref, db_ref,
                       ln_g_ref, ln_b_ref, wp_ref, bp_ref, wo_att_ref, wo_conv_ref, fg_ref,
                       y_ref, st_out_ref, ext_ref, *, n_new):
    g = st_ref.shape[0]
    d_conv = st_ref.shape[2]
    hist = CONV_W - 1
    ext_ref[:, 0:hist, :] = st_ref[...]
    ext_ref[:, hist:hist + n_new, :] = u_ref[...].reshape(g, n_new, d_conv)
    c = jnp.broadcast_to(db_ref[...].reshape(1, 1, d_conv), (g, n_new, d_conv))
    for w in range(CONV_W):
        c = c + ext_ref[:, w:w + n_new, :] * dw_ref[w:w + 1, :].reshape(1, 1, d_conv)
    st_out_ref[...] = ext_ref[:, n_new:n_new + hist, :]
    act = _layer_norm_silu(c.reshape(g * n_new, d_conv), ln_g_ref, ln_b_ref)
    conv = _dot(act, wp_ref[...]) + bp_ref[...]
    y_ref[...] = _out_tail(x_ref[...], (att_ref[...] * sga_ref[...]).astype(jnp.bfloat16),
                           (conv * sgc_ref[...]).astype(jnp.bfloat16), wo_att_ref, wo_conv_ref,
                           fg_ref)


def _mix_sample(x2d, att, sga, u, state, sgc, weights, *, n_new, seqs_per_block):
    rows, d_model = x2d.shape
    d_att = att.shape[1]
    n_seq, hist, d_conv = state.shape
    block_rows = seqs_per_block * n_new
    row = lambda c: pl.BlockSpec((block_rows, c), lambda i: (i, 0))
    st_spec = pl.BlockSpec((seqs_per_block, hist, d_conv), lambda i: (i, 0, 0))
    ext_rows = -(-(hist + n_new) // SUBLANES) * SUBLANES
    kernel = functools.partial(_mix_sample_kernel, n_new=n_new)
    w_out = weights[6]
    return pl.pallas_call(
        kernel,
        grid=(n_seq // seqs_per_block,),
        in_specs=[row(d_model), row(d_att), row(d_att), row(d_conv), st_spec, row(d_conv)]
        + _weight_specs(d_conv, d_model, d_att),
        out_specs=[row(d_model), st_spec],
        out_shape=[jax.ShapeDtypeStruct((rows, d_model), jnp.float32),
                   jax.ShapeDtypeStruct(state.shape, jnp.float32)],
        scratch_shapes=[pltpu.VMEM((seqs_per_block, ext_rows, d_conv), jnp.float32)],
        compiler_params=pltpu.CompilerParams(
            dimension_semantics=("arbitrary",), vmem_limit_bytes=VMEM_LIMIT_BYTES),
        name="mix_sample",
    )(x2d, att, sga, u, state, sgc, *weights[:6], w_out, w_out, weights[7])


def kernel(x_prompt, x_sample, cache_k, cache_v, state_conv, page_table, norm_g, w_in, lambda_q1,
           lambda_k1, lambda_q2, lambda_k2, subln_g, dw_w, dw_b, conv_ln_g, conv_ln_b, w_pw2,
           b_pw2, w_out, final_norm_g):
    batch, seq, d_model = x_prompt.shape
    n_seq, n_new, _ = x_sample.shape
    depth, _, _, n_heads, v_dim = cache_k.shape
    assert depth == 1 and n_heads == N_HEADS and v_dim == V_DIM
    d_conv = dw_w.shape[2]
    bf16 = jnp.bfloat16

    norm_g2 = norm_g[0].reshape(1, d_model)
    w_in_b = w_in[0].astype(bf16)
    lams = [t[0].reshape(1, HEAD_DIM) for t in (lambda_q1, lambda_k1, lambda_q2, lambda_k2)]
    subln = subln_g[0].reshape(1, V_DIM)
    mix_w = [dw_w[0], dw_b[0].reshape(1, d_conv), conv_ln_g[0].reshape(1, d_conv),
             conv_ln_b[0].reshape(1, d_conv), w_pw2[0].astype(bf16), b_pw2[0].reshape(1, d_conv),
             w_out[0].astype(bf16), final_norm_g.reshape(1, d_model)]

    xp = x_prompt.reshape(batch * seq, d_model)
    xs = x_sample.reshape(n_seq * n_new, d_model)
    q, kb, vt, sga, u, sgc, conv_p, k, v = _in_proj_prompt(
        xp, norm_g2, w_in_b, batch=batch, seq=seq, block_rows=1024)
    qs, ks, vs, sgas, us, sgcs = _in_proj_sample(xs, norm_g2, w_in_b, block_rows=512)
    att_s, convg = _sample_attn(page_table, qs, ks, vs, lams, subln, cache_k, cache_v, u, sgc,
                                mix_w[:6], n_new=n_new, prompt_seq=seq)
    attg = _prompt_attn(q, kb, vt, sga, lams, subln, batch=batch, seq=seq, block=512)
    yp = _out_prompt(xp, attg, convg, mix_w[6], mix_w[7], block_rows=1024)
    ys, conv_s = _mix_sample(xs, att_s, sgas, us, state_conv[0], sgcs, mix_w, n_new=n_new,
                             seqs_per_block=32)
    kv_shape = (1, batch, seq, N_HEADS, V_DIM)
    kvs_shape = (1, n_seq, n_new, N_HEADS, V_DIM)

    return (yp.reshape(batch, seq, d_model), ys.reshape(n_seq, n_new, d_model),
            k.reshape(kv_shape), v.reshape(kv_shape), conv_p[None],
            ks.reshape(kvs_shape), vs.reshape(kvs_shape), conv_s[None])
```

```python
import functools
import math

import jax
import jax.numpy as jnp
from jax import lax
from jax.experimental import pallas as pl
from jax.experimental.pallas import tpu as pltpu

N_HEADS = 4
HEAD_DIM = 64
V_DIM = 2 * HEAD_DIM
CONV_W = 31
EPS = 1e-5
LAM_INIT = 0.8 - 0.6 * math.exp(-0.3 * 0)

Q_SCALE = HEAD_DIM ** -0.5 * math.log2(math.e)

SUBLANES = 8
VMEM_LIMIT_BYTES = 56 * 1024 * 1024

KEY_SUB = 512
ATTN_UNROLL = 8
ONES_ROWS = 16
PAGE_SLOTS = 3
CONV_HALO = 32
CONV_CHUNK = 64
NEG_INF = float("-inf")


def _silu(x):
    return x * jax.nn.sigmoid(x)


def _dot(a, b):
    return jnp.dot(a, b, preferred_element_type=jnp.float32)


def _dot_nt(a, b):
    return lax.dot_general(a, b, (((1,), (1,)), ((), ())), preferred_element_type=jnp.float32)


def _normed_input(x_ref, g_ref):
    x = x_ref[...]
    h = x * lax.rsqrt(jnp.mean(x * x, axis=-1, keepdims=True) + EPS) * g_ref[...]
    return h.astype(jnp.bfloat16)


def _layer_norm_silu(c, ln_g_ref, ln_b_ref):
    mu = jnp.mean(c, axis=-1, keepdims=True)
    d = c - mu
    var = jnp.mean(d * d, axis=-1, keepdims=True)
    ln = d * lax.rsqrt(var + EPS) * ln_g_ref[...] + ln_b_ref[...]
    return _silu(ln).astype(jnp.bfloat16)


def _in_proj_sample_kernel(x_ref, g_ref, w_ref, q_ref, k_ref, v_ref, sga_ref, u_ref, sgc_ref):
    hb = _normed_input(x_ref, g_ref)
    c = q_ref.shape[-1]

    def col(i):
        return _dot(hb, w_ref[:, i * c:(i + 1) * c])

    q_ref[...] = col(0) * Q_SCALE
    k = col(1)
    v = col(2)
    for h in range(N_HEADS):
        hs = slice(h * V_DIM, (h + 1) * V_DIM)
        k_ref[:, h, :] = k[:, hs]
        v_ref[:, h, :] = v[:, hs]
    sga_ref[...] = _silu(col(3))
    u_ref[...] = col(4) * jax.nn.sigmoid(col(5))
    sgc_ref[...] = _silu(col(6))


def _in_proj_sample(x2d, norm_g, w_in_bf16, *, block_rows):
    rows, d_model = x2d.shape
    c = w_in_bf16.shape[1] // 7
    row_spec = pl.BlockSpec((block_rows, c), lambda i: (i, 0))
    kv_spec = pl.BlockSpec((block_rows, N_HEADS, V_DIM), lambda i: (i, 0, 0))
    f32_rows = jax.ShapeDtypeStruct((rows, c), jnp.float32)
    kv_rows = jax.ShapeDtypeStruct((rows, N_HEADS, V_DIM), jnp.float32)
    return pl.pallas_call(
        _in_proj_sample_kernel,
        grid=(rows // block_rows,),
        in_specs=[
            pl.BlockSpec((block_rows, d_model), lambda i: (i, 0)),
            pl.BlockSpec((1, d_model), lambda i: (0, 0)),
            pl.BlockSpec(w_in_bf16.shape, lambda i: (0, 0)),
        ],
        out_specs=[row_spec, kv_spec, kv_spec, row_spec, row_spec, row_spec],
        out_shape=[f32_rows, kv_rows, kv_rows, f32_rows, f32_rows, f32_rows],
        compiler_params=pltpu.CompilerParams(
            dimension_semantics=("arbitrary",), vmem_limit_bytes=VMEM_LIMIT_BYTES),
        name="in_proj_sample",
    )(x2d, norm_g, w_in_bf16)


def _in_proj_prompt_kernel(x_ref, g_ref, w_ref, q_ref, kb_ref, vt_ref, sga_ref, u_ref, sgc_ref,
                           cstate_ref, k_hbm, v_hbm, kst_ref, vst_ref, sem):
    i = pl.program_id(0)
    rows = x_ref.shape[0]
    hb = _normed_input(x_ref, g_ref)
    c = q_ref.shape[-1]

    def col(j):
        return _dot(hb, w_ref[:, j * c:(j + 1) * c])

    def kv_copies(step):
        dst_rows = pl.ds(step * rows, rows)
        copies = []
        for h in range(N_HEADS):
            copies.append(pltpu.make_async_copy(kst_ref.at[h], k_hbm.at[dst_rows, h, :], sem.at[0]))
            copies.append(pltpu.make_async_copy(vst_ref.at[h], v_hbm.at[dst_rows, h, :], sem.at[1]))
        return copies

    q_ref[...] = (col(0) * Q_SCALE).astype(jnp.bfloat16)
    k = col(1)
    v = col(2)
    kb_ref[...] = k.astype(jnp.bfloat16)
    vt_ref[...] = v.T.astype(jnp.bfloat16)

    @pl.when(i > 0)
    def _():
        for cp in kv_copies(i - 1):
            cp.wait()

    for h in range(N_HEADS):
        kst_ref[h] = k[:, h * V_DIM:(h + 1) * V_DIM]
        vst_ref[h] = v[:, h * V_DIM:(h + 1) * V_DIM]
    for cp in kv_copies(i):
        cp.start()

    sga_ref[...] = _silu(col(3)).astype(jnp.bfloat16)
    u_ref[...] = col(4) * jax.nn.sigmoid(col(5))
    sgc_ref[...] = _silu(col(6)).astype(jnp.bfloat16)
    cstate_ref[0] = u_ref[rows - (CONV_W - 1):rows, :]

    @pl.when(i == pl.num_programs(0) - 1)
    def _():
        for cp in kv_copies(i):
            cp.wait()


def _in_proj_prompt(x2d, norm_g, w_in_bf16, *, batch, seq, block_rows):
    rows, d_model = x2d.shape
    c = w_in_bf16.shape[1] // 7
    tiles_per_seq = seq // block_rows
    hist = CONV_W - 1
    full = lambda a: pl.BlockSpec(a.shape, lambda i: (0,) * a.ndim)
    row_spec = pl.BlockSpec((block_rows, c), lambda i: (i, 0))
    bf16_rows = jax.ShapeDtypeStruct((rows, c), jnp.bfloat16)
    kv_rows = jax.ShapeDtypeStruct((rows, N_HEADS, V_DIM), jnp.float32)
    any_spec = pl.BlockSpec(memory_space=pl.ANY)
    return pl.pallas_call(
        _in_proj_prompt_kernel,
        grid=(rows // block_rows,),
        in_specs=[pl.BlockSpec((block_rows, d_model), lambda i: (i, 0)), full(norm_g),
                  full(w_in_bf16)],
        out_specs=[row_spec, row_spec, pl.BlockSpec((c, block_rows), lambda i: (0, i)), row_spec,
                   row_spec, row_spec,
                   pl.BlockSpec((1, hist, c), lambda i: (i // tiles_per_seq, 0, 0)),
                   any_spec, any_spec],
        out_shape=[bf16_rows, bf16_rows, jax.ShapeDtypeStruct((c, rows), jnp.bfloat16), bf16_rows,
                   jax.ShapeDtypeStruct((rows, c), jnp.float32), bf16_rows,
                   jax.ShapeDtypeStruct((batch, hist, c), jnp.float32), kv_rows, kv_rows],
        scratch_shapes=[
            pltpu.VMEM((N_HEADS, block_rows, V_DIM), jnp.float32),
            pltpu.VMEM((N_HEADS, block_rows, V_DIM), jnp.float32),
            pltpu.SemaphoreType.DMA((2,)),
        ],
        compiler_params=pltpu.CompilerParams(
            dimension_semantics=("arbitrary",), vmem_limit_bytes=VMEM_LIMIT_BYTES),
        name="in_proj_prompt",
    )(x2d, norm_g, w_in_bf16)


def _prompt_conv_history(rows, first, ext_ref):
    @pl.when(first)
    def _():
        ext_ref[0:CONV_HALO, :] = jnp.zeros((CONV_HALO, ext_ref.shape[1]), jnp.float32)

    @pl.when(jnp.logical_not(first))
    def _():
        ext_ref[0:CONV_HALO, :] = ext_ref[rows:rows + CONV_HALO, :]


def _prompt_conv_rows(u_ref, sgc_ref, dw_ref, db_ref, ln_g_ref, ln_b_ref, wp_ref, bp_ref,
                      cg_ref, ext_ref, shift_ref):
    rows, c = u_ref.shape
    hist = CONV_W - 1
    ext_ref[CONV_HALO:, :] = u_ref[...]
    for o in range(1, SUBLANES):
        shift_ref[o - 1] = ext_ref[o:o + shift_ref.shape[1], :]

    acts = []
    for ch in range(rows // CONV_CHUNK):
        r0 = ch * CONV_CHUNK
        acc = jnp.broadcast_to(db_ref[...], (CONV_CHUNK, c))
        for w in range(CONV_W):
            start = CONV_HALO - hist + w
            o = start % SUBLANES
            a = r0 + start - o
            src = ext_ref if o == 0 else shift_ref.at[o - 1]
            acc = acc + src[a:a + CONV_CHUNK, :] * dw_ref[w:w + 1, :]
        acts.append(_layer_norm_silu(acc, ln_g_ref, ln_b_ref))
    conv = _dot(jnp.concatenate(acts, axis=0), wp_ref[...]) + bp_ref[...]
    cg_ref[...] = (conv * sgc_ref[...].astype(jnp.float32)).astype(jnp.bfloat16)


def _lam(lq1_ref, lk1_ref, lq2_ref, lk2_ref):
    s1 = jnp.sum(lq1_ref[...] * lk1_ref[...], axis=-1, keepdims=True)
    s2 = jnp.sum(lq2_ref[...] * lk2_ref[...], axis=-1, keepdims=True)
    return jnp.exp(s1) - jnp.exp(s2) + LAM_INIT


def _split_maps(q):
    lane = lax.broadcasted_iota(jnp.int32, q.shape, 1)
    zero = jnp.zeros_like(q)
    return jnp.where(lane < HEAD_DIM, q, zero), jnp.where(lane >= HEAD_DIM, q, zero)


def _prompt_attn_kernel(q_ref, k_ref, vt_ref, sga_ref, lq1_ref, lk1_ref, lq2_ref, lk2_ref, g_ref,
                        o_ref, m_ref, acc_ref, s_ref, *, block):
    qi = pl.program_id(2)
    qs = _split_maps(q_ref[...])

    m_ref[...] = jnp.full_like(m_ref, NEG_INF)
    acc_ref[...] = jnp.zeros_like(acc_ref)

    n_sub = block // KEY_SUB
    ones_rows = jnp.ones((ONES_ROWS, KEY_SUB), jnp.bfloat16)

    def fold(x, op):
        return op(x.reshape(x.shape[0] // SUBLANES, SUBLANES, x.shape[1]), axis=0)

    def sub(c):
        return slice(c * KEY_SUB, (c + 1) * KEY_SUB)

    def scores(kc, mp):
        start = pl.multiple_of(kc * block, block)
        for c in range(n_sub):
            rows = pl.ds(start + c * KEY_SUB, KEY_SUB)
            s_ref[mp, sub(c), :] = _dot_nt(k_ref[rows, :], qs[mp])

    def mask_diagonal(mp):
        for c in range(n_sub):
            s = s_ref[mp, sub(c), :]
            key = lax.broadcasted_iota(jnp.int32, s.shape, 0) + c * KEY_SUB
            qry = lax.broadcasted_iota(jnp.int32, s.shape, 1)
            s_ref[mp, sub(c), :] = jnp.where(key <= qry, s, NEG_INF)

    def accumulate(kc, mp):
        start = pl.multiple_of(kc * block, block)
        mx = None
        for c in range(n_sub):
            part = fold(s_ref[mp, sub(c), :], jnp.max)
            mx = part if mx is None else jnp.maximum(mx, part)
        m_prev = m_ref[mp]
        m_new = jnp.maximum(m_prev, jnp.max(mx, axis=0, keepdims=True))
        alpha = jnp.exp2(m_prev - m_new)
        pv = None
        for c in range(n_sub):
            p = jnp.exp2(s_ref[mp, sub(c), :] - m_new)
            cols = pl.ds(start + c * KEY_SUB, KEY_SUB)
            vt_ones = jnp.concatenate([vt_ref[:, cols], ones_rows], axis=0)
            d = _dot(vt_ones, p.astype(jnp.bfloat16))
            pv = d if pv is None else pv + d
        acc_ref[mp] = alpha * acc_ref[mp] + pv
        m_ref[mp] = m_new

    scores(0, 0)

    def pipelined(kc):
        scores(kc, 1)
        accumulate(kc, 0)
        scores(kc + 1, 0)
        accumulate(kc, 1)

    def pipelined_group(first, n):
        for j in range(n):
            pipelined(first + j)

    def unrolled(t, carry):
        pipelined_group(t * ATTN_UNROLL, ATTN_UNROLL)
        return carry

    n_unrolled = lax.shift_right_logical(qi, ATTN_UNROLL.bit_length() - 1)
    lax.fori_loop(0, n_unrolled, unrolled, 0)
    done = n_unrolled * ATTN_UNROLL
    n = ATTN_UNROLL // 2
    while n >= 1:
        @pl.when(lax.bitwise_and(qi, n) != 0)
        def _(n=n, first=done):
            pipelined_group(first, n)
        done = done + lax.bitwise_and(qi, n)
        n //= 2

    scores(qi, 1)
    for mp in range(2):
        mask_diagonal(mp)
        accumulate(qi, mp)

    lam = _lam(lq1_ref, lk1_ref, lq2_ref, lk2_ref)
    o = (acc_ref[0, :V_DIM] / acc_ref[0, V_DIM:V_DIM + 1]
         - lam * (acc_ref[1, :V_DIM] / acc_ref[1, V_DIM:V_DIM + 1]))
    y = o * lax.rsqrt(jnp.mean(o * o, axis=0, keepdims=True) + EPS)
    att = (y * g_ref[...] * (1.0 - LAM_INIT)).T
    o_ref[...] = (att * sga_ref[...].astype(jnp.float32)).astype(o_ref.dtype)


def _prompt_attn(q, kb, vt, sga, lams, subln_g, *, batch, seq, block):
    rows, d_att = q.shape
    nq = seq // block
    vec = lambda n: pl.BlockSpec((1, n), lambda b, h, i: (0, 0))
    q_spec = pl.BlockSpec((block, V_DIM), lambda b, h, i: (b * nq + i, h))
    kernel = functools.partial(_prompt_attn_kernel, block=block)
    return pl.pallas_call(
        kernel,
        grid=(batch, N_HEADS, nq),
        in_specs=[
            q_spec,
            pl.BlockSpec((seq, V_DIM), lambda b, h, i: (b, h)),
            pl.BlockSpec((V_DIM, seq), lambda b, h, i: (h, b)),
            q_spec,
            vec(HEAD_DIM), vec(HEAD_DIM), vec(HEAD_DIM), vec(HEAD_DIM),
            pl.BlockSpec((V_DIM, 1), lambda b, h, i: (0, 0)),
        ],
        out_specs=q_spec,
        out_shape=jax.ShapeDtypeStruct((rows, d_att), jnp.bfloat16),
        scratch_shapes=[
            pltpu.VMEM((2, 1, block), jnp.float32),
            pltpu.VMEM((2, V_DIM + ONES_ROWS, block), jnp.float32),
            pltpu.VMEM((2, block, block), jnp.float32),
        ],
        compiler_params=pltpu.CompilerParams(
            dimension_semantics=("arbitrary", "arbitrary", "arbitrary"),
            vmem_limit_bytes=VMEM_LIMIT_BYTES),
        name="prompt_attn",
    )(q, kb, vt, sga, *lams, subln_g.reshape(V_DIM, 1))


def _sample_attn_kernel(pt_ref, q_ref, kn_ref, vn_ref, lq1_ref, lk1_ref, lq2_ref, lk2_ref, g_ref,
                        u_ref, sgc_ref, dw_ref, db_ref, ln_g_ref, ln_b_ref, wp_ref, bp_ref,
                        ck_hbm, cv_hbm, o_ref, cg_ref, kbuf, vbuf, ext_ref, shift_ref, sem, *,
                        n_pages, page, conv_tiles_per_seq, sample_seqs):
    s_idx = pl.program_id(0)
    n_seq = pl.num_programs(0)
    n_slots = kbuf.shape[0]
    ahead = n_slots - 1
    slot = lax.rem(s_idx, n_slots)

    def page_copies(seq, slot_):
        copies = []
        for p in range(n_pages):
            pid = pt_ref[seq * n_pages + p]
            dst = pl.ds(p * page, page)
            for h in range(N_HEADS):
                copies.append(pltpu.make_async_copy(
                    ck_hbm.at[0, pid, :, h, :], kbuf.at[slot_, h, dst], sem.at[0, slot_]))
                copies.append(pltpu.make_async_copy(
                    cv_hbm.at[0, pid, :, h, :], vbuf.at[slot_, h, dst], sem.at[1, slot_]))
        return copies

    assert sample_seqs >= ahead

    @pl.when(s_idx == 0)
    def _():
        for j in range(ahead):
            for cp in page_copies(j, j):
                cp.start()

    @pl.when(s_idx + ahead < n_seq)
    def _():
        for cp in page_copies(s_idx + ahead, lax.rem(s_idx + ahead, n_slots)):
            cp.start()

    _prompt_conv_history(u_ref.shape[0], s_idx % conv_tiles_per_seq == 0, ext_ref)

    for cp in page_copies(s_idx, slot):
        cp.wait()

    _prompt_conv_rows(u_ref, sgc_ref, dw_ref, db_ref, ln_g_ref, ln_b_ref, wp_ref, bp_ref, cg_ref,
                      ext_ref, shift_ref)

    lam = _lam(lq1_ref, lk1_ref, lq2_ref, lk2_ref)
    t = q_ref.shape[0]
    row = lax.broadcasted_iota(jnp.int32, (t, t), 0)
    colk = lax.broadcasted_iota(jnp.int32, (t, t), 1)
    causal = jnp.concatenate([colk <= row] * 4, axis=0)
    bf16 = jnp.bfloat16
    pairs = [(h, h + 1) for h in range(0, N_HEADS, 2)]

    def lanes(ref_fn, pair):
        return jnp.concatenate([ref_fn(h).astype(bf16) for h in pair], axis=1)

    s_past, s_new = [], []
    for ha, hb in pairs:
        qa, qb = [jnp.concatenate(_split_maps(q_ref[:, h * V_DIM:(h + 1) * V_DIM].astype(bf16)),
                                  axis=0) for h in (ha, hb)]
        zero = jnp.zeros_like(qa)
        q_pair = jnp.concatenate([jnp.concatenate([qa, zero], axis=1),
                                  jnp.concatenate([zero, qb], axis=1)], axis=0)
        s_past.append(_dot_nt(q_pair, lanes(lambda h: kbuf[slot, h], (ha, hb))))
        s_new.append(jnp.where(
            causal, _dot_nt(q_pair, lanes(lambda h: kn_ref[:, h, :], (ha, hb))), NEG_INF))
    a_past, a_new = [], []
    for i in range(len(pairs)):
        m = jnp.maximum(jnp.max(s_past[i], axis=-1, keepdims=True),
                        jnp.max(s_new[i], axis=-1, keepdims=True))
        e_past = jnp.exp2(s_past[i] - m)
        e_new = jnp.exp2(s_new[i] - m)
        denom = jnp.sum(e_past, axis=-1, keepdims=True) + jnp.sum(e_new, axis=-1, keepdims=True)
        p_past = e_past / denom
        p_new = e_new / denom

        def differential(p):
            return jnp.concatenate([p[0:t] - lam * p[t:2 * t],
                                    p[2 * t:3 * t] - lam * p[3 * t:4 * t]], axis=0).astype(bf16)

        a_past.append(differential(p_past))
        a_new.append(differential(p_new))
    for i, pair in enumerate(pairs):
        o_pair = (_dot(a_past[i], lanes(lambda h: vbuf[slot, h], pair))
                  + _dot(a_new[i], lanes(lambda h: vn_ref[:, h, :], pair)))
        for j, h in enumerate(pair):
            o = o_pair[j * t:(j + 1) * t, j * V_DIM:(j + 1) * V_DIM]
            y = o * lax.rsqrt(jnp.mean(o * o, axis=-1, keepdims=True) + EPS)
            o_ref[:, h * V_DIM:(h + 1) * V_DIM] = y * g_ref[...] * (1.0 - LAM_INIT)


def _sample_attn(page_table, q, k_new, v_new, lams, subln_g, cache_k, cache_v, u_prompt,
                 sgc_prompt, conv_w, *, n_new, prompt_seq):
    n_seq, n_pages = page_table.shape
    _, _, page, n_heads, v_dim = cache_k.shape
    past = n_pages * page
    d_att = q.shape[1]
    prompt_rows, d_conv = u_prompt.shape
    conv_rows = prompt_rows // n_seq
    assert conv_rows * n_seq == prompt_rows and conv_rows % CONV_CHUNK == 0
    assert prompt_seq % conv_rows == 0 and conv_rows >= CONV_HALO
    row_spec = pl.BlockSpec((n_new, d_att), lambda s, pt: (s, 0))
    kv_spec = pl.BlockSpec((n_new, n_heads, v_dim), lambda s, pt: (s, 0, 0))
    conv_spec = pl.BlockSpec((conv_rows, d_conv), lambda s, pt: (s, 0))
    vec = lambda n: pl.BlockSpec((1, n), lambda s, pt: (0, 0))
    full = lambda a: pl.BlockSpec(a.shape, lambda s, pt: (0,) * a.ndim)
    kernel = functools.partial(_sample_attn_kernel, n_pages=n_pages, page=page,
                               conv_tiles_per_seq=prompt_seq // conv_rows, sample_seqs=n_seq)
    return pl.pallas_call(
        kernel,
        grid_spec=pltpu.PrefetchScalarGridSpec(
            num_scalar_prefetch=1,
            grid=(n_seq,),
            in_specs=[
                row_spec, kv_spec, kv_spec,
                vec(HEAD_DIM), vec(HEAD_DIM), vec(HEAD_DIM), vec(HEAD_DIM), vec(V_DIM),
                conv_spec, conv_spec] + [full(a) for a in conv_w] + [
                pl.BlockSpec(memory_space=pl.ANY),
                pl.BlockSpec(memory_space=pl.ANY),
            ],
            out_specs=[row_spec, conv_spec],
            scratch_shapes=[
                pltpu.VMEM((PAGE_SLOTS, n_heads, past, v_dim), jnp.float32),
                pltpu.VMEM((PAGE_SLOTS, n_heads, past, v_dim), jnp.float32),
                pltpu.VMEM((CONV_HALO + conv_rows, d_conv), jnp.float32),
                pltpu.VMEM((SUBLANES - 1, CONV_HALO + conv_rows - SUBLANES, d_conv), jnp.float32),
                pltpu.SemaphoreType.DMA((2, PAGE_SLOTS)),
            ],
        ),
        out_shape=[jax.ShapeDtypeStruct(q.shape, jnp.float32),
                   jax.ShapeDtypeStruct((prompt_rows, d_conv), jnp.bfloat16)],
        compiler_params=pltpu.CompilerParams(
            dimension_semantics=("arbitrary",), vmem_limit_bytes=VMEM_LIMIT_BYTES),
        name="sample_attn",
    )(page_table.reshape(-1), q, k_new, v_new, *lams, subln_g, u_prompt, sgc_prompt, *conv_w,
      cache_k, cache_v)


def _out_tail(x, attg, convg, wo_att_ref, wo_conv_ref, fg_ref):
    y = x + _dot(attg, wo_att_ref[...]) + _dot(convg, wo_conv_ref[...])
    return y * lax.rsqrt(jnp.mean(y * y, axis=-1, keepdims=True) + EPS) * fg_ref[...]


def _weight_specs(d_conv, d_model, d_att):
    full = lambda r, c: pl.BlockSpec((r, c), lambda i: (0, 0))
    return [
        full(CONV_W, d_conv), full(1, d_conv), full(1, d_conv), full(1, d_conv),
        full(d_conv, d_conv), full(1, d_conv),
        pl.BlockSpec((d_att, d_model), lambda i: (0, 0)),
        pl.BlockSpec((d_conv, d_model), lambda i: (d_att // d_conv, 0)),
        full(1, d_model),
    ]


def _out_prompt_kernel(x_ref, attg_ref, cg_ref, wo_att_ref, wo_conv_ref, fg_ref, y_ref):
    y_ref[...] = _out_tail(x_ref[...], attg_ref[...], cg_ref[...], wo_att_ref, wo_conv_ref, fg_ref)


def _out_prompt(x2d, attg, convg, w_out_bf16, final_g, *, block_rows):
    rows, d_model = x2d.shape
    d_att = attg.shape[1]
    d_conv = convg.shape[1]
    row = lambda c: pl.BlockSpec((block_rows, c), lambda i: (i, 0))
    return pl.pallas_call(
        _out_prompt_kernel,
        grid=(rows // block_rows,),
        in_specs=[row(d_model), row(d_att), row(d_conv),
                  pl.BlockSpec((d_att, d_model), lambda i: (0, 0)),
                  pl.BlockSpec((d_conv, d_model), lambda i: (d_att // d_conv, 0)),
                  pl.BlockSpec((1, d_model), lambda i: (0, 0))],
        out_specs=row(d_model),
        out_shape=jax.ShapeDtypeStruct((rows, d_model), jnp.float32),
        compiler_params=pltpu.CompilerParams(
            dimension_semantics=("arbitrary",), vmem_limit_bytes=VMEM_LIMIT_BYTES),
        name="out_prompt",
    )(x2d, attg, convg, w_out_bf16, w_out_bf16, final_g)


def _mix_sample_kernel(x_ref, att_ref, sga_ref, u_ref, st_ref, sgc_ref, dw_ref, db_ref,
                       ln_g_ref, ln_b_ref, wp_ref, bp_ref, wo_att_ref, wo_conv_ref, fg_ref,
                       y_ref, st_out_ref, ext_ref, *, n_new):
    g = st_ref.shape[0]
    d_conv = st_ref.shape[2]
    hist = CONV_W - 1
    ext_ref[:, 0:hist, :] = st_ref[...]
    ext_ref[:, hist:hist + n_new, :] = u_ref[...].reshape(g, n_new, d_conv)
    c = jnp.broadcast_to(db_ref[...].reshape(1, 1, d_conv), (g, n_new, d_conv))
    for w in range(CONV_W):
        c = c + ext_ref[:, w:w + n_new, :] * dw_ref[w:w + 1, :].reshape(1, 1, d_conv)
    st_out_ref[...] = ext_ref[:, n_new:n_new + hist, :]
    act = _layer_norm_silu(c.reshape(g * n_new, d_conv), ln_g_ref, ln_b_ref)
    conv = _dot(act, wp_ref[...]) + bp_ref[...]
    y_ref[...] = _out_tail(x_ref[...], (att_ref[...] * sga_ref[...]).astype(jnp.bfloat16),
                           (conv * sgc_ref[...]).astype(jnp.bfloat16), wo_att_ref, wo_conv_ref,
                           fg_ref)


def _mix_sample(x2d, att, sga, u, state, sgc, weights, *, n_new, seqs_per_block):
    rows, d_model = x2d.shape
    d_att = att.shape[1]
    n_seq, hist, d_conv = state.shape
    block_rows = seqs_per_block * n_new
    row = lambda c: pl.BlockSpec((block_rows, c), lambda i: (i, 0))
    st_spec = pl.BlockSpec((seqs_per_block, hist, d_conv), lambda i: (i, 0, 0))
    ext_rows = -(-(hist + n_new) // SUBLANES) * SUBLANES
    kernel = functools.partial(_mix_sample_kernel, n_new=n_new)
    w_out = weights[6]
    return pl.pallas_call(
        kernel,
        grid=(n_seq // seqs_per_block,),
        in_specs=[row(d_model), row(d_att), row(d_att), row(d_conv), st_spec, row(d_conv)]
        + _weight_specs(d_conv, d_model, d_att),
        out_specs=[row(d_model), st_spec],
        out_shape=[jax.ShapeDtypeStruct((rows, d_model), jnp.float32),
                   jax.ShapeDtypeStruct(state.shape, jnp.float32)],
        scratch_shapes=[pltpu.VMEM((seqs_per_block, ext_rows, d_conv), jnp.float32)],
        compiler_params=pltpu.CompilerParams(
            dimension_semantics=("arbitrary",), vmem_limit_bytes=VMEM_LIMIT_BYTES),
        name="mix_sample",
    )(x2d, att, sga, u, state, sgc, *weights[:6], w_out, w_out, weights[7])


def kernel(x_prompt, x_sample, cache_k, cache_v, state_conv, page_table, norm_g, w_in, lambda_q1,
           lambda_k1, lambda_q2, lambda_k2, subln_g, dw_w, dw_b, conv_ln_g, conv_ln_b, w_pw2,
           b_pw2, w_out, final_norm_g):
    batch, seq, d_model = x_prompt.shape
    n_seq, n_new, _ = x_sample.shape
    depth, _, _, n_heads, v_dim = cache_k.shape
    assert depth == 1 and n_heads == N_HEADS and v_dim == V_DIM
    d_conv = dw_w.shape[2]
    bf16 = jnp.bfloat16

    norm_g2 = norm_g[0].reshape(1, d_model)
    w_in_b = w_in[0].astype(bf16)
    lams = [t[0].reshape(1, HEAD_DIM) for t in (lambda_q1, lambda_k1, lambda_q2, lambda_k2)]
    subln = subln_g[0].reshape(1, V_DIM)
    mix_w = [dw_w[0], dw_b[0].reshape(1, d_conv), conv_ln_g[0].reshape(1, d_conv),
             conv_ln_b[0].reshape(1, d_conv), w_pw2[0].astype(bf16), b_pw2[0].reshape(1, d_conv),
             w_out[0].astype(bf16), final_norm_g.reshape(1, d_model)]

    xp = x_prompt.reshape(batch * seq, d_model)
    xs = x_sample.reshape(n_seq * n_new, d_model)
    q, kb, vt, sga, u, sgc, conv_p, k, v = _in_proj_prompt(
        xp, norm_g2, w_in_b, batch=batch, seq=seq, block_rows=1024)
    qs, ks, vs, sgas, us, sgcs = _in_proj_sample(xs, norm_g2, w_in_b, block_rows=512)
    att_s, convg = _sample_attn(page_table, qs, ks, vs, lams, subln, cache_k, cache_v, u, sgc,
                                mix_w[:6], n_new=n_new, prompt_seq=seq)
    attg = _prompt_attn(q, kb, vt, sga, lams, subln, batch=batch, seq=seq, block=512)
    yp = _out_prompt(xp, attg, convg, mix_w[6], mix_w[7], block_rows=1024)
    ys, conv_s = _mix_sample(xs, att_s, sgas, us, state_conv[0], sgcs, mix_w, n_new=n_new,
                             seqs_per_block=32)
    kv_shape = (1, batch, seq, N_HEADS, V_DIM)
    kvs_shape = (1, n_seq, n_new, N_HEADS, V_DIM)

    return (yp.reshape(batch, seq, d_model), ys.reshape(n_seq, n_new, d_model),
            k.reshape(kv_shape), v.reshape(kv_shape), conv_p[None],
            ks.reshape(kvs_shape), vs.reshape(kvs_shape), conv_s[None])
```

```python
import functools
import math

import jax
import jax.numpy as jnp
from jax import lax
from jax.experimental import pallas as pl
from jax.experimental.pallas import tpu as pltpu

N_HEADS = 4
HEAD_DIM = 64
V_DIM = 2 * HEAD_DIM
CONV_W = 31
EPS = 1e-5
LAM_INIT = 0.8 - 0.6 * math.exp(-0.3 * 0)

Q_SCALE = HEAD_DIM ** -0.5 * math.log2(math.e)

SUBLANES = 8
VMEM_LIMIT_BYTES = 56 * 1024 * 1024

KEY_SUB = 512
ATTN_UNROLL = 8
ONES_ROWS = 16
PAGE_SLOTS = 3
CONV_HALO = 32
CONV_CHUNK = 64
NEG_INF = float("-inf")


def _silu(x):
    return x * jax.nn.sigmoid(x)


def _dot(a, b):
    return jnp.dot(a, b, preferred_element_type=jnp.float32)


def _dot_nt(a, b):
    return lax.dot_general(a, b, (((1,), (1,)), ((), ())), preferred_element_type=jnp.float32)


def _normed_input(x_ref, g_ref):
    x = x_ref[...]
    h = x * lax.rsqrt(jnp.mean(x * x, axis=-1, keepdims=True) + EPS) * g_ref[...]
    return h.astype(jnp.bfloat16)


def _layer_norm_silu(c, ln_g_ref, ln_b_ref):
    mu = jnp.mean(c, axis=-1, keepdims=True)
    d = c - mu
    var = jnp.mean(d * d, axis=-1, keepdims=True)
    ln = d * lax.rsqrt(var + EPS) * ln_g_ref[...] + ln_b_ref[...]
    return _silu(ln).astype(jnp.bfloat16)


def _in_proj_sample_kernel(x_ref, g_ref, w_ref, q_ref, k_ref, v_ref, sga_ref, u_ref, sgc_ref):
    hb = _normed_input(x_ref, g_ref)
    c = q_ref.shape[-1]

    def col(i):
        return _dot(hb, w_ref[:, i * c:(i + 1) * c])

    q_ref[...] = col(0) * Q_SCALE
    k = col(1)
    v = col(2)
    for h in range(N_HEADS):
        hs = slice(h * V_DIM, (h + 1) * V_DIM)
        k_ref[:, h, :] = k[:, hs]
        v_ref[:, h, :] = v[:, hs]
    sga_ref[...] = _silu(col(3))
    u_ref[...] = col(4) * jax.nn.sigmoid(col(5))
    sgc_ref[...] = _silu(col(6))


def _in_proj_sample(x2d, norm_g, w_in_bf16, *, block_rows):
    rows, d_model = x2d.shape
    c = w_in_bf16.shape[1] // 7
    row_spec = pl.BlockSpec((block_rows, c), lambda i: (i, 0))
    kv_spec = pl.BlockSpec((block_rows, N_HEADS, V_DIM), lambda i: (i, 0, 0))
    f32_rows = jax.ShapeDtypeStruct((rows, c), jnp.float32)
    kv_rows = jax.ShapeDtypeStruct((rows, N_HEADS, V_DIM), jnp.float32)
    return pl.pallas_call(
        _in_proj_sample_kernel,
        grid=(rows // block_rows,),
        in_specs=[
            pl.BlockSpec((block_rows, d_model), lambda i: (i, 0)),
            pl.BlockSpec((1, d_model), lambda i: (0, 0)),
            pl.BlockSpec(w_in_bf16.shape, lambda i: (0, 0)),
        ],
        out_specs=[row_spec, kv_spec, kv_spec, row_spec, row_spec, row_spec],
        out_shape=[f32_rows, kv_rows, kv_rows, f32_rows, f32_rows, f32_rows],
        compiler_params=pltpu.CompilerParams(
            dimension_semantics=("arbitrary",), vmem_limit_bytes=VMEM_LIMIT_BYTES),
        name="in_proj_sample",
    )(x2d, norm_g, w_in_bf16)


def _in_proj_prompt_kernel(x_ref, g_ref, w_ref, q_ref, kb_ref, vt_ref, sga_ref, u_ref, sgc_ref,
                           cstate_ref, k_hbm, v_hbm, kst_ref, vst_ref, sem):
    i = pl.program_id(0)
    rows = x_ref.shape[0]
    hb = _normed_input(x_ref, g_ref)
    c = q_ref.shape[-1]

    def col(j):
        return _dot(hb, w_ref[:, j * c:(j + 1) * c])

    def kv_copies(step):
        dst_rows = pl.ds(step * rows, rows)
        copies = []
        for h in range(N_HEADS):
            copies.append(pltpu.make_async_copy(kst_ref.at[h], k_hbm.at[dst_rows, h, :], sem.at[0]))
            copies.append(pltpu.make_async_copy(vst_ref.at[h], v_hbm.at[dst_rows, h, :], sem.at[1]))
        return copies

    q_ref[...] = (col(0) * Q_SCALE).astype(jnp.bfloat16)
    k = col(1)
    v = col(2)
    kb_ref[...] = k.astype(jnp.bfloat16)
    vt_ref[...] = v.T.astype(jnp.bfloat16)

    @pl.when(i > 0)
    def _():
        for cp in kv_copies(i - 1):
            cp.wait()

    for h in range(N_HEADS):
        kst_ref[h] = k[:, h * V_DIM:(h + 1) * V_DIM]
        vst_ref[h] = v[:, h * V_DIM:(h + 1) * V_DIM]
    for cp in kv_copies(i):
        cp.start()

    sga_ref[...] = _silu(col(3)).astype(jnp.bfloat16)
    u_ref[...] = col(4) * jax.nn.sigmoid(col(5))
    sgc_ref[...] = _silu(col(6)).astype(jnp.bfloat16)
    cstate_ref[0] = u_ref[rows - (CONV_W - 1):rows, :]

    @pl.when(i == pl.num_programs(0) - 1)
    def _():
        for cp in kv_copies(i):
            cp.wait()


def _in_proj_prompt(x2d, norm_g, w_in_bf16, *, batch, seq, block_rows):
    rows, d_model = x2d.shape
    c = w_in_bf16.shape[1] // 7
    tiles_per_seq = seq // block_rows
    hist = CONV_W - 1
    full = lambda a: pl.BlockSpec(a.shape, lambda i: (0,) * a.ndim)
    row_spec = pl.BlockSpec((block_rows, c), lambda i: (i, 0))
    bf16_rows = jax.ShapeDtypeStruct((rows, c), jnp.bfloat16)
    kv_rows = jax.ShapeDtypeStruct((rows, N_HEADS, V_DIM), jnp.float32)
    any_spec = pl.BlockSpec(memory_space=pl.ANY)
    return pl.pallas_call(
        _in_proj_prompt_kernel,
        grid=(rows // block_rows,),
        in_specs=[pl.BlockSpec((block_rows, d_model), lambda i: (i, 0)), full(norm_g),
                  full(w_in_bf16)],
        out_specs=[row_spec, row_spec, pl.BlockSpec((c, block_rows), lambda i: (0, i)), row_spec,
                   row_spec, row_spec,
                   pl.BlockSpec((1, hist, c), lambda i: (i // tiles_per_seq, 0, 0)),
                   any_spec, any_spec],
        out_shape=[bf16_rows, bf16_rows, jax.ShapeDtypeStruct((c, rows), jnp.bfloat16), bf16_rows,
                   jax.ShapeDtypeStruct((rows, c), jnp.float32), bf16_rows,
                   jax.ShapeDtypeStruct((batch, hist, c), jnp.float32), kv_rows, kv_rows],
        scratch_shapes=[
            pltpu.VMEM((N_HEADS, block_rows, V_DIM), jnp.float32),
            pltpu.VMEM((N_HEADS, block_rows, V_DIM), jnp.float32),
            pltpu.SemaphoreType.DMA((2,)),
        ],
        compiler_params=pltpu.CompilerParams(
            dimension_semantics=("arbitrary",), vmem_limit_bytes=VMEM_LIMIT_BYTES),
        name="in_proj_prompt",
    )(x2d, norm_g, w_in_bf16)


def _prompt_conv_history(rows, first, ext_ref):
    @pl.when(first)
    def _():
        ext_ref[0:CONV_HALO, :] = jnp.zeros((CONV_HALO, ext_ref.shape[1]), jnp.float32)

    @pl.when(jnp.logical_not(first))
    def _():
        ext_ref[0:CONV_HALO, :] = ext_ref[rows:rows + CONV_HALO, :]


def _prompt_conv_rows(u_ref, sgc_ref, dw_ref, db_ref, ln_g_ref, ln_b_ref, wp_ref, bp_ref,
                      cg_ref, ext_ref, shift_ref):
    rows, c = u_ref.shape
    hist = CONV_W - 1
    ext_ref[CONV_HALO:, :] = u_ref[...]
    for o in range(1, SUBLANES):
        shift_ref[o - 1] = ext_ref[o:o + shift_ref.shape[1], :]

    acts = []
    for ch in range(rows // CONV_CHUNK):
        r0 = ch * CONV_CHUNK
        acc = jnp.broadcast_to(db_ref[...], (CONV_CHUNK, c))
        for w in range(CONV_W):
            start = CONV_HALO - hist + w
            o = start % SUBLANES
            a = r0 + start - o
            src = ext_ref if o == 0 else shift_ref.at[o - 1]
            acc = acc + src[a:a + CONV_CHUNK, :] * dw_ref[w:w + 1, :]
        acts.append(_layer_norm_silu(acc, ln_g_ref, ln_b_ref))
    conv = _dot(jnp.concatenate(acts, axis=0), wp_ref[...]) + bp_ref[...]
    cg_ref[...] = (conv * sgc_ref[...].astype(jnp.float32)).astype(jnp.bfloat16)


def _lam(lq1_ref, lk1_ref, lq2_ref, lk2_ref):
    s1 = jnp.sum(lq1_ref[...] * lk1_ref[...], axis=-1, keepdims=True)
    s2 = jnp.sum(lq2_ref[...] * lk2_ref[...], axis=-1, keepdims=True)
    return jnp.exp(s1) - jnp.exp(s2) + LAM_INIT


def _split_maps(q):
    lane = lax.broadcasted_iota(jnp.int32, q.shape, 1)
    zero = jnp.zeros_like(q)
    return jnp.where(lane < HEAD_DIM, q, zero), jnp.where(lane >= HEAD_DIM, q, zero)


def _prompt_attn_kernel(q_ref, k_ref, vt_ref, sga_ref, lq1_ref, lk1_ref, lq2_ref, lk2_ref, g_ref,
                        o_ref, m_ref, acc_ref, s_ref, *, block):
    qi = pl.program_id(2)
    qs = _split_maps(q_ref[...])

    m_ref[...] = jnp.full_like(m_ref, NEG_INF)
    acc_ref[...] = jnp.zeros_like(acc_ref)

    n_sub = block // KEY_SUB
    ones_rows = jnp.ones((ONES_ROWS, KEY_SUB), jnp.bfloat16)

    def fold(x, op):
        return op(x.reshape(x.shape[0] // SUBLANES, SUBLANES, x.shape[1]), axis=0)

    def sub(c):
        return slice(c * KEY_SUB, (c + 1) * KEY_SUB)

    def scores(kc, mp):
        start = pl.multiple_of(kc * block, block)
        for c in range(n_sub):
            rows = pl.ds(start + c * KEY_SUB, KEY_SUB)
            s_ref[mp, sub(c), :] = _dot_nt(k_ref[rows, :], qs[mp])

    def mask_diagonal(mp):
        for c in range(n_sub):
            s = s_ref[mp, sub(c), :]
            key = lax.broadcasted_iota(jnp.int32, s.shape, 0) + c * KEY_SUB
            qry = lax.broadcasted_iota(jnp.int32, s.shape, 1)
            s_ref[mp, sub(c), :] = jnp.where(key <= qry, s, NEG_INF)

    def accumulate(kc, mp):
        start = pl.multiple_of(kc * block, block)
        mx = None
        for c in range(n_sub):
            part = fold(s_ref[mp, sub(c), :], jnp.max)
            mx = part if mx is None else jnp.maximum(mx, part)
        m_prev = m_ref[mp]
        m_new = jnp.maximum(m_prev, jnp.max(mx, axis=0, keepdims=True))
        alpha = jnp.exp2(m_prev - m_new)
        pv = None
        for c in range(n_sub):
            p = jnp.exp2(s_ref[mp, sub(c), :] - m_new)
            cols = pl.ds(start + c * KEY_SUB, KEY_SUB)
            vt_ones = jnp.concatenate([vt_ref[:, cols], ones_rows], axis=0)
            d = _dot(vt_ones, p.astype(jnp.bfloat16))
            pv = d if pv is None else pv + d
        acc_ref[mp] = alpha * acc_ref[mp] + pv
        m_ref[mp] = m_new

    scores(0, 0)

    def pipelined(kc):
        scores(kc, 1)
        accumulate(kc, 0)
        scores(kc + 1, 0)
        accumulate(kc, 1)

    def pipelined_group(first, n):
        for j in range(n):
            pipelined(first + j)

    def unrolled(t, carry):
        pipelined_group(t * ATTN_UNROLL, ATTN_UNROLL)
        return carry

    n_unrolled = lax.shift_right_logical(qi, ATTN_UNROLL.bit_length() - 1)
    lax.fori_loop(0, n_unrolled, unrolled, 0)
    done = n_unrolled * ATTN_UNROLL
    n = ATTN_UNROLL // 2
    while n >= 1:
        @pl.when(lax.bitwise_and(qi, n) != 0)
        def _(n=n, first=done):
            pipelined_group(first, n)
        done = done + lax.bitwise_and(qi, n)
        n //= 2

    scores(qi, 1)
    for mp in range(2):
        mask_diagonal(mp)
        accumulate(qi, mp)

    lam = _lam(lq1_ref, lk1_ref, lq2_ref, lk2_ref)
    o = (acc_ref[0, :V_DIM] / acc_ref[0, V_DIM:V_DIM + 1]
         - lam * (acc_ref[1, :V_DIM] / acc_ref[1, V_DIM:V_DIM + 1]))
    y = o * lax.rsqrt(jnp.mean(o * o, axis=0, keepdims=True) + EPS)
    att = (y * g_ref[...] * (1.0 - LAM_INIT)).T
    o_ref[...] = (att * sga_ref[...].astype(jnp.float32)).astype(o_ref.dtype)


def _prompt_attn(q, kb, vt, sga, lams, subln_g, *, batch, seq, block):
    rows, d_att = q.shape
    nq = seq // block
    vec = lambda n: pl.BlockSpec((1, n), lambda b, h, i: (0, 0))
    q_spec = pl.BlockSpec((block, V_DIM), lambda b, h, i: (b * nq + i, h))
    kernel = functools.partial(_prompt_attn_kernel, block=block)
    return pl.pallas_call(
        kernel,
        grid=(batch, N_HEADS, nq),
        in_specs=[
            q_spec,
            pl.BlockSpec((seq, V_DIM), lambda b, h, i: (b, h)),
            pl.BlockSpec((V_DIM, seq), lambda b, h, i: (h, b)),
            q_spec,
            vec(HEAD_DIM), vec(HEAD_DIM), vec(HEAD_DIM), vec(HEAD_DIM),
            pl.BlockSpec((V_DIM, 1), lambda b, h, i: (0, 0)),
        ],
        out_specs=q_spec,
        out_shape=jax.ShapeDtypeStruct((rows, d_att), jnp.bfloat16),
        scratch_shapes=[
            pltpu.VMEM((2, 1, block), jnp.float32),
            pltpu.VMEM((2, V_DIM + ONES_ROWS, block), jnp.float32),
            pltpu.VMEM((2, block, block), jnp.float32),
        ],
        compiler_params=pltpu.CompilerParams(
            dimension_semantics=("arbitrary", "arbitrary", "arbitrary"),
            vmem_limit_bytes=VMEM_LIMIT_BYTES),
        name="prompt_attn",
    )(q, kb, vt, sga, *lams, subln_g.reshape(V_DIM, 1))


def _sample_attn_kernel(pt_ref, q_ref, kn_ref, vn_ref, lq1_ref, lk1_ref, lq2_ref, lk2_ref, g_ref,
                        u_ref, sgc_ref, dw_ref, db_ref, ln_g_ref, ln_b_ref, wp_ref, bp_ref,
                        ck_hbm, cv_hbm, o_ref, cg_ref, kbuf, vbuf, ext_ref, shift_ref, sem, *,
                        n_pages, page, conv_tiles_per_seq, sample_seqs):
    s_idx = pl.program_id(0)
    n_seq = pl.num_programs(0)
    n_slots = kbuf.shape[0]
    ahead = n_slots - 1
    slot = lax.rem(s_idx, n_slots)

    def page_copies(seq, slot_):
        copies = []
        for p in range(n_pages):
            pid = pt_ref[seq * n_pages + p]
            dst = pl.ds(p * page, page)
            for h in range(N_HEADS):
                copies.append(pltpu.make_async_copy(
                    ck_hbm.at[0, pid, :, h, :], kbuf.at[slot_, h, dst], sem.at[0, slot_]))
                copies.append(pltpu.make_async_copy(
                    cv_hbm.at[0, pid, :, h, :], vbuf.at[slot_, h, dst], sem.at[1, slot_]))
        return copies

    assert sample_seqs >= ahead

    @pl.when(s_idx == 0)
    def _():
        for j in range(ahead):
            for cp in page_copies(j, j):
                cp.start()

    @pl.when(s_idx + ahead < n_seq)
    def _():
        for cp in page_copies(s_idx + ahead, lax.rem(s_idx + ahead, n_slots)):
            cp.start()

    _prompt_conv_history(u_ref.shape[0], s_idx % conv_tiles_per_seq == 0, ext_ref)

    for cp in page_copies(s_idx, slot):
        cp.wait()

    _prompt_conv_rows(u_ref, sgc_ref, dw_ref, db_ref, ln_g_ref, ln_b_ref, wp_ref, bp_ref, cg_ref,
                      ext_ref, shift_ref)

    lam = _lam(lq1_ref, lk1_ref, lq2_ref, lk2_ref)
    t = q_ref.shape[0]
    row = lax.broadcasted_iota(jnp.int32, (t, t), 0)
    colk = lax.broadcasted_iota(jnp.int32, (t, t), 1)
    causal = jnp.concatenate([colk <= row] * 4, axis=0)
    bf16 = jnp.bfloat16
    pairs = [(h, h + 1) for h in range(0, N_HEADS, 2)]

    def lanes(ref_fn, pair):
        return jnp.concatenate([ref_fn(h).astype(bf16) for h in pair], axis=1)

    s_past, s_new = [], []
    for ha, hb in pairs:
        qa, qb = [jnp.concatenate(_split_maps(q_ref[:, h * V_DIM:(h + 1) * V_DIM].astype(bf16)),
                                  axis=0) for h in (ha, hb)]
        zero = jnp.zeros_like(qa)
        q_pair = jnp.concatenate([jnp.concatenate([qa, zero], axis=1),
                                  jnp.concatenate([zero, qb], axis=1)], axis=0)
        s_past.append(_dot_nt(q_pair, lanes(lambda h: kbuf[slot, h], (ha, hb))))
        s_new.append(jnp.where(
            causal, _dot_nt(q_pair, lanes(lambda h: kn_ref[:, h, :], (ha, hb))), NEG_INF))
    a_past, a_new = [], []
    for i in range(len(pairs)):
        m = jnp.maximum(jnp.max(s_past[i], axis=-1, keepdims=True),
                        jnp.max(s_new[i], axis=-1, keepdims=True))
        e_past = jnp.exp2(s_past[i] - m)
        e_new = jnp.exp2(s_new[i] - m)
        denom = jnp.sum(e_past, axis=-1, keepdims=True) + jnp.sum(e_new, axis=-1, keepdims=True)
        p_past = e_past / denom
        p_new = e_new / denom

        def differential(p):
            return jnp.concatenate([p[0:t] - lam * p[t:2 * t],
                                    p[2 * t:3 * t] - lam * p[3 * t:4 * t]], axis=0).astype(bf16)

        a_past.append(differential(p_past))
        a_new.append(differential(p_new))
    for i, pair in enumerate(pairs):
        o_pair = (_dot(a_past[i], lanes(lambda h: vbuf[slot, h], pair))
                  + _dot(a_new[i], lanes(lambda h: vn_ref[:, h, :], pair)))
        for j, h in enumerate(pair):
            o = o_pair[j * t:(j + 1) * t, j * V_DIM:(j + 1) * V_DIM]
            y = o * lax.rsqrt(jnp.mean(o * o, axis=-1, keepdims=True) + EPS)
            o_ref[:, h * V_DIM:(h + 1) * V_DIM] = y * g_ref[...] * (1.0 - LAM_INIT)


def _sample_attn(page_table, q, k_new, v_new, lams, subln_g, cache_k, cache_v, u_prompt,
                 sgc_prompt, conv_w, *, n_new, prompt_seq):
    n_seq, n_pages = page_table.shape
    _, _, page, n_heads, v_dim = cache_k.shape
    past = n_pages * page
    d_att = q.shape[1]
    prompt_rows, d_conv = u_prompt.shape
    conv_rows = prompt_rows // n_seq
    assert conv_rows * n_seq == prompt_rows and conv_rows % CONV_CHUNK == 0
    assert prompt_seq % conv_rows == 0 and conv_rows >= CONV_HALO
    row_spec = pl.BlockSpec((n_new, d_att), lambda s, pt: (s, 0))
    kv_spec = pl.BlockSpec((n_new, n_heads, v_dim), lambda s, pt: (s, 0, 0))
    conv_spec = pl.BlockSpec((conv_rows, d_conv), lambda s, pt: (s, 0))
    vec = lambda n: pl.BlockSpec((1, n), lambda s, pt: (0, 0))
    full = lambda a: pl.BlockSpec(a.shape, lambda s, pt: (0,) * a.ndim)
    kernel = functools.partial(_sample_attn_kernel, n_pages=n_pages, page=page,
                               conv_tiles_per_seq=prompt_seq // conv_rows, sample_seqs=n_seq)
    return pl.pallas_call(
        kernel,
        grid_spec=pltpu.PrefetchScalarGridSpec(
            num_scalar_prefetch=1,
            grid=(n_seq,),
            in_specs=[
                row_spec, kv_spec, kv_spec,
                vec(HEAD_DIM), vec(HEAD_DIM), vec(HEAD_DIM), vec(HEAD_DIM), vec(V_DIM),
                conv_spec, conv_spec] + [full(a) for a in conv_w] + [
                pl.BlockSpec(memory_space=pl.ANY),
                pl.BlockSpec(memory_space=pl.ANY),
            ],
            out_specs=[row_spec, conv_spec],
            scratch_shapes=[
                pltpu.VMEM((PAGE_SLOTS, n_heads, past, v_dim), jnp.float32),
                pltpu.VMEM((PAGE_SLOTS, n_heads, past, v_dim), jnp.float32),
                pltpu.VMEM((CONV_HALO + conv_rows, d_conv), jnp.float32),
                pltpu.VMEM((SUBLANES - 1, CONV_HALO + conv_rows - SUBLANES, d_conv), jnp.float32),
                pltpu.SemaphoreType.DMA((2, PAGE_SLOTS)),
            ],
        ),
        out_shape=[jax.ShapeDtypeStruct(q.shape, jnp.float32),
                   jax.ShapeDtypeStruct((prompt_rows, d_conv), jnp.bfloat16)],
        compiler_params=pltpu.CompilerParams(
            dimension_semantics=("arbitrary",), vmem_limit_bytes=VMEM_LIMIT_BYTES),
        name="sample_attn",
    )(page_table.reshape(-1), q, k_new, v_new, *lams, subln_g, u_prompt, sgc_prompt, *conv_w,
      cache_k, cache_v)


def _out_tail(x, attg, convg, wo_att_ref, wo_conv_ref, fg_ref):
    y = x + _dot(attg, wo_att_ref[...]) + _dot(convg, wo_conv_ref[...])
    return y * lax.rsqrt(jnp.mean(y * y, axis=-1, keepdims=True) + EPS) * fg_ref[...]


def _weight_specs(d_conv, d_model, d_att):
    full = lambda r, c: pl.BlockSpec((r, c), lambda i: (0, 0))
    return [
        full(CONV_W, d_conv), full(1, d_conv), full(1, d_conv), full(1, d_conv),
        full(d_conv, d_conv), full(1, d_conv),
        pl.BlockSpec((d_att, d_model), lambda i: (0, 0)),
        pl.BlockSpec((d_conv, d_model), lambda i: (d_att // d_conv, 0)),
        full(1, d_model),
    ]


def _out_prompt_kernel(x_ref, attg_ref, cg_ref, wo_att_ref, wo_conv_ref, fg_ref, y_ref):
    y_ref[...] = _out_tail(x_ref[...], attg_ref[...], cg_ref[...], wo_att_ref, wo_conv_ref, fg_ref)


def _out_prompt(x2d, attg, convg, w_out_bf16, final_g, *, block_rows):
    rows, d_model = x2d.shape
    d_att = attg.shape[1]
    d_conv = convg.shape[1]
    row = lambda c: pl.BlockSpec((block_rows, c), lambda i: (i, 0))
    return pl.pallas_call(
        _out_prompt_kernel,
        grid=(rows // block_rows,),
        in_specs=[row(d_model), row(d_att), row(d_conv),
                  pl.BlockSpec((d_att, d_model), lambda i: (0, 0)),
                  pl.BlockSpec((d_conv, d_model), lambda i: (d_att // d_conv, 0)),
                  pl.BlockSpec((1, d_model), lambda i: (0, 0))],
        out_specs=row(d_model),
        out_shape=jax.ShapeDtypeStruct((rows, d_model), jnp.float32),
        compiler_params=pltpu.CompilerParams(
            dimension_semantics=("arbitrary",), vmem_limit_bytes=VMEM_LIMIT_BYTES),
        name="out_prompt",
    )(x2d, attg, convg, w_out_bf16, w_out_bf16, final_g)


def _mix_sample_kernel(x_ref, att_ref, sga_ref, u_ref, st_ref, sgc_ref, dw_ref, db_ref,
                       ln_g_ref, ln_b_ref, wp_ref, bp_ref, wo_att_ref, wo_conv_ref, fg_ref,
                       y_ref, st_out_ref, ext_ref, *, n_new):
    g = st_ref.shape[0]
    d_conv = st_ref.shape[2]
    hist = CONV_W - 1
    ext_ref[:, 0:hist, :] = st_ref[...]
    ext_ref[:, hist:hist + n_new, :] = u_ref[...].reshape(g, n_new, d_conv)
    c = jnp.broadcast_to(db_ref[...].reshape(1, 1, d_conv), (g, n_new, d_conv))
    for w in range(CONV_W):
        c = c + ext_ref[:, w:w + n_new, :] * dw_ref[w:w + 1, :].reshape(1, 1, d_conv)
    st_out_ref[...] = ext_ref[:, n_new:n_new + hist, :]
    act = _layer_norm_silu(c.reshape(g * n_new, d_conv), ln_g_ref, ln_b_ref)
    conv = _dot(act, wp_ref[...]) + bp_ref[...]
    y_ref[...] = _out_tail(x_ref[...], (att_ref[...] * sga_ref[...]).astype(jnp.bfloat16),
                           (conv * sgc_ref[...]).astype(jnp.bfloat16), wo_att_ref, wo_conv_ref,
                           fg_ref)


def _mix_sample(x2d, att, sga, u, state, sgc, weights, *, n_new, seqs_per_block):
    rows, d_model = x2d.shape
    d_att = att.shape[1]
    n_seq, hist, d_conv = state.shape
    block_rows = seqs_per_block * n_new
    row = lambda c: pl.BlockSpec((block_rows, c), lambda i: (i, 0))
    st_spec = pl.BlockSpec((seqs_per_block, hist, d_conv), lambda i: (i, 0, 0))
    ext_rows = -(-(hist + n_new) // SUBLANES) * SUBLANES
    kernel = functools.partial(_mix_sample_kernel, n_new=n_new)
    w_out = weights[6]
    return pl.pallas_call(
        kernel,
        grid=(n_seq // seqs_per_block,),
        in_specs=[row(d_model), row(d_att), row(d_att), row(d_conv), st_spec, row(d_conv)]
        + _weight_specs(d_conv, d_model, d_att),
        out_specs=[row(d_model), st_spec],
        out_shape=[jax.ShapeDtypeStruct((rows, d_model), jnp.float32),
                   jax.ShapeDtypeStruct(state.shape, jnp.float32)],
        scratch_shapes=[pltpu.VMEM((seqs_per_block, ext_rows, d_conv), jnp.float32)],
        compiler_params=pltpu.CompilerParams(
            dimension_semantics=("arbitrary",), vmem_limit_bytes=VMEM_LIMIT_BYTES),
        name="mix_sample",
    )(x2d, att, sga, u, state, sgc, *weights[:6], w_out, w_out, weights[7])


def kernel(x_prompt, x_sample, cache_k, cache_v, state_conv, page_table, norm_g, w_in, lambda_q1,
           lambda_k1, lambda_q2, lambda_k2, subln_g, dw_w, dw_b, conv_ln_g, conv_ln_b, w_pw2,
           b_pw2, w_out, final_norm_g):
    batch, seq, d_model = x_prompt.shape
    n_seq, n_new, _ = x_sample.shape
    depth, _, _, n_heads, v_dim = cache_k.shape
    assert depth == 1 and n_heads == N_HEADS and v_dim == V_DIM
    d_conv = dw_w.shape[2]
    bf16 = jnp.bfloat16

    norm_g2 = norm_g[0].reshape(1, d_model)
    w_in_b = w_in[0].astype(bf16)
    lams = [t[0].reshape(1, HEAD_DIM) for t in (lambda_q1, lambda_k1, lambda_q2, lambda_k2)]
    subln = subln_g[0].reshape(1, V_DIM)
    mix_w = [dw_w[0], dw_b[0].reshape(1, d_conv), conv_ln_g[0].reshape(1, d_conv),
             conv_ln_b[0].reshape(1, d_conv), w_pw2[0].astype(bf16), b_pw2[0].reshape(1, d_conv),
             w_out[0].astype(bf16), final_norm_g.reshape(1, d_model)]

    xp = x_prompt.reshape(batch * seq, d_model)
    xs = x_sample.reshape(n_seq * n_new, d_model)
    q, kb, vt, sga, u, sgc, conv_p, k, v = _in_proj_prompt(
        xp, norm_g2, w_in_b, batch=batch, seq=seq, block_rows=1024)
    qs, ks, vs, sgas, us, sgcs = _in_proj_sample(xs, norm_g2, w_in_b, block_rows=512)
    att_s, convg = _sample_attn(page_table, qs, ks, vs, lams, subln, cache_k, cache_v, u, sgc,
                                mix_w[:6], n_new=n_new, prompt_seq=seq)
    attg = _prompt_attn(q, kb, vt, sga, lams, subln, batch=batch, seq=seq, block=512)
    yp = _out_prompt(xp, attg, convg, mix_w[6], mix_w[7], block_rows=2048)
    ys, conv_s = _mix_sample(xs, att_s, sgas, us, state_conv[0], sgcs, mix_w, n_new=n_new,
                             seqs_per_block=32)
    kv_shape = (1, batch, seq, N_HEADS, V_DIM)
    kvs_shape = (1, n_seq, n_new, N_HEADS, V_DIM)

    return (yp.reshape(batch, seq, d_model), ys.reshape(n_seq, n_new, d_model),
            k.reshape(kv_shape), v.reshape(kv_shape), conv_p[None],
            ks.reshape(kvs_shape), vs.reshape(kvs_shape), conv_s[None])
```

```python
import functools
import math

import jax
import jax.numpy as jnp
from jax import lax
from jax.experimental import pallas as pl
from jax.experimental.pallas import tpu as pltpu

N_HEADS = 4
HEAD_DIM = 64
V_DIM = 2 * HEAD_DIM
CONV_W = 31
EPS = 1e-5
LAM_INIT = 0.8 - 0.6 * math.exp(-0.3 * 0)

Q_SCALE = HEAD_DIM ** -0.5 * math.log2(math.e)

SUBLANES = 8
VMEM_LIMIT_BYTES = 56 * 1024 * 1024

KEY_SUB = 512
ATTN_UNROLL = 8
NORM_CHUNKS = 2
ONES_ROWS = 16
PAGE_SLOTS = 3
CONV_HALO = 32
CONV_CHUNK = 64
NEG_INF = float("-inf")


def _silu(x):
    return x * jax.nn.sigmoid(x)


def _dot(a, b):
    return jnp.dot(a, b, preferred_element_type=jnp.float32)


def _dot_nt(a, b):
    return lax.dot_general(a, b, (((1,), (1,)), ((), ())), preferred_element_type=jnp.float32)


def _normed_input(x_ref, g_ref):
    x = x_ref[...]
    h = x * lax.rsqrt(jnp.mean(x * x, axis=-1, keepdims=True) + EPS) * g_ref[...]
    return h.astype(jnp.bfloat16)


def _layer_norm_silu(c, ln_g_ref, ln_b_ref):
    mu = jnp.mean(c, axis=-1, keepdims=True)
    d = c - mu
    var = jnp.mean(d * d, axis=-1, keepdims=True)
    ln = d * lax.rsqrt(var + EPS) * ln_g_ref[...] + ln_b_ref[...]
    return _silu(ln).astype(jnp.bfloat16)


def _in_proj_sample_kernel(x_ref, g_ref, w_ref, q_ref, k_ref, v_ref, sga_ref, u_ref, sgc_ref):
    hb = _normed_input(x_ref, g_ref)
    c = q_ref.shape[-1]

    def col(i):
        return _dot(hb, w_ref[:, i * c:(i + 1) * c])

    q_ref[...] = col(0) * Q_SCALE
    k = col(1)
    v = col(2)
    for h in range(N_HEADS):
        hs = slice(h * V_DIM, (h + 1) * V_DIM)
        k_ref[:, h, :] = k[:, hs]
        v_ref[:, h, :] = v[:, hs]
    sga_ref[...] = _silu(col(3))
    u_ref[...] = col(4) * jax.nn.sigmoid(col(5))
    sgc_ref[...] = _silu(col(6))


def _in_proj_sample(x2d, norm_g, w_in_bf16, *, block_rows):
    rows, d_model = x2d.shape
    c = w_in_bf16.shape[1] // 7
    row_spec = pl.BlockSpec((block_rows, c), lambda i: (i, 0))
    kv_spec = pl.BlockSpec((block_rows, N_HEADS, V_DIM), lambda i: (i, 0, 0))
    f32_rows = jax.ShapeDtypeStruct((rows, c), jnp.float32)
    kv_rows = jax.ShapeDtypeStruct((rows, N_HEADS, V_DIM), jnp.float32)
    return pl.pallas_call(
        _in_proj_sample_kernel,
        grid=(rows // block_rows,),
        in_specs=[
            pl.BlockSpec((block_rows, d_model), lambda i: (i, 0)),
            pl.BlockSpec((1, d_model), lambda i: (0, 0)),
            pl.BlockSpec(w_in_bf16.shape, lambda i: (0, 0)),
        ],
        out_specs=[row_spec, kv_spec, kv_spec, row_spec, row_spec, row_spec],
        out_shape=[f32_rows, kv_rows, kv_rows, f32_rows, f32_rows, f32_rows],
        compiler_params=pltpu.CompilerParams(
            dimension_semantics=("arbitrary",), vmem_limit_bytes=VMEM_LIMIT_BYTES),
        name="in_proj_sample",
    )(x2d, norm_g, w_in_bf16)


def _in_proj_prompt_kernel(x_ref, g_ref, w_ref, q_ref, kb_ref, vt_ref, sga_ref, u_ref, sgc_ref,
                           cstate_ref, k_hbm, v_hbm, kst_ref, vst_ref, sem):
    i = pl.program_id(0)
    rows = x_ref.shape[0]
    c = q_ref.shape[-1]

    def kv_copies(step):
        dst_rows = pl.ds(step * rows, rows)
        copies = []
        for h in range(N_HEADS):
            copies.append(pltpu.make_async_copy(kst_ref.at[h], k_hbm.at[dst_rows, h, :], sem.at[0]))
            copies.append(pltpu.make_async_copy(vst_ref.at[h], v_hbm.at[dst_rows, h, :], sem.at[1]))
        return copies

    @pl.when(i > 0)
    def _():
        for cp in kv_copies(i - 1):
            cp.wait()

    chunk = rows // NORM_CHUNKS
    for r in range(NORM_CHUNKS):
        rs = slice(r * chunk, (r + 1) * chunk)
        x = x_ref[rs, :]
        hb = (x * lax.rsqrt(jnp.mean(x * x, axis=-1, keepdims=True) + EPS)
              * g_ref[...]).astype(jnp.bfloat16)

        def col(j):
            return _dot(hb, w_ref[:, j * c:(j + 1) * c])

        q_ref[rs, :] = (col(0) * Q_SCALE).astype(jnp.bfloat16)
        k = col(1)
        v = col(2)
        kb_ref[rs, :] = k.astype(jnp.bfloat16)
        vt_ref[:, rs] = v.T.astype(jnp.bfloat16)
        for h in range(N_HEADS):
            kst_ref[h, rs, :] = k[:, h * V_DIM:(h + 1) * V_DIM]
            vst_ref[h, rs, :] = v[:, h * V_DIM:(h + 1) * V_DIM]
        sga_ref[rs, :] = _silu(col(3)).astype(jnp.bfloat16)
        u_ref[rs, :] = col(4) * jax.nn.sigmoid(col(5))
        sgc_ref[rs, :] = _silu(col(6)).astype(jnp.bfloat16)
    for cp in kv_copies(i):
        cp.start()
    cstate_ref[0] = u_ref[rows - (CONV_W - 1):rows, :]

    @pl.when(i == pl.num_programs(0) - 1)
    def _():
        for cp in kv_copies(i):
            cp.wait()


def _in_proj_prompt(x2d, norm_g, w_in_bf16, *, batch, seq, block_rows):
    rows, d_model = x2d.shape
    c = w_in_bf16.shape[1] // 7
    tiles_per_seq = seq // block_rows
    hist = CONV_W - 1
    full = lambda a: pl.BlockSpec(a.shape, lambda i: (0,) * a.ndim)
    row_spec = pl.BlockSpec((block_rows, c), lambda i: (i, 0))
    bf16_rows = jax.ShapeDtypeStruct((rows, c), jnp.bfloat16)
    kv_rows = jax.ShapeDtypeStruct((rows, N_HEADS, V_DIM), jnp.float32)
    any_spec = pl.BlockSpec(memory_space=pl.ANY)
    return pl.pallas_call(
        _in_proj_prompt_kernel,
        grid=(rows // block_rows,),
        in_specs=[pl.BlockSpec((block_rows, d_model), lambda i: (i, 0)), full(norm_g),
                  full(w_in_bf16)],
        out_specs=[row_spec, row_spec, pl.BlockSpec((c, block_rows), lambda i: (0, i)), row_spec,
                   row_spec, row_spec,
                   pl.BlockSpec((1, hist, c), lambda i: (i // tiles_per_seq, 0, 0)),
                   any_spec, any_spec],
        out_shape=[bf16_rows, bf16_rows, jax.ShapeDtypeStruct((c, rows), jnp.bfloat16), bf16_rows,
                   jax.ShapeDtypeStruct((rows, c), jnp.float32), bf16_rows,
                   jax.ShapeDtypeStruct((batch, hist, c), jnp.float32), kv_rows, kv_rows],
        scratch_shapes=[
            pltpu.VMEM((N_HEADS, block_rows, V_DIM), jnp.float32),
            pltpu.VMEM((N_HEADS, block_rows, V_DIM), jnp.float32),
            pltpu.SemaphoreType.DMA((2,)),
        ],
        compiler_params=pltpu.CompilerParams(
            dimension_semantics=("arbitrary",), vmem_limit_bytes=VMEM_LIMIT_BYTES),
        name="in_proj_prompt",
    )(x2d, norm_g, w_in_bf16)


def _prompt_conv_history(rows, first, ext_ref):
    @pl.when(first)
    def _():
        ext_ref[0:CONV_HALO, :] = jnp.zeros((CONV_HALO, ext_ref.shape[1]), jnp.float32)

    @pl.when(jnp.logical_not(first))
    def _():
        ext_ref[0:CONV_HALO, :] = ext_ref[rows:rows + CONV_HALO, :]


def _prompt_conv_rows(u_ref, sgc_ref, dw_ref, db_ref, ln_g_ref, ln_b_ref, wp_ref, bp_ref,
                      cg_ref, ext_ref, shift_ref):
    rows, c = u_ref.shape
    hist = CONV_W - 1
    ext_ref[CONV_HALO:, :] = u_ref[...]
    for o in range(1, SUBLANES):
        shift_ref[o - 1] = ext_ref[o:o + shift_ref.shape[1], :]

    acts = []
    for ch in range(rows // CONV_CHUNK):
        r0 = ch * CONV_CHUNK
        acc = jnp.broadcast_to(db_ref[...], (CONV_CHUNK, c))
        for w in range(CONV_W):
            start = CONV_HALO - hist + w
            o = start % SUBLANES
            a = r0 + start - o
            src = ext_ref if o == 0 else shift_ref.at[o - 1]
            acc = acc + src[a:a + CONV_CHUNK, :] * dw_ref[w:w + 1, :]
        acts.append(_layer_norm_silu(acc, ln_g_ref, ln_b_ref))
    conv = _dot(jnp.concatenate(acts, axis=0), wp_ref[...]) + bp_ref[...]
    cg_ref[...] = (conv * sgc_ref[...].astype(jnp.float32)).astype(jnp.bfloat16)


def _lam(lq1_ref, lk1_ref, lq2_ref, lk2_ref):
    s1 = jnp.sum(lq1_ref[...] * lk1_ref[...], axis=-1, keepdims=True)
    s2 = jnp.sum(lq2_ref[...] * lk2_ref[...], axis=-1, keepdims=True)
    return jnp.exp(s1) - jnp.exp(s2) + LAM_INIT


def _split_maps(q):
    lane = lax.broadcasted_iota(jnp.int32, q.shape, 1)
    zero = jnp.zeros_like(q)
    return jnp.where(lane < HEAD_DIM, q, zero), jnp.where(lane >= HEAD_DIM, q, zero)


def _prompt_attn_kernel(q_ref, k_ref, vt_ref, sga_ref, lq1_ref, lk1_ref, lq2_ref, lk2_ref, g_ref,
                        o_ref, m_ref, acc_ref, s_ref, *, block):
    qi = pl.program_id(2)
    qs = _split_maps(q_ref[...])

    m_ref[...] = jnp.full_like(m_ref, NEG_INF)
    acc_ref[...] = jnp.zeros_like(acc_ref)

    n_sub = block // KEY_SUB
    ones_rows = jnp.ones((ONES_ROWS, KEY_SUB), jnp.bfloat16)

    def fold(x, op):
        return op(x.reshape(x.shape[0] // SUBLANES, SUBLANES, x.shape[1]), axis=0)

    def sub(c):
        return slice(c * KEY_SUB, (c + 1) * KEY_SUB)

    def scores(kc, mp):
        start = pl.multiple_of(kc * block, block)
        for c in range(n_sub):
            rows = pl.ds(start + c * KEY_SUB, KEY_SUB)
            s_ref[mp, sub(c), :] = _dot_nt(k_ref[rows, :], qs[mp])

    def mask_diagonal(mp):
        for c in range(n_sub):
            s = s_ref[mp, sub(c), :]
            key = lax.broadcasted_iota(jnp.int32, s.shape, 0) + c * KEY_SUB
            qry = lax.broadcasted_iota(jnp.int32, s.shape, 1)
            s_ref[mp, sub(c), :] = jnp.where(key <= qry, s, NEG_INF)

    def accumulate(kc, mp):
        start = pl.multiple_of(kc * block, block)
        mx = None
        for c in range(n_sub):
            part = fold(s_ref[mp, sub(c), :], jnp.max)
            mx = part if mx is None else jnp.maximum(mx, part)
        m_prev = m_ref[mp]
        m_new = jnp.maximum(m_prev, jnp.max(mx, axis=0, keepdims=True))
        alpha = jnp.exp2(m_prev - m_new)
        pv = None
        for c in range(n_sub):
            p = jnp.exp2(s_ref[mp, sub(c), :] - m_new)
            cols = pl.ds(start + c * KEY_SUB, KEY_SUB)
            vt_ones = jnp.concatenate([vt_ref[:, cols], ones_rows], axis=0)
            d = _dot(vt_ones, p.astype(jnp.bfloat16))
            pv = d if pv is None else pv + d
        acc_ref[mp] = alpha * acc_ref[mp] + pv
        m_ref[mp] = m_new

    scores(0, 0)

    def pipelined(kc):
        scores(kc, 1)
        accumulate(kc, 0)
        scores(kc + 1, 0)
        accumulate(kc, 1)

    def pipelined_group(first, n):
        for j in range(n):
            pipelined(first + j)

    def unrolled(t, carry):
        pipelined_group(t * ATTN_UNROLL, ATTN_UNROLL)
        return carry

    n_unrolled = lax.shift_right_logical(qi, ATTN_UNROLL.bit_length() - 1)
    lax.fori_loop(0, n_unrolled, unrolled, 0)
    done = n_unrolled * ATTN_UNROLL
    n = ATTN_UNROLL // 2
    while n >= 1:
        @pl.when(lax.bitwise_and(qi, n) != 0)
        def _(n=n, first=done):
            pipelined_group(first, n)
        done = done + lax.bitwise_and(qi, n)
        n //= 2

    scores(qi, 1)
    for mp in range(2):
        mask_diagonal(mp)
        accumulate(qi, mp)

    lam = _lam(lq1_ref, lk1_ref, lq2_ref, lk2_ref)
    o = (acc_ref[0, :V_DIM] / acc_ref[0, V_DIM:V_DIM + 1]
         - lam * (acc_ref[1, :V_DIM] / acc_ref[1, V_DIM:V_DIM + 1]))
    y = o * lax.rsqrt(jnp.mean(o * o, axis=0, keepdims=True) + EPS)
    att = (y * g_ref[...] * (1.0 - LAM_INIT)).T
    o_ref[...] = (att * sga_ref[...].astype(jnp.float32)).astype(o_ref.dtype)


def _prompt_attn(q, kb, vt, sga, lams, subln_g, *, batch, seq, block):
    rows, d_att = q.shape
    nq = seq // block
    vec = lambda n: pl.BlockSpec((1, n), lambda b, h, i: (0, 0))
    q_spec = pl.BlockSpec((block, V_DIM), lambda b, h, i: (b * nq + i, h))
    kernel = functools.partial(_prompt_attn_kernel, block=block)
    return pl.pallas_call(
        kernel,
        grid=(batch, N_HEADS, nq),
        in_specs=[
            q_spec,
            pl.BlockSpec((seq, V_DIM), lambda b, h, i: (b, h)),
            pl.BlockSpec((V_DIM, seq), lambda b, h, i: (h, b)),
            q_spec,
            vec(HEAD_DIM), vec(HEAD_DIM), vec(HEAD_DIM), vec(HEAD_DIM),
            pl.BlockSpec((V_DIM, 1), lambda b, h, i: (0, 0)),
        ],
        out_specs=q_spec,
        out_shape=jax.ShapeDtypeStruct((rows, d_att), jnp.bfloat16),
        scratch_shapes=[
            pltpu.VMEM((2, 1, block), jnp.float32),
            pltpu.VMEM((2, V_DIM + ONES_ROWS, block), jnp.float32),
            pltpu.VMEM((2, block, block), jnp.float32),
        ],
        compiler_params=pltpu.CompilerParams(
            dimension_semantics=("arbitrary", "arbitrary", "arbitrary"),
            vmem_limit_bytes=VMEM_LIMIT_BYTES),
        name="prompt_attn",
    )(q, kb, vt, sga, *lams, subln_g.reshape(V_DIM, 1))


def _sample_attn_kernel(pt_ref, q_ref, kn_ref, vn_ref, lq1_ref, lk1_ref, lq2_ref, lk2_ref, g_ref,
                        u_ref, sgc_ref, dw_ref, db_ref, ln_g_ref, ln_b_ref, wp_ref, bp_ref,
                        ck_hbm, cv_hbm, o_ref, cg_ref, kbuf, vbuf, ext_ref, shift_ref, sem, *,
                        n_pages, page, conv_tiles_per_seq, sample_seqs):
    s_idx = pl.program_id(0)
    n_seq = pl.num_programs(0)
    n_slots = kbuf.shape[0]
    ahead = n_slots - 1
    slot = lax.rem(s_idx, n_slots)

    def page_copies(seq, slot_):
        copies = []
        for p in range(n_pages):
            pid = pt_ref[seq * n_pages + p]
            dst = pl.ds(p * page, page)
            for h in range(N_HEADS):
                copies.append(pltpu.make_async_copy(
                    ck_hbm.at[0, pid, :, h, :], kbuf.at[slot_, h, dst], sem.at[0, slot_]))
                copies.append(pltpu.make_async_copy(
                    cv_hbm.at[0, pid, :, h, :], vbuf.at[slot_, h, dst], sem.at[1, slot_]))
        return copies

    assert sample_seqs >= ahead

    @pl.when(s_idx == 0)
    def _():
        for j in range(ahead):
            for cp in page_copies(j, j):
                cp.start()

    @pl.when(s_idx + ahead < n_seq)
    def _():
        for cp in page_copies(s_idx + ahead, lax.rem(s_idx + ahead, n_slots)):
            cp.start()

    _prompt_conv_history(u_ref.shape[0], s_idx % conv_tiles_per_seq == 0, ext_ref)

    for cp in page_copies(s_idx, slot):
        cp.wait()

    _prompt_conv_rows(u_ref, sgc_ref, dw_ref, db_ref, ln_g_ref, ln_b_ref, wp_ref, bp_ref, cg_ref,
                      ext_ref, shift_ref)

    lam = _lam(lq1_ref, lk1_ref, lq2_ref, lk2_ref)
    t = q_ref.shape[0]
    row = lax.broadcasted_iota(jnp.int32, (t, t), 0)
    colk = lax.broadcasted_iota(jnp.int32, (t, t), 1)
    causal = jnp.concatenate([colk <= row] * 4, axis=0)
    bf16 = jnp.bfloat16
    pairs = [(h, h + 1) for h in range(0, N_HEADS, 2)]

    def lanes(ref_fn, pair):
        return jnp.concatenate([ref_fn(h).astype(bf16) for h in pair], axis=1)

    s_past, s_new = [], []
    for ha, hb in pairs:
        qa, qb = [jnp.concatenate(_split_maps(q_ref[:, h * V_DIM:(h + 1) * V_DIM].astype(bf16)),
                                  axis=0) for h in (ha, hb)]
        zero = jnp.zeros_like(qa)
        q_pair = jnp.concatenate([jnp.concatenate([qa, zero], axis=1),
                                  jnp.concatenate([zero, qb], axis=1)], axis=0)
        s_past.append(_dot_nt(q_pair, lanes(lambda h: kbuf[slot, h], (ha, hb))))
        s_new.append(jnp.where(
            causal, _dot_nt(q_pair, lanes(lambda h: kn_ref[:, h, :], (ha, hb))), NEG_INF))
    a_past, a_new = [], []
    for i in range(len(pairs)):
        m = jnp.maximum(jnp.max(s_past[i], axis=-1, keepdims=True),
                        jnp.max(s_new[i], axis=-1, keepdims=True))
        e_past = jnp.exp2(s_past[i] - m)
        e_new = jnp.exp2(s_new[i] - m)
        denom = jnp.sum(e_past, axis=-1, keepdims=True) + jnp.sum(e_new, axis=-1, keepdims=True)
        p_past = e_past / denom
        p_new = e_new / denom

        def differential(p):
            return jnp.concatenate([p[0:t] - lam * p[t:2 * t],
                                    p[2 * t:3 * t] - lam * p[3 * t:4 * t]], axis=0).astype(bf16)

        a_past.append(differential(p_past))
        a_new.append(differential(p_new))
    for i, pair in enumerate(pairs):
        o_pair = (_dot(a_past[i], lanes(lambda h: vbuf[slot, h], pair))
                  + _dot(a_new[i], lanes(lambda h: vn_ref[:, h, :], pair)))
        for j, h in enumerate(pair):
            o = o_pair[j * t:(j + 1) * t, j * V_DIM:(j + 1) * V_DIM]
            y = o * lax.rsqrt(jnp.mean(o * o, axis=-1, keepdims=True) + EPS)
            o_ref[:, h * V_DIM:(h + 1) * V_DIM] = y * g_ref[...] * (1.0 - LAM_INIT)


def _sample_attn(page_table, q, k_new, v_new, lams, subln_g, cache_k, cache_v, u_prompt,
                 sgc_prompt, conv_w, *, n_new, prompt_seq):
    n_seq, n_pages = page_table.shape
    _, _, page, n_heads, v_dim = cache_k.shape
    past = n_pages * page
    d_att = q.shape[1]
    prompt_rows, d_conv = u_prompt.shape
    conv_rows = prompt_rows // n_seq
    assert conv_rows * n_seq == prompt_rows and conv_rows % CONV_CHUNK == 0
    assert prompt_seq % conv_rows == 0 and conv_rows >= CONV_HALO
    row_spec = pl.BlockSpec((n_new, d_att), lambda s, pt: (s, 0))
    kv_spec = pl.BlockSpec((n_new, n_heads, v_dim), lambda s, pt: (s, 0, 0))
    conv_spec = pl.BlockSpec((conv_rows, d_conv), lambda s, pt: (s, 0))
    vec = lambda n: pl.BlockSpec((1, n), lambda s, pt: (0, 0))
    full = lambda a: pl.BlockSpec(a.shape, lambda s, pt: (0,) * a.ndim)
    kernel = functools.partial(_sample_attn_kernel, n_pages=n_pages, page=page,
                               conv_tiles_per_seq=prompt_seq // conv_rows, sample_seqs=n_seq)
    return pl.pallas_call(
        kernel,
        grid_spec=pltpu.PrefetchScalarGridSpec(
            num_scalar_prefetch=1,
            grid=(n_seq,),
            in_specs=[
                row_spec, kv_spec, kv_spec,
                vec(HEAD_DIM), vec(HEAD_DIM), vec(HEAD_DIM), vec(HEAD_DIM), vec(V_DIM),
                conv_spec, conv_spec] + [full(a) for a in conv_w] + [
                pl.BlockSpec(memory_space=pl.ANY),
                pl.BlockSpec(memory_space=pl.ANY),
            ],
            out_specs=[row_spec, conv_spec],
            scratch_shapes=[
                pltpu.VMEM((PAGE_SLOTS, n_heads, past, v_dim), jnp.float32),
                pltpu.VMEM((PAGE_SLOTS, n_heads, past, v_dim), jnp.float32),
                pltpu.VMEM((CONV_HALO + conv_rows, d_conv), jnp.float32),
                pltpu.VMEM((SUBLANES - 1, CONV_HALO + conv_rows - SUBLANES, d_conv), jnp.float32),
                pltpu.SemaphoreType.DMA((2, PAGE_SLOTS)),
            ],
        ),
        out_shape=[jax.ShapeDtypeStruct(q.shape, jnp.float32),
                   jax.ShapeDtypeStruct((prompt_rows, d_conv), jnp.bfloat16)],
        compiler_params=pltpu.CompilerParams(
            dimension_semantics=("arbitrary",), vmem_limit_bytes=VMEM_LIMIT_BYTES),
        name="sample_attn",
    )(page_table.reshape(-1), q, k_new, v_new, *lams, subln_g, u_prompt, sgc_prompt, *conv_w,
      cache_k, cache_v)


def _out_tail(x, attg, convg, wo_att_ref, wo_conv_ref, fg_ref):
    y = x + _dot(attg, wo_att_ref[...]) + _dot(convg, wo_conv_ref[...])
    return y * lax.rsqrt(jnp.mean(y * y, axis=-1, keepdims=True) + EPS) * fg_ref[...]


def _weight_specs(d_conv, d_model, d_att):
    full = lambda r, c: pl.BlockSpec((r, c), lambda i: (0, 0))
    return [
        full(CONV_W, d_conv), full(1, d_conv), full(1, d_conv), full(1, d_conv),
        full(d_conv, d_conv), full(1, d_conv),
        pl.BlockSpec((d_att, d_model), lambda i: (0, 0)),
        pl.BlockSpec((d_conv, d_model), lambda i: (d_att // d_conv, 0)),
        full(1, d_model),
    ]


def _out_prompt_kernel(x_ref, attg_ref, cg_ref, wo_att_ref, wo_conv_ref, fg_ref, y_ref):
    y_ref[...] = _out_tail(x_ref[...], attg_ref[...], cg_ref[...], wo_att_ref, wo_conv_ref, fg_ref)


def _out_prompt(x2d, attg, convg, w_out_bf16, final_g, *, block_rows):
    rows, d_model = x2d.shape
    d_att = attg.shape[1]
    d_conv = convg.shape[1]
    row = lambda c: pl.BlockSpec((block_rows, c), lambda i: (i, 0))
    return pl.pallas_call(
        _out_prompt_kernel,
        grid=(rows // block_rows,),
        in_specs=[row(d_model), row(d_att), row(d_conv),
                  pl.BlockSpec((d_att, d_model), lambda i: (0, 0)),
                  pl.BlockSpec((d_conv, d_model), lambda i: (d_att // d_conv, 0)),
                  pl.BlockSpec((1, d_model), lambda i: (0, 0))],
        out_specs=row(d_model),
        out_shape=jax.ShapeDtypeStruct((rows, d_model), jnp.float32),
        compiler_params=pltpu.CompilerParams(
            dimension_semantics=("arbitrary",), vmem_limit_bytes=VMEM_LIMIT_BYTES),
        name="out_prompt",
    )(x2d, attg, convg, w_out_bf16, w_out_bf16, final_g)


def _mix_sample_kernel(x_ref, att_ref, sga_ref, u_ref, st_ref, sgc_ref, dw_ref, db_ref,
                       ln_g_ref, ln_b_ref, wp_ref, bp_ref, wo_att_ref, wo_conv_ref, fg_ref,
                       y_ref, st_out_ref, ext_ref, *, n_new):
    g = st_ref.shape[0]
    d_conv = st_ref.shape[2]
    hist = CONV_W - 1
    ext_ref[:, 0:hist, :] = st_ref[...]
    ext_ref[:, hist:hist + n_new, :] = u_ref[...].reshape(g, n_new, d_conv)
    c = jnp.broadcast_to(db_ref[...].reshape(1, 1, d_conv), (g, n_new, d_conv))
    for w in range(CONV_W):
        c = c + ext_ref[:, w:w + n_new, :] * dw_ref[w:w + 1, :].reshape(1, 1, d_conv)
    st_out_ref[...] = ext_ref[:, n_new:n_new + hist, :]
    act = _layer_norm_silu(c.reshape(g * n_new, d_conv), ln_g_ref, ln_b_ref)
    conv = _dot(act, wp_ref[...]) + bp_ref[...]
    y_ref[...] = _out_tail(x_ref[...], (att_ref[...] * sga_ref[...]).astype(jnp.bfloat16),
                           (conv * sgc_ref[...]).astype(jnp.bfloat16), wo_att_ref, wo_conv_ref,
                           fg_ref)


def _mix_sample(x2d, att, sga, u, state, sgc, weights, *, n_new, seqs_per_block):
    rows, d_model = x2d.shape
    d_att = att.shape[1]
    n_seq, hist, d_conv = state.shape
    block_rows = seqs_per_block * n_new
    row = lambda c: pl.BlockSpec((block_rows, c), lambda i: (i, 0))
    st_spec = pl.BlockSpec((seqs_per_block, hist, d_conv), lambda i: (i, 0, 0))
    ext_rows = -(-(hist + n_new) // SUBLANES) * SUBLANES
    kernel = functools.partial(_mix_sample_kernel, n_new=n_new)
    w_out = weights[6]
    return pl.pallas_call(
        kernel,
        grid=(n_seq // seqs_per_block,),
        in_specs=[row(d_model), row(d_att), row(d_att), row(d_conv), st_spec, row(d_conv)]
        + _weight_specs(d_conv, d_model, d_att),
        out_specs=[row(d_model), st_spec],
        out_shape=[jax.ShapeDtypeStruct((rows, d_model), jnp.float32),
                   jax.ShapeDtypeStruct(state.shape, jnp.float32)],
        scratch_shapes=[pltpu.VMEM((seqs_per_block, ext_rows, d_conv), jnp.float32)],
        compiler_params=pltpu.CompilerParams(
            dimension_semantics=("arbitrary",), vmem_limit_bytes=VMEM_LIMIT_BYTES),
        name="mix_sample",
    )(x2d, att, sga, u, state, sgc, *weights[:6], w_out, w_out, weights[7])


def kernel(x_prompt, x_sample, cache_k, cache_v, state_conv, page_table, norm_g, w_in, lambda_q1,
           lambda_k1, lambda_q2, lambda_k2, subln_g, dw_w, dw_b, conv_ln_g, conv_ln_b, w_pw2,
           b_pw2, w_out, final_norm_g):
    batch, seq, d_model = x_prompt.shape
    n_seq, n_new, _ = x_sample.shape
    depth, _, _, n_heads, v_dim = cache_k.shape
    assert depth == 1 and n_heads == N_HEADS and v_dim == V_DIM
    d_conv = dw_w.shape[2]
    bf16 = jnp.bfloat16

    norm_g2 = norm_g[0].reshape(1, d_model)
    w_in_b = w_in[0].astype(bf16)
    lams = [t[0].reshape(1, HEAD_DIM) for t in (lambda_q1, lambda_k1, lambda_q2, lambda_k2)]
    subln = subln_g[0].reshape(1, V_DIM)
    mix_w = [dw_w[0], dw_b[0].reshape(1, d_conv), conv_ln_g[0].reshape(1, d_conv),
             conv_ln_b[0].reshape(1, d_conv), w_pw2[0].astype(bf16), b_pw2[0].reshape(1, d_conv),
             w_out[0].astype(bf16), final_norm_g.reshape(1, d_model)]

    xp = x_prompt.reshape(batch * seq, d_model)
    xs = x_sample.reshape(n_seq * n_new, d_model)
    q, kb, vt, sga, u, sgc, conv_p, k, v = _in_proj_prompt(
        xp, norm_g2, w_in_b, batch=batch, seq=seq, block_rows=1024)
    qs, ks, vs, sgas, us, sgcs = _in_proj_sample(xs, norm_g2, w_in_b, block_rows=512)
    att_s, convg = _sample_attn(page_table, qs, ks, vs, lams, subln, cache_k, cache_v, u, sgc,
                                mix_w[:6], n_new=n_new, prompt_seq=seq)
    attg = _prompt_attn(q, kb, vt, sga, lams, subln, batch=batch, seq=seq, block=512)
    yp = _out_prompt(xp, attg, convg, mix_w[6], mix_w[7], block_rows=2048)
    ys, conv_s = _mix_sample(xs, att_s, sgas, us, state_conv[0], sgcs, mix_w, n_new=n_new,
                             seqs_per_block=32)
    kv_shape = (1, batch, seq, N_HEADS, V_DIM)
    kvs_shape = (1, n_seq, n_new, N_HEADS, V_DIM)

    return (yp.reshape(batch, seq, d_model), ys.reshape(n_seq, n_new, d_model),
            k.reshape(kv_shape), v.reshape(kv_shape), conv_p[None],
            ks.reshape(kvs_shape), vs.reshape(kvs_shape), conv_s[None])
```
